```python
import jax
import jax.numpy as jnp
from jax import lax
import numpy as np

D_MODEL = 2048
BATCH = 1
SEQ = 8192
DEPTH = 2

HEAD_DIM = 128
ROPE_THETA = 10000.0
ATTN_SCALE = HEAD_DIM ** -0.5
QB = 128
NEG = -1e30
TINY = 1e-30

DIL_PATTERNS = ((128, 1), (512, 4), (2048, 16))
A_HEADS_PER_GROUP = 4
A_HEADS = A_HEADS_PER_GROUP * len(DIL_PATTERNS)
B_HEADS = 4
C_HEADS = 4
IDX_HEADS = 16
IDX_DIM = 64
DSA_TOPK = 256
D_HEADS = 4
CMP_LEN = 32
CMP_STRIDE = 16
CMP_HIDDEN = 256
SLC_BLOCK = 64
SLC_COUNT = 16
WIN = 512
FORCE = 1e9
N_BRANCH = 4
BRANCH_WIDTH = 4 * HEAD_DIM
FFN_CEIL = -(-8 * D_MODEL // 3)
FFN_HIDDEN = -(-FFN_CEIL // 256) * 256

IN_SPLITS = (
    A_HEADS * HEAD_DIM, A_HEADS * HEAD_DIM, A_HEADS * HEAD_DIM,
    B_HEADS * HEAD_DIM, B_HEADS * HEAD_DIM, B_HEADS * HEAD_DIM, B_HEADS,
    C_HEADS * HEAD_DIM, C_HEADS * HEAD_DIM, C_HEADS * HEAD_DIM,
    IDX_HEADS * IDX_DIM, IDX_DIM, IDX_HEADS,
    D_HEADS * HEAD_DIM,
    HEAD_DIM, HEAD_DIM, HEAD_DIM, HEAD_DIM, HEAD_DIM, HEAD_DIM,
    3 * D_HEADS,
)
IN_COLS = sum(IN_SPLITS)

kernel_name = 'hybrid_gated_dilated_fox_dsa_nsa_block'


def rms_norm(x, g, eps=1e-6):
    xf = x.astype(jnp.float32)
    y = xf * lax.rsqrt(jnp.mean(xf * xf, axis=-1, keepdims=True) + eps)
    return (y * g.astype(jnp.float32)).astype(x.dtype)


def rope(x, pos):
    half = x.shape[-1] // 2
    inv_freq = ROPE_THETA ** (-jnp.arange(half, dtype=jnp.float32) / half)
    ang = pos.astype(jnp.float32)[:, None] * inv_freq[None, :]
    cos = jnp.cos(ang)[None, :, None, :]
    sin = jnp.sin(ang)[None, :, None, :]
    x1 = x[..., :half].astype(jnp.float32)
    x2 = x[..., half:].astype(jnp.float32)
    return jnp.concatenate([x1 * cos - x2 * sin, x2 * cos + x1 * sin], axis=-1).astype(x.dtype)


def masked_softmax(logits, mask):
    logits = jnp.where(mask, logits.astype(jnp.float32), NEG)
    m = jnp.max(logits, axis=-1, keepdims=True)
    p = jnp.where(mask, jnp.exp(logits - m), 0.0)
    s = jnp.maximum(jnp.sum(p, axis=-1, keepdims=True), TINY)
    return p / s, (m + jnp.log(s))[..., 0]


def sweep_query_blocks(fn, seq_len):
    out = lax.map(fn, jnp.arange(seq_len // QB))
    out = jnp.moveaxis(out, 0, 1)
    return out.reshape((out.shape[0], seq_len) + out.shape[3:])


def dilated_group(q, k, v, window, dilation):
    B, S, H, Dh = q.shape
    span = window // dilation
    L = S // dilation
    nb = -(-L // QB)
    Lp = nb * QB

    def to_blocks(a):
        a = a.reshape(B, L, dilation, H, Dh).transpose(0, 2, 1, 3, 4)
        a = jnp.pad(a, ((0, 0), (0, 0), (0, Lp - L), (0, 0), (0, 0)))
        return a.reshape(B, dilation, nb, QB, H, Dh)

    def with_prev(a):
        prev = jnp.pad(a, ((0, 0), (0, 0), (1, 0), (0, 0), (0, 0), (0, 0)))[:, :, :-1]
        return jnp.concatenate([prev, a], axis=3)

    qb = to_blocks(q)
    kb = with_prev(to_blocks(k))
    vb = with_prev(to_blocks(v))
    logits = jnp.einsum('brnqhd,brnkhd->brnhqk', qb, kb).astype(jnp.float32) * ATTN_SCALE
    qi = jnp.arange(QB)[:, None]
    ki = jnp.arange(2 * QB)[None, :]
    dist = QB + qi - ki
    band = (dist >= 0) & (dist <= span)
    has_prev = (jnp.arange(nb) > 0)[:, None, None] | (ki >= QB)[None]
    mask = (band[None] & has_prev)[None, None, :, None]
    probs, lse = masked_softmax(logits, mask)
    out = jnp.einsum('brnhqk,brnkhd->brnqhd', probs.astype(v.dtype), vb)
    out = out.reshape(B, dilation, Lp, H, Dh)[:, :, :L].transpose(0, 2, 1, 3, 4).reshape(B, S, H, Dh)
    lse = lse.transpose(0, 1, 2, 4, 3).reshape(B, dilation, Lp, H)[:, :, :L]
    lse = lse.transpose(0, 2, 1, 3).reshape(B, S, H)
    return out, lse


def mixer_a(q, k, v):
    outs, lses = [], []
    for g, (window, dilation) in enumerate(DIL_PATTERNS):
        sl = slice(g * A_HEADS_PER_GROUP, (g + 1) * A_HEADS_PER_GROUP)
        o, l = dilated_group(q[:, :, sl], k[:, :, sl], v[:, :, sl], window, dilation)
        outs.append(o)
        lses.append(l)
    wts = jax.nn.softmax(jnp.stack(lses, axis=0), axis=0)
    return jnp.einsum('gbsh,gbshd->bshd', wts.astype(v.dtype), jnp.stack(outs, axis=0))


def mixer_b(q, k, v, f_logit):
    B, S, H, Dh = q.shape
    c = lax.cumsum(jax.nn.log_sigmoid(f_logit.astype(jnp.float32)), axis=1)
    c_k = c.transpose(0, 2, 1)[:, :, None, :]
    kpos = jnp.arange(S)

    def block(b):
        q0 = b * QB
        qb = lax.dynamic_slice_in_dim(q, q0, QB, axis=1)
        cb = lax.dynamic_slice_in_dim(c, q0, QB, axis=1).transpose(0, 2, 1)[..., None]
        logits = jnp.einsum('bqhd,bkhd->bhqk', qb, k).astype(jnp.float32) * ATTN_SCALE + cb - c_k
        mask = (q0 + jnp.arange(QB))[:, None] >= kpos[None, :]
        probs, _ = masked_softmax(logits, mask)
        return jnp.einsum('bhqk,bkhd->bqhd', probs.astype(v.dtype), v)

    return sweep_query_blocks(block, S)


def mixer_c(q, k, v, q_idx, k_idx, w_idx):
    B, S, H, Dh = q.shape
    topk = min(DSA_TOPK, S // 4)
    kpos = jnp.arange(S)
    bi = jnp.arange(B)[:, None, None]

    def block(b):
        q0 = b * QB
        qpos = q0 + jnp.arange(QB)
        qib = lax.dynamic_slice_in_dim(q_idx, q0, QB, axis=1)
        wib = lax.dynamic_slice_in_dim(w_idx, q0, QB, axis=1).astype(jnp.float32) * IDX_HEADS ** -0.5
        rel = jax.nn.relu(jnp.einsum('bqhe,bke->bqhk', qib, k_idx).astype(jnp.float32) * IDX_DIM ** -0.5)
        score = jnp.einsum('bqh,bqhk->bqk', wib, rel)
        score = jnp.where((kpos[None, :] <= qpos[:, None])[None], score, NEG)
        _, idx = lax.top_k(score, topk)
        ks = k[bi, idx]
        vs = v[bi, idx]
        qb = lax.dynamic_slice_in_dim(q, q0, QB, axis=1)
        logits = jnp.einsum('bqhd,bqkhd->bqhk', qb, ks).astype(jnp.float32) * ATTN_SCALE
        valid = (idx <= qpos[None, :, None])[:, :, None, :]
        probs, _ = masked_softmax(logits, valid)
        return jnp.einsum('bqhk,bqkhd->bqhd', probs.astype(v.dtype), vs)

    return sweep_query_blocks(block, S)


def compress(x, pe, w1, w2):
    B, S, Dh = x.shape
    nc = (S - CMP_LEN) // CMP_STRIDE + 1
    idx = jnp.arange(nc)[:, None] * CMP_STRIDE + jnp.arange(CMP_LEN)[None, :]
    blocks = (x[:, idx] + pe).reshape(B, nc, CMP_LEN * Dh)
    return jax.nn.gelu(blocks @ w1) @ w2


def mixer_d(q, kc, vc, ks, vs, kw, vw, gate_logit, pe, w1, w2):
    B, S, H, Dh = q.shape
    k_cmp = compress(kc, pe[0], w1[0], w2[0])
    v_cmp = compress(vc, pe[1], w1[1], w2[1])
    nc = k_cmp.shape[1]
    cmp_end = jnp.arange(nc) * CMP_STRIDE + CMP_LEN - 1
    n_slc = S // SLC_BLOCK
    n_sel = min(SLC_COUNT, n_slc)
    ci = jnp.arange(nc)[:, None]
    sj = jnp.arange(n_slc)[None, :]
    overlap = ((ci * CMP_STRIDE < (sj + 1) * SLC_BLOCK) &
               (ci * CMP_STRIDE + CMP_LEN > sj * SLC_BLOCK)).astype(jnp.float32)
    ks_blk = ks.reshape(B, n_slc, SLC_BLOCK, Dh)
    vs_blk = vs.reshape(B, n_slc, SLC_BLOCK, Dh)
    kw_pad = jnp.pad(kw, ((0, 0), (WIN, 0), (0, 0)))
    vw_pad = jnp.pad(vw, ((0, 0), (WIN, 0), (0, 0)))
    gates = jax.nn.sigmoid(gate_logit.astype(jnp.float32))
    bi = jnp.arange(B)[:, None, None]
    jpos = jnp.arange(n_slc)[None, :]

    def block(b):
        q0 = b * QB
        qpos = q0 + jnp.arange(QB)
        qb = lax.dynamic_slice_in_dim(q, q0, QB, axis=1)
        lc = jnp.einsum('bqhd,bnd->bqhn', qb, k_cmp).astype(jnp.float32) * ATTN_SCALE
        mc = (cmp_end[None, :] <= qpos[:, None])[None, :, None, :]
        pc, _ = masked_softmax(lc, mc)
        o_cmp = jnp.einsum('bqhn,bnd->bqhd', pc.astype(qb.dtype), v_cmp)
        imp = jnp.einsum('bqhn,nj->bqj', pc, overlap)
        cur = (qpos // SLC_BLOCK)[:, None]
        admissible = jpos * SLC_BLOCK <= qpos[:, None]
        forced = (jpos == 0) | (jpos == cur) | (jpos == cur - 1)
        imp = jnp.where(forced[None], FORCE, jnp.where(admissible[None], imp, NEG))
        _, sel = lax.top_k(imp, n_sel)
        kg = ks_blk[bi, sel]
        vg = vs_blk[bi, sel].reshape(B, QB, n_sel * SLC_BLOCK, Dh)
        ls = jnp.einsum('bqhd,bqnld->bqhnl', qb, kg).astype(jnp.float32) * ATTN_SCALE
        ls = ls.reshape(B, QB, H, n_sel * SLC_BLOCK)
        tokpos = sel[..., None] * SLC_BLOCK + jnp.arange(SLC_BLOCK)
        ms = (tokpos <= qpos[None, :, None, None]).reshape(B, QB, 1, n_sel * SLC_BLOCK)
        ps, _ = masked_softmax(ls, ms)
        o_slc = jnp.einsum('bqhk,bqkd->bqhd', ps.astype(qb.dtype), vg)
        kwb = lax.dynamic_slice_in_dim(kw_pad, q0, QB + WIN, axis=1)
        vwb = lax.dynamic_slice_in_dim(vw_pad, q0, QB + WIN, axis=1)
        lw = jnp.einsum('bqhd,bkd->bqhk', qb, kwb).astype(jnp.float32) * ATTN_SCALE
        kpos = q0 - WIN + jnp.arange(QB + WIN)
        dist = qpos[:, None] - kpos[None, :]
        mw = ((dist >= 0) & (dist < WIN) & (kpos[None, :] >= 0))[None, :, None, :]
        pw, _ = masked_softmax(lw, mw)
        o_win = jnp.einsum('bqhk,bkd->bqhd', pw.astype(qb.dtype), vwb)
        gb = lax.dynamic_slice_in_dim(gates, q0, QB, axis=1)
        out = gb[..., 0:1] * o_cmp + gb[..., 1:2] * o_slc + gb[..., 2:3] * o_win
        return out.astype(qb.dtype)

    return sweep_query_blocks(block, S)


def hybrid_layer(x, norm_mix, w_in, w_gate, b_gate, b_f, cmp_pe, cmp_w1, cmp_w2,
                 w_branch, w_out, norm_ffn, w_ffn_in, w_ffn_out):
    B, S, _ = x.shape
    pos = jnp.arange(S)
    h = rms_norm(x, norm_mix)
    (aq, ak, av, bq, bk, bv, bfl, cq, ck, cv, cqi, cki, cwi,
     dq, dkc, dvc, dks, dvs, dkw, dvw, dg) = jnp.split(
        h @ w_in, np.cumsum(IN_SPLITS)[:-1].tolist(), axis=-1)

    def heads(t):
        return t.reshape(B, S, -1, HEAD_DIM)

    def rope1(t):
        return rope(t[:, :, None, :], pos)[:, :, 0]

    o_a = mixer_a(rope(heads(aq), pos), rope(heads(ak), pos), heads(av))
    o_b = mixer_b(heads(bq), heads(bk), heads(bv), bfl + b_f)
    o_c = mixer_c(rope(heads(cq), pos), rope(heads(ck), pos), heads(cv),
                  rope(cqi.reshape(B, S, IDX_HEADS, IDX_DIM), pos), rope1(cki), cwi)
    o_d = mixer_d(rope(heads(dq), pos), rope1(dkc), dvc, rope1(dks), dvs, rope1(dkw), dvw,
                  dg.reshape(B, S, D_HEADS, 3), cmp_pe, cmp_w1, cmp_w2)

    branches = jnp.stack([o_a, o_b, o_c, o_d], axis=2).reshape(B, S, N_BRANCH, BRANCH_WIDTH)
    y = jnp.einsum('bsnc,ncd->bsnd', branches, w_branch)
    g = jax.nn.sigmoid(jnp.einsum('bsd,dne->bsne', h, w_gate) + b_gate)
    x = x + jnp.einsum('bsnd,bsnd->bsd', g, y) @ w_out

    h2 = rms_norm(x, norm_ffn)
    gate, up = jnp.split(h2 @ w_ffn_in, 2, axis=-1)
    return x + (jax.nn.silu(gate) * up) @ w_ffn_out


def setup_inputs(seed: int = 0) -> dict:
    key = jax.random.key(seed)
    ks = jax.random.split(key, 16)

    def dense(k, shape, fan_in):
        return jax.random.normal(k, shape, jnp.float32) * fan_in ** -0.5

    def gain(k, shape):
        return 1.0 + 0.05 * jax.random.normal(k, shape, jnp.float32)

    return {
        'x': jax.random.normal(ks[0], (BATCH, SEQ, D_MODEL), jnp.float32),
        'norm_mix': gain(ks[1], (DEPTH, D_MODEL)),
        'w_in': dense(ks[2], (DEPTH, D_MODEL, IN_COLS), D_MODEL),
        'w_gate': dense(ks[3], (DEPTH, D_MODEL, N_BRANCH, D_MODEL), D_MODEL),
        'b_gate': 0.02 * jax.random.normal(ks[4], (DEPTH, N_BRANCH, D_MODEL), jnp.float32),
        'b_f': 3.0 + 0.1 * jax.random.normal(ks[5], (DEPTH, B_HEADS), jnp.float32),
        'cmp_pe': 0.1 * jax.random.normal(ks[6], (DEPTH, 2, CMP_LEN, HEAD_DIM), jnp.float32),
        'cmp_w1': dense(ks[7], (DEPTH, 2, CMP_LEN * HEAD_DIM, CMP_HIDDEN), CMP_LEN * HEAD_DIM),
        'cmp_w2': dense(ks[8], (DEPTH, 2, CMP_HIDDEN, HEAD_DIM), CMP_HIDDEN),
        'w_branch': dense(ks[9], (DEPTH, N_BRANCH, BRANCH_WIDTH, D_MODEL), BRANCH_WIDTH),
        'w_out': dense(ks[10], (DEPTH, D_MODEL, D_MODEL), D_MODEL),
        'norm_ffn': gain(ks[11], (DEPTH, D_MODEL)),
        'w_ffn_in': dense(ks[12], (DEPTH, D_MODEL, 2 * FFN_HIDDEN), D_MODEL),
        'w_ffn_out': dense(ks[13], (DEPTH, FFN_HIDDEN, D_MODEL), FFN_HIDDEN),
        'norm_final': gain(ks[14], (D_MODEL,)),
    }


def reference(x, norm_mix, w_in, w_gate, b_gate, b_f, cmp_pe, cmp_w1, cmp_w2,
              w_branch, w_out, norm_ffn, w_ffn_in, w_ffn_out, norm_final):
    for l in range(DEPTH):
        x = hybrid_layer(x, norm_mix[l], w_in[l], w_gate[l], b_gate[l], b_f[l],
                         cmp_pe[l], cmp_w1[l], cmp_w2[l], w_branch[l], w_out[l],
                         norm_ffn[l], w_ffn_in[l], w_ffn_out[l])
    return rms_norm(x, norm_final)
```

```python
import functools

import jax
import jax.numpy as jnp
import numpy as np
from jax import lax
from jax.experimental import pallas as pl
from jax.experimental.pallas import tpu as pltpu

F32 = jnp.float32
BF16 = jnp.bfloat16
I32 = jnp.int32

HEAD_DIM = 128
ROPE_THETA = 10000.0
ATTN_SCALE = HEAD_DIM ** -0.5
QB = 128
NEG = -1e30
TINY = 1e-30
DIL_PATTERNS = ((128, 1), (512, 4), (2048, 16))
A_HEADS_PER_GROUP = 4
A_HEADS = A_HEADS_PER_GROUP * len(DIL_PATTERNS)
B_HEADS = 4
C_HEADS = 4
IDX_HEADS = 16
IDX_DIM = 64
DSA_TOPK = 256
D_HEADS = 4
CMP_LEN = 32
CMP_STRIDE = 16
CMP_HIDDEN = 256
SLC_BLOCK = 64
SLC_COUNT = 16
WIN = 512
FORCE = 1e9
N_BRANCH = 4
BRANCH_WIDTH = 4 * HEAD_DIM

LANE = 128
VMEM_LIMIT = 56 * 1024 * 1024

INT_MIN = -2 ** 31
_NEG_BITS = int(np.array(NEG, np.float32).view(np.int32))
NEG_KEY = _NEG_BITS ^ 0x7FFFFFFF


def _params(sem):
    return pltpu.CompilerParams(dimension_semantics=sem, vmem_limit_bytes=VMEM_LIMIT)


def _dot(a, b):
    return jnp.dot(a, b, preferred_element_type=F32)


def _dot_nt(a, b):
    return lax.dot_general(a, b, (((1,), (1,)), ((), ())), preferred_element_type=F32)


def _rmsnorm_body(x_ref, g_ref, o_ref):
    x = x_ref[...]
    ms = jnp.mean(x * x, axis=-1, keepdims=True)
    o_ref[...] = (x * lax.rsqrt(ms + 1e-6) * g_ref[...]).astype(o_ref.dtype)


def _rmsnorm(x, g, out_dtype, tm=512):
    S, D = x.shape
    return pl.pallas_call(
        _rmsnorm_body,
        grid=(S // tm,),
        in_specs=[pl.BlockSpec((tm, D), lambda i: (i, 0)),
                  pl.BlockSpec((1, D), lambda i: (0, 0))],
        out_specs=pl.BlockSpec((tm, D), lambda i: (i, 0)),
        out_shape=jax.ShapeDtypeStruct((S, D), out_dtype),
        compiler_params=_params(("parallel",)),
        name="rmsnorm",
    )(x, g.reshape(1, D))


def _rope128(acc, cos, sin):
    parts = []
    for c in range(acc.shape[1] // LANE):
        blk = acc[:, c * LANE:(c + 1) * LANE]
        parts.append(blk * cos + pltpu.roll(blk, LANE // 2, axis=1) * sin)
    return parts[0] if len(parts) == 1 else jnp.concatenate(parts, axis=1)


def _rope64(acc, cos, sin_a, sin_b):
    parts = []
    for c in range(acc.shape[1] // LANE):
        blk = acc[:, c * LANE:(c + 1) * LANE]
        parts.append(blk * cos + pltpu.roll(blk, LANE - IDX_DIM // 2, axis=1) * sin_a
                     + pltpu.roll(blk, IDX_DIM // 2, axis=1) * sin_b)
    return parts[0] if len(parts) == 1 else jnp.concatenate(parts, axis=1)


def _mm_body(*refs, mode):
    a_ref, b_ref = refs[0], refs[1]
    o_ref = refs[-1]
    acc = _dot(a_ref[...], b_ref[...])
    if mode == "rope128":
        acc = _rope128(acc, refs[2][...], refs[3][...])
    elif mode == "rope64":
        acc = _rope64(acc, refs[2][...], refs[3][...], refs[4][...])
    elif mode == "resid":
        acc = refs[2][...] + acc
    o_ref[...] = acc.astype(o_ref.dtype)


def _mm(a, b, *, tm, tn, out_dtype, mode="plain", extras=(), name="mm"):
    M, K = a.shape
    N = b.shape[1]
    assert M % tm == 0 and N % tn == 0, (M, N, tm, tn)
    in_specs = [pl.BlockSpec((tm, K), lambda j, i: (i, 0)),
                pl.BlockSpec((K, tn), lambda j, i: (0, j))]
    if mode in ("rope128", "rope64"):
        in_specs += [pl.BlockSpec((tm, LANE), lambda j, i: (i, 0)) for _ in extras]
    elif mode == "resid":
        in_specs += [pl.BlockSpec((tm, tn), lambda j, i: (i, j))]
    return pl.pallas_call(
        functools.partial(_mm_body, mode=mode),
        grid=(N // tn, M // tm),
        in_specs=in_specs,
        out_specs=pl.BlockSpec((tm, tn), lambda j, i: (i, j)),
        out_shape=jax.ShapeDtypeStruct((M, N), out_dtype),
        compiler_params=_params(("parallel", "parallel")),
        name=name,
    )(a, b, *extras)


def _swiglu_body(a_ref, bg_ref, bu_ref, o_ref):
    a = a_ref[...]
    g = _dot(a, bg_ref[...])
    u = _dot(a, bu_ref[...])
    o_ref[...] = (g * jax.nn.sigmoid(g) * u).astype(o_ref.dtype)


def _mm_swiglu(a, b, *, tm, tn):
    M, K = a.shape
    H = b.shape[1] // 2
    nj = H // tn
    assert H % tn == 0 and M % tm == 0
    return pl.pallas_call(
        _swiglu_body,
        grid=(nj, M // tm),
        in_specs=[pl.BlockSpec((tm, K), lambda j, i: (i, 0)),
                  pl.BlockSpec((K, tn), lambda j, i: (0, j)),
                  pl.BlockSpec((K, tn), lambda j, i: (0, j + nj))],
        out_specs=pl.BlockSpec((tm, tn), lambda j, i: (i, j)),
        out_shape=jax.ShapeDtypeStruct((M, H), BF16),
        compiler_params=_params(("parallel", "parallel")),
        name="ffn_in_swiglu",
    )(a, b, b)


def _gate_merge_body(h_ref, wg0, wg1, wg2, wg3, bg_ref, br0, br1, br2, br3, wb_ref, o_ref):
    h = h_ref[...]
    acc = None
    for n, (wg, br) in enumerate(((wg0, br0), (wg1, br1), (wg2, br2), (wg3, br3))):
        gl = _dot(h, wg[...]) + bg_ref[n]
        y = _dot(br[...], wb_ref[n])
        t = jax.nn.sigmoid(gl) * y
        acc = t if acc is None else acc + t
    o_ref[...] = acc.astype(o_ref.dtype)


def _gate_merge(h, wg, bg, branches, wb, *, tm, tn):
    S, D = h.shape
    nj = D // tn
    wg_specs = [pl.BlockSpec((D, tn), (lambda j, i, n=n: (0, n * nj + j))) for n in range(N_BRANCH)]
    br_specs = [pl.BlockSpec((tm, BRANCH_WIDTH), lambda j, i: (i, 0)) for _ in range(N_BRANCH)]
    return pl.pallas_call(
        _gate_merge_body,
        grid=(nj, S // tm),
        in_specs=[pl.BlockSpec((tm, D), lambda j, i: (i, 0))] + wg_specs
        + [pl.BlockSpec((N_BRANCH, 1, tn), lambda j, i: (0, 0, j))] + br_specs
        + [pl.BlockSpec((N_BRANCH, BRANCH_WIDTH, tn), lambda j, i: (0, 0, j))],
        out_specs=pl.BlockSpec((tm, tn), lambda j, i: (i, j)),
        out_shape=jax.ShapeDtypeStruct((S, D), BF16),
        compiler_params=_params(("parallel", "parallel")),
        name="gate_merge",
    )(h, wg, wg, wg, wg, bg, *branches, wb)


def _dilated_body(q_ref, k_ref, v_ref, o_ref, lse_ref, *, ta, blocks_per_residue):
    i = pl.program_id(1)
    nblk = ta // QB
    qi = lax.broadcasted_iota(I32, (QB, QB), 0)
    ki = lax.broadcasted_iota(I32, (QB, QB), 1)
    mask_cur = qi >= ki
    mask_prev = ki >= qi
    for n in range(nblk):
        b = i * nblk + n
        has_prev = (b % blocks_per_residue) != 0
        pb = jnp.maximum(b - 1, 0)
        r_cur = pl.ds(pl.multiple_of(b * QB, QB), QB)
        r_prev = pl.ds(pl.multiple_of(pb * QB, QB), QB)
        qn = q_ref[n * QB:(n + 1) * QB, :]
        s_c = jnp.where(mask_cur, _dot_nt(qn, k_ref[r_cur, :]) * ATTN_SCALE, NEG)
        s_p = jnp.where(mask_prev, _dot_nt(qn, k_ref[r_prev, :]) * ATTN_SCALE, NEG)
        s_p = jnp.where(has_prev, s_p, NEG)
        m = jnp.maximum(jnp.max(s_c, axis=-1, keepdims=True), jnp.max(s_p, axis=-1, keepdims=True))
        p_c = jnp.exp(s_c - m)
        p_p = jnp.exp(s_p - m)
        den = jnp.sum(p_c, axis=-1, keepdims=True) + jnp.sum(p_p, axis=-1, keepdims=True)
        pv = _dot(p_c.astype(BF16), v_ref[r_cur, :]) + _dot(p_p.astype(BF16), v_ref[r_prev, :])
        o_ref[n * QB:(n + 1) * QB, :] = pv / den
        lse_ref[n * QB:(n + 1) * QB, :] = m + jnp.log(den)


def _dilated_group(q_arr, q_off, k_arr, k_off, v_arr, v_off, *, S, dilation, ta=1024):
    ta = min(ta, S)
    L = S // dilation
    assert L % QB == 0
    kern = functools.partial(_dilated_body, ta=ta, blocks_per_residue=L // QB)
    return pl.pallas_call(
        kern,
        grid=(A_HEADS_PER_GROUP, S // ta),
        in_specs=[pl.BlockSpec((ta, HEAD_DIM), lambda h, i: (i, q_off + h)),
                  pl.BlockSpec((S, HEAD_DIM), lambda h, i: (0, k_off + h)),
                  pl.BlockSpec((S, HEAD_DIM), lambda h, i: (0, v_off + h))],
        out_specs=[pl.BlockSpec((ta, HEAD_DIM), lambda h, i: (i, h)),
                   pl.BlockSpec((None, ta, 1), lambda h, i: (h, i, 0))],
        out_shape=[jax.ShapeDtypeStruct((S, A_HEADS_PER_GROUP * HEAD_DIM), F32),
                   jax.ShapeDtypeStruct((A_HEADS_PER_GROUP, S, 1), F32)],
        compiler_params=_params(("parallel", "parallel")),
        name="dilated_attn",
    )(q_arr, k_arr, v_arr)


def _combine_a_body(o0, o1, o2, l0, l1, l2, out_ref):
    for h in range(A_HEADS_PER_GROUP):
        a0, a1, a2 = l0[h], l1[h], l2[h]
        m = jnp.maximum(jnp.maximum(a0, a1), a2)
        e0, e1, e2 = jnp.exp(a0 - m), jnp.exp(a1 - m), jnp.exp(a2 - m)
        den = e0 + e1 + e2
        sl = slice(h * HEAD_DIM, (h + 1) * HEAD_DIM)
        out = (e0 / den) * o0[:, sl] + (e1 / den) * o1[:, sl] + (e2 / den) * o2[:, sl]
        out_ref[:, sl] = out.astype(out_ref.dtype)


def _combine_a(outs, lses, *, tm=512):
    S, W = outs[0].shape
    o_spec = pl.BlockSpec((tm, W), lambda i: (i, 0))
    l_spec = pl.BlockSpec((A_HEADS_PER_GROUP, tm, 1), lambda i: (0, i, 0))
    return pl.pallas_call(
        _combine_a_body,
        grid=(S // tm,),
        in_specs=[o_spec] * 3 + [l_spec] * 3,
        out_specs=o_spec,
        out_shape=jax.ShapeDtypeStruct((S, W), BF16),
        compiler_params=_params(("parallel",)),
        name="dilated_combine",
    )(*outs, *lses)


def _forget_cumsum_body(f_ref, b_ref, c_ref, *, rows_per_head):
    x = f_ref[...] + b_ref[...]
    x = jnp.minimum(x, 0.0) - jnp.log1p(jnp.exp(-jnp.abs(x)))
    lane = lax.broadcasted_iota(I32, x.shape, 1)
    s = 1
    while s < LANE:
        x = x + jnp.where(lane >= s, pltpu.roll(x, s, axis=1), 0.0)
        s *= 2
    tot = jnp.broadcast_to(x[:, LANE - 1:LANE], x.shape)
    row = lax.broadcasted_iota(I32, x.shape, 0) % rows_per_head
    t = tot
    s = 1
    while s < rows_per_head:
        t = t + jnp.where(row >= s, pltpu.roll(t, s, axis=0), 0.0)
        s *= 2
    c_ref[...] = x + (t - tot)


def _forget_cumsum(f_t, b_f):
    H, S = f_t.shape
    rph = S // LANE
    rows = H * rph
    b_col = jnp.repeat(b_f.astype(F32), rph).reshape(rows, 1)
    c = pl.pallas_call(
        functools.partial(_forget_cumsum_body, rows_per_head=rph),
        out_shape=jax.ShapeDtypeStruct((rows, LANE), F32),
        name="forget_cumsum",
    )(f_t.reshape(rows, LANE), b_col)
    return c.reshape(H, S)


def _fox_body(q_ref, k_ref, v_ref, c_ref, o_ref, *, t):
    i = pl.program_id(1)
    q = q_ref[...]
    c0 = jnp.max(c_ref[i], axis=-1, keepdims=True)

    def tile(j, carry, masked):
        m, l, acc = carry
        rows = pl.ds(pl.multiple_of(j * t, t), t)
        s = _dot_nt(q, k_ref[rows, :]) * ATTN_SCALE + (c0 - c_ref[j])
        if masked:
            qpos = lax.broadcasted_iota(I32, (t, t), 0)
            kpos = lax.broadcasted_iota(I32, (t, t), 1)
            s = jnp.where(qpos >= kpos, s, NEG)
        m_new = jnp.maximum(m, jnp.max(s, axis=-1, keepdims=True))
        alpha = jnp.exp(m - m_new)
        p = jnp.exp(s - m_new)
        l = alpha * l + jnp.sum(p, axis=-1, keepdims=True)
        acc = alpha * acc + _dot(p.astype(BF16), v_ref[rows, :])
        return m_new, l, acc

    init = (jnp.full((t, 1), NEG, F32), jnp.zeros((t, 1), F32), jnp.zeros((t, HEAD_DIM), F32))
    carry = lax.fori_loop(0, i, lambda j, c: tile(j, c, False), init)
    m, l, acc = tile(i, carry, True)
    o_ref[...] = (acc / l).astype(o_ref.dtype)


def _fox(qkv_arr, q_off, k_off, v_off, c, *, S, t=512):
    t = min(t, S)
    c4 = c.reshape(B_HEADS, S // t, 1, t)
    return pl.pallas_call(
        functools.partial(_fox_body, t=t),
        grid=(B_HEADS, S // t),
        in_specs=[pl.BlockSpec((t, HEAD_DIM), lambda h, i: (i, q_off + h)),
                  pl.BlockSpec((S, HEAD_DIM), lambda h, i: (0, k_off + h)),
                  pl.BlockSpec((S, HEAD_DIM), lambda h, i: (0, v_off + h)),
                  pl.BlockSpec((None, S // t, 1, t), lambda h, i: (h, 0, 0, 0))],
        out_specs=pl.BlockSpec((t, HEAD_DIM), lambda h, i: (i, h)),
        out_shape=jax.ShapeDtypeStruct((S, B_HEADS * HEAD_DIM), BF16),
        compiler_params=_params(("parallel", "parallel")),
        name="fox_attn",
    )(qkv_arr, qkv_arr, qkv_arr, c4)


def _float_key(s):
    bits = pltpu.bitcast(s + 0.0, I32)
    return bits ^ (lax.shift_right_arithmetic(bits, 31) & 0x7FFFFFFF)


def _dsa_body(qi_ref, kidx_ref, w_ref, q_ref, k_ref, vt_ref, o_ref, keys_ref, *, tq, tk, topk, ch):
    i = pl.program_id(0)
    q0 = i * tq
    n_all = (q0 + tq + tk - 1) // tk
    n_full = q0 // tk
    w = w_ref[...] * (IDX_DIM ** -0.5 * IDX_HEADS ** -0.5)

    def score_tile(j, masked):
        rows = pl.ds(pl.multiple_of(j * tk, tk), tk)
        kt = kidx_ref[rows, :]
        acc = jnp.zeros((tk, tq), F32)
        for h in range(IDX_HEADS):
            rel = _dot_nt(kt, qi_ref[h])
            acc = acc + jnp.maximum(rel, 0.0) * w[h:h + 1, :]
        key = _float_key(acc)
        if masked:
            kpos = j * tk + lax.broadcasted_iota(I32, (tk, tq), 0)
            qpos = q0 + lax.broadcasted_iota(I32, (tk, tq), 1)
            key = jnp.where(kpos <= qpos, key, NEG_KEY)
        keys_ref[rows, :] = key

    def _s_full(j, c):
        score_tile(j, False)
        return c

    def _s_mask(j, c):
        score_tile(j, True)
        return c

    lax.fori_loop(0, n_full, _s_full, 0)
    lax.fori_loop(n_full, n_all, _s_mask, 0)

    n_chunks = (n_all * tk) // ch

    def count_ge(cand):
        def body(c, acc):
            t = keys_ref[pl.ds(pl.multiple_of(c * ch, ch), ch), :]
            return acc + jnp.sum(jnp.where(t >= cand, 1.0, 0.0), axis=0, keepdims=True)
        return lax.fori_loop(0, n_chunks, body, jnp.zeros((1, tq), F32))

    kf = float(topk)
    lo = jnp.where(count_ge(jnp.zeros((1, tq), I32)) >= kf, 0, INT_MIN).astype(I32)

    def bit_step(b, lo):
        cand = lo + lax.shift_left(jnp.int32(1), 30 - b)
        return jnp.where(count_ge(cand) >= kf, cand, lo)

    thr = lax.fori_loop(0, 31, bit_step, lo)

    n_ge = count_ge(thr)
    tied = jnp.where((n_ge > kf) & (thr != NEG_KEY), 1.0, 0.0)
    has_ties = jnp.max(tied) > 0.0

    @pl.when(has_ties)
    def _():
        need = kf - count_ge(thr + 1)
        tri = (lax.broadcasted_iota(I32, (tk, tk), 0) >= lax.broadcasted_iota(I32, (tk, tk), 1)).astype(BF16)

        def demote(j, seen):
            rows = pl.ds(pl.multiple_of(j * tk, tk), tk)
            key = keys_ref[rows, :]
            eq = jnp.where(key == thr, 1.0, 0.0)
            rank = _dot(tri, eq.astype(BF16)) + seen
            keys_ref[rows, :] = jnp.where((eq > 0.5) & (rank > need), INT_MIN, key)
            return seen + jnp.sum(eq, axis=0, keepdims=True)

        lax.fori_loop(0, n_all, demote, jnp.zeros((1, tq), F32))

    for h in range(C_HEADS):
        cols = slice(h * HEAD_DIM, (h + 1) * HEAD_DIM)
        qh = q_ref[:, cols]

        def tile(j, carry, masked, qh=qh, cols=cols, h=h):
            m, l, acc = carry
            rows = pl.ds(pl.multiple_of(j * tk, tk), tk)
            s = _dot_nt(k_ref[rows, cols], qh) * ATTN_SCALE
            sel = keys_ref[rows, :] >= thr
            if masked:
                kpos = j * tk + lax.broadcasted_iota(I32, (tk, tq), 0)
                qpos = q0 + lax.broadcasted_iota(I32, (tk, tq), 1)
                sel = sel & (kpos <= qpos)
            s = jnp.where(sel, s, NEG)
            m_new = jnp.maximum(m, jnp.max(s, axis=0, keepdims=True))
            alpha = jnp.exp(m - m_new)
            p = jnp.where(sel, jnp.exp(s - m_new), 0.0)
            l = alpha * l + jnp.sum(p, axis=0, keepdims=True)
            acc = alpha * acc + _dot(vt_ref[j, cols, :], p.astype(BF16))
            return m_new, l, acc

        init = (jnp.full((1, tq), NEG, F32), jnp.zeros((1, tq), F32), jnp.zeros((HEAD_DIM, tq), F32))
        carry = lax.fori_loop(0, n_full, lambda j, c: tile(j, c, False), init)
        m, l, acc = lax.fori_loop(n_full, n_all, lambda j, c: tile(j, c, True), carry)
        o_ref[cols, :] = (acc / jnp.maximum(l, TINY)).astype(o_ref.dtype)


def _dsa(qi, kidx, w_t, q_arr, q_blk, k_arr, k_blk, v_t, *, S, tq=256, tk=256):
    tq = min(tq, S)
    tk = min(tk, S)
    topk = min(DSA_TOPK, S // 4)
    W = C_HEADS * HEAD_DIM
    kern = functools.partial(_dsa_body, tq=tq, tk=tk, topk=topk, ch=min(tk, 256))
    return pl.pallas_call(
        kern,
        grid=(S // tq,),
        in_specs=[pl.BlockSpec((IDX_HEADS, tq, IDX_DIM), lambda i: (0, i, 0)),
                  pl.BlockSpec((S, IDX_DIM), lambda i: (0, 0)),
                  pl.BlockSpec((IDX_HEADS, tq), lambda i: (0, i)),
                  pl.BlockSpec((tq, W), lambda i: (i, q_blk)),
                  pl.BlockSpec((S, W), lambda i: (0, k_blk)),
                  pl.BlockSpec((S // tk, W, tk), lambda i: (0, 0, 0))],
        out_specs=pl.BlockSpec((W, tq), lambda i: (0, i)),
        out_shape=jax.ShapeDtypeStruct((W, S), BF16),
        scratch_shapes=[pltpu.VMEM((S, tq), I32)],
        compiler_params=_params(("arbitrary",)),
        name="dsa_attn",
    )(qi, kidx, w_t, q_arr, k_arr, v_t)


def _compress_body(x_ref, pea_ref, peb_ref, w1a_ref, w1b_ref, w2_ref, o_ref):
    x = x_ref[...].astype(F32)
    a = _dot((x + pea_ref[...]).astype(BF16), w1a_ref[...].astype(BF16))
    b = _dot((x + peb_ref[...]).astype(BF16), w1b_ref[...].astype(BF16))
    n = a.shape[0]
    hid = a + pltpu.roll(b, n - 1, axis=0)
    o_ref[...] = _dot(jax.nn.gelu(hid).astype(BF16), w2_ref[...].astype(BF16)).astype(o_ref.dtype)


def _compress(x2, pe, w1, w2):
    _, S, Dh = x2.shape
    n = S // CMP_STRIDE
    half = CMP_STRIDE * Dh
    xr = x2.reshape(2, n, half)
    pe_a = pe[:, :CMP_STRIDE].reshape(2, 1, half)
    pe_b = pe[:, CMP_STRIDE:].reshape(2, 1, half)
    return pl.pallas_call(
        _compress_body,
        grid=(2,),
        in_specs=[pl.BlockSpec((None, n, half), lambda g: (g, 0, 0)),
                  pl.BlockSpec((None, 1, half), lambda g: (g, 0, 0)),
                  pl.BlockSpec((None, 1, half), lambda g: (g, 0, 0)),
                  pl.BlockSpec((None, half, CMP_HIDDEN), lambda g: (g, 0, 0)),
                  pl.BlockSpec((None, half, CMP_HIDDEN), lambda g: (g, 1, 0)),
                  pl.BlockSpec((None, CMP_HIDDEN, Dh), lambda g: (g, 0, 0))],
        out_specs=pl.BlockSpec((None, n, Dh), lambda g: (g, 0, 0)),
        out_shape=jax.ShapeDtypeStruct((2, n, Dh), BF16),
        compiler_params=_params(("parallel",)),
        name="nsa_compress",
    )(xr, pe_a, pe_b, w1, w1, w2)


def _nsa_body(q_ref, kc_ref, vc_ref, ks_ref, vs_ref, kw_ref, vw_ref, g_ref, ovt_ref, o_ref,
              *, tq, tk, n_slc, n_sel, ncp, gate_col):
    i = pl.program_id(0)
    q0 = i * tq
    H = D_HEADS
    R = H * tq
    qs = jnp.concatenate([q_ref[:, h * HEAD_DIM:(h + 1) * HEAD_DIM] for h in range(H)], axis=0)

    def qpos_of(shape):
        return q0 + (lax.broadcasted_iota(I32, shape, 0) & (tq - 1))

    lc = _dot_nt(qs, kc_ref[...]) * ATTN_SCALE
    cend = lax.broadcasted_iota(I32, (R, ncp), 1) * CMP_STRIDE + (CMP_LEN - 1)
    mc = cend <= qpos_of((R, ncp))
    lc = jnp.where(mc, lc, NEG)
    mx = jnp.max(lc, axis=-1, keepdims=True)
    pc = jnp.where(mc, jnp.exp(lc - mx), 0.0)
    pc = pc / jnp.maximum(jnp.sum(pc, axis=-1, keepdims=True), TINY)
    o_cmp = _dot(pc.astype(BF16), vc_ref[...])

    psum = pc[0:tq]
    for h in range(1, H):
        psum = psum + pc[h * tq:(h + 1) * tq]
    p_hi = psum.astype(BF16)
    p_lo = (psum - p_hi.astype(F32)).astype(BF16)
    ovt = ovt_ref[...]
    imp = _dot_nt(ovt, p_hi) + _dot_nt(ovt, p_lo)
    jblk = lax.broadcasted_iota(I32, (n_slc, tq), 0)
    jblk_f = jblk.astype(F32)
    qpos_l = q0 + lax.broadcasted_iota(I32, (n_slc, tq), 1)
    cur = lax.shift_right_logical(qpos_l, int(np.log2(SLC_BLOCK)))
    forced = (jblk == 0) | (jblk == cur) | (jblk == cur - 1)
    val = jnp.where(forced, FORCE, jnp.where(jblk * SLC_BLOCK <= qpos_l, imp, NEG))
    sel_t = jnp.zeros((n_slc, tq), F32)
    for _ in range(n_sel):
        mxv = jnp.max(val, axis=0, keepdims=True)
        first = jnp.min(jnp.where(val == mxv, jblk_f, float(n_slc)), axis=0, keepdims=True)
        hit = jblk_f == first
        sel_t = jnp.where(hit, 1.0, sel_t)
        val = jnp.where(hit, -jnp.inf, val)
    sel = jnp.transpose(sel_t).astype(BF16)

    bpt = tk // SLC_BLOCK
    n_all = (q0 + tq + tk - 1) // tk

    def slc_tile(j, carry):
        m, l, acc = carry
        rows = pl.ds(pl.multiple_of(j * tk, tk), tk)
        s = _dot_nt(qs, ks_ref[rows, :]) * ATTN_SCALE
        eb = lax.broadcasted_iota(I32, (n_slc, tk), 0) - j * bpt
        ec = lax.shift_right_logical(lax.broadcasted_iota(I32, (n_slc, tk), 1), int(np.log2(SLC_BLOCK)))
        expand = jnp.where(eb == ec, 1.0, 0.0).astype(BF16)
        mt = _dot(sel, expand)
        kpos = j * tk + lax.broadcasted_iota(I32, (tq, tk), 1)
        qpos = q0 + lax.broadcasted_iota(I32, (tq, tk), 0)
        mt = jnp.where(kpos <= qpos, mt, 0.0)
        m4 = jnp.concatenate([mt] * H, axis=0)
        s = jnp.where(m4 > 0.5, s, NEG)
        m_new = jnp.maximum(m, jnp.max(s, axis=-1, keepdims=True))
        alpha = jnp.exp(m - m_new)
        p = jnp.exp(s - m_new)
        l = alpha * l + jnp.sum(p, axis=-1, keepdims=True)
        acc = alpha * acc + _dot(p.astype(BF16), vs_ref[rows, :])
        return m_new, l, acc

    init = (jnp.full((R, 1), NEG, F32), jnp.zeros((R, 1), F32), jnp.zeros((R, HEAD_DIM), F32))
    _, l_s, acc_s = lax.fori_loop(0, n_all, slc_tile, init)
    o_slc = acc_s / l_s

    wlen = WIN + tq
    start = pl.multiple_of(jnp.maximum(q0 - WIN, 0), tq)
    wrows = pl.ds(start, wlen)
    lw = _dot_nt(qs, kw_ref[wrows, :]) * ATTN_SCALE
    dist = qpos_of((R, wlen)) - (start + lax.broadcasted_iota(I32, (R, wlen), 1))
    mw = (dist >= 0) & (dist < WIN)
    lw = jnp.where(mw, lw, NEG)
    mxw = jnp.max(lw, axis=-1, keepdims=True)
    pw = jnp.exp(lw - mxw)
    o_win = _dot(pw.astype(BF16), vw_ref[wrows, :]) / jnp.sum(pw, axis=-1, keepdims=True)

    gates = jax.nn.sigmoid(g_ref[...])
    for h in range(H):
        r = slice(h * tq, (h + 1) * tq)
        c = gate_col + 3 * h
        out = (gates[:, c:c + 1] * o_cmp[r] + gates[:, c + 1:c + 2] * o_slc[r]
               + gates[:, c + 2:c + 3] * o_win[r])
        o_ref[:, h * HEAD_DIM:(h + 1) * HEAD_DIM] = out.astype(o_ref.dtype)


def _nsa(q_arr, q_blk, kv_cmp, k_arr, ks_blk, kw_blk, v_arr, vs_blk, vw_blk, misc, gate_col,
         *, S, tq=256, tk=512):
    tq = min(tq, S)
    tk = min(tk, S)
    assert S >= WIN + tq and tq & (tq - 1) == 0
    n_slc = S // SLC_BLOCK
    n_sel = min(SLC_COUNT, n_slc)
    ncp = S // CMP_STRIDE
    W = D_HEADS * HEAD_DIM
    ci = np.arange(ncp)[None, :]
    sj = np.arange(n_slc)[:, None]
    ovt = ((ci * CMP_STRIDE < (sj + 1) * SLC_BLOCK) & (ci * CMP_STRIDE + CMP_LEN > sj * SLC_BLOCK)
           & (ci < ncp - 1))
    ovt = jnp.asarray(ovt, BF16)
    kern = functools.partial(_nsa_body, tq=tq, tk=tk, n_slc=n_slc, n_sel=n_sel, ncp=ncp, gate_col=gate_col)
    col = lambda b: pl.BlockSpec((S, HEAD_DIM), lambda i: (0, b))
    return pl.pallas_call(
        kern,
        grid=(S // tq,),
        in_specs=[pl.BlockSpec((tq, W), lambda i: (i, q_blk)),
                  pl.BlockSpec((None, ncp, HEAD_DIM), lambda i: (0, 0, 0)),
                  pl.BlockSpec((None, ncp, HEAD_DIM), lambda i: (1, 0, 0)),
                  col(ks_blk), pl.BlockSpec((S, HEAD_DIM), lambda i: (0, vs_blk)),
                  col(kw_blk), pl.BlockSpec((S, HEAD_DIM), lambda i: (0, vw_blk)),
                  pl.BlockSpec((tq, LANE), lambda i: (i, 0)),
                  pl.BlockSpec((n_slc, ncp), lambda i: (0, 0))],
        out_specs=pl.BlockSpec((tq, W), lambda i: (i, 0)),
        out_shape=jax.ShapeDtypeStruct((S, W), BF16),
        compiler_params=_params(("parallel",)),
        name="nsa_attn",
    )(q_arr, kv_cmp, kv_cmp, k_arr, v_arr, k_arr, v_arr, misc, ovt)


def _rope_tables(S):
    pos = jnp.arange(S, dtype=F32)[:, None]
    half = HEAD_DIM // 2
    ang = pos * (ROPE_THETA ** (-jnp.arange(half, dtype=F32) / half))[None, :]
    cos, sin = jnp.cos(ang), jnp.sin(ang)
    t128 = (jnp.concatenate([cos, cos], 1), jnp.concatenate([-sin, sin], 1))
    h64 = IDX_DIM // 2
    ang = pos * (ROPE_THETA ** (-jnp.arange(h64, dtype=F32) / h64))[None, :]
    cos, sin = jnp.cos(ang), jnp.sin(ang)
    z = jnp.zeros_like(sin)
    t64 = (jnp.concatenate([cos] * 4, 1), jnp.concatenate([-sin, z, -sin, z], 1),
           jnp.concatenate([z, sin, z, sin], 1))
    return t128, t64


def _split_w_in(w_in):
    offs = np.cumsum([0,
                      A_HEADS * HEAD_DIM, A_HEADS * HEAD_DIM, A_HEADS * HEAD_DIM,
                      B_HEADS * HEAD_DIM, B_HEADS * HEAD_DIM, B_HEADS * HEAD_DIM, B_HEADS,
                      C_HEADS * HEAD_DIM, C_HEADS * HEAD_DIM, C_HEADS * HEAD_DIM,
                      IDX_HEADS * IDX_DIM, IDX_DIM, IDX_HEADS,
                      D_HEADS * HEAD_DIM] + [HEAD_DIM] * 6 + [3 * D_HEADS])
    names = ["aq", "ak", "av", "bq", "bk", "bv", "bf", "cq", "ck", "cv", "cqi", "cki", "cwi",
             "dq", "dkc", "dvc", "dks", "dvs", "dkw", "dvw", "dg"]
    col = {n: w_in[:, offs[k]:offs[k + 1]] for k, n in enumerate(names)}
    D = w_in.shape[0]

    def pack(ns, pad=0):
        parts = [col[n] for n in ns]
        if pad:
            parts.append(jnp.zeros((D, pad), w_in.dtype))
        return jnp.concatenate(parts, axis=1).astype(BF16)

    n_misc = B_HEADS + IDX_HEADS + 3 * D_HEADS
    return dict(
        a_qk=pack(["aq", "ak"]),
        ab_v=pack(["av", "bq", "bk", "bv"]),
        cd_qk=pack(["cq", "ck", "dq", "dkc", "dks", "dkw"]),
        idx=pack(["cqi", "cki"], pad=LANE - IDX_DIM),
        cd_v=pack(["cv", "dvc", "dvs", "dvw"]),
        misc=pack(["bf", "cwi", "dg"], pad=LANE - n_misc),
    )


def _deinterleave(a, d):
    S, W = a.shape
    return a.reshape(S // d, d, W).transpose(1, 0, 2).reshape(S, W)


def _interleave(a, d):
    S, W = a.shape
    return a.reshape(d, S // d, W).transpose(1, 0, 2).reshape(S, W)


def _pick(n, cands):
    for c in cands:
        if n % c == 0:
            return c
    return n


def _layer(x, p, tables):
    S, D = x.shape
    (cos128, sin128), (cos64, sin64a, sin64b) = tables
    tm = _pick(S, (1024, 512, 256))
    h = _rmsnorm(x, p["norm_mix"], BF16)
    w = _split_w_in(p["w_in"])

    def proj(name, mode="plain", extras=(), out_dtype=BF16):
        n = w[name].shape[1]
        tn = _pick(n, (1024, 896, 768, 640, 512, 384, 256, 128))
        return _mm(h, w[name], tm=tm, tn=tn, out_dtype=out_dtype, mode=mode, extras=extras, name="in_" + name)

    a_qk = proj("a_qk", "rope128", (cos128, sin128))
    ab_v = proj("ab_v")
    cd_qk = proj("cd_qk", "rope128", (cos128, sin128))
    idx = proj("idx", "rope64", (cos64, sin64a, sin64b))
    cd_v = proj("cd_v")
    misc = proj("misc", out_dtype=F32)

    G = A_HEADS_PER_GROUP
    outs, lses = [], []
    for g, (window, dilation) in enumerate(DIL_PATTERNS):
        assert window // dilation == QB
        if dilation == 1:
            o, l = _dilated_group(a_qk, g * G, a_qk, A_HEADS + g * G, ab_v, g * G, S=S, dilation=1)
        else:
            W = G * HEAD_DIM
            qd = _deinterleave(a_qk[:, g * W:(g + 1) * W], dilation)
            kd = _deinterleave(a_qk[:, A_HEADS * HEAD_DIM + g * W:A_HEADS * HEAD_DIM + (g + 1) * W], dilation)
            vd = _deinterleave(ab_v[:, g * W:(g + 1) * W], dilation)
            o, l = _dilated_group(qd, 0, kd, 0, vd, 0, S=S, dilation=dilation)
            o = _interleave(o, dilation)
            l = l.reshape(G, dilation, S // dilation).transpose(0, 2, 1).reshape(G, S, 1)
        outs.append(o)
        lses.append(l)
    o_a = _combine_a(outs, lses)

    c = _forget_cumsum(misc[:, :B_HEADS].T, p["b_f"])
    o_b = _fox(ab_v, A_HEADS, A_HEADS + B_HEADS, A_HEADS + 2 * B_HEADS, c, S=S)

    tkc = min(256, S)
    qi = idx[:, :IDX_HEADS * IDX_DIM].reshape(S, IDX_HEADS, IDX_DIM).transpose(1, 0, 2)
    kidx = idx[:, IDX_HEADS * IDX_DIM:IDX_HEADS * IDX_DIM + IDX_DIM]
    w_t = misc[:, B_HEADS:B_HEADS + IDX_HEADS].T
    cw = C_HEADS * HEAD_DIM
    v_t = cd_v[:, :cw].reshape(S // tkc, tkc, cw).transpose(0, 2, 1)
    o_c = _dsa(qi, kidx, w_t, cd_qk, 0, cd_qk, 1, v_t, S=S, tk=tkc).T

    kv_c = jnp.stack([cd_qk[:, 3 * cw:3 * cw + HEAD_DIM], cd_v[:, cw:cw + HEAD_DIM]])
    kv_cmp = _compress(kv_c, p["cmp_pe"], p["cmp_w1"], p["cmp_w2"])
    nb = cw // HEAD_DIM
    o_d = _nsa(cd_qk, 2, kv_cmp, cd_qk, 3 * nb + 1, 3 * nb + 2, cd_v, nb + 1, nb + 2,
               misc, B_HEADS + IDX_HEADS, S=S)

    tn = _pick(D, (512, 256, 128))
    wg = p["w_gate"].reshape(D, N_BRANCH * D).astype(BF16)
    bg = p["b_gate"].reshape(N_BRANCH, 1, D)
    merged = _gate_merge(h, wg, bg, (o_a, o_b, o_c, o_d), p["w_branch"].astype(BF16), tm=tm, tn=tn)
    x = _mm(merged, p["w_out"].astype(BF16), tm=tm, tn=tn, out_dtype=F32, mode="resid", extras=(x,), name="out_proj")

    h2 = _rmsnorm(x, p["norm_ffn"], BF16)
    Hf = p["w_ffn_in"].shape[1] // 2
    act = _mm_swiglu(h2, p["w_ffn_in"].astype(BF16), tm=tm, tn=_pick(Hf, (512, 256, 128)))
    x = _mm(act, p["w_ffn_out"].astype(BF16), tm=_pick(S, (512, 256)), tn=tn, out_dtype=F32, mode="resid",
            extras=(x,), name="ffn_out")
    return x


def kernel(x, norm_mix, w_in, w_gate, b_gate, b_f, cmp_pe, cmp_w1, cmp_w2, w_branch, w_out, norm_ffn,
           w_ffn_in, w_ffn_out, norm_final):
    B, S, D = x.shape
    tables = _rope_tables(S)
    outs = []
    for b in range(B):
        xb = x[b]
        for l in range(norm_mix.shape[0]):
            p = dict(norm_mix=norm_mix[l], w_in=w_in[l], w_gate=w_gate[l], b_gate=b_gate[l], b_f=b_f[l],
                     cmp_pe=cmp_pe[l], cmp_w1=cmp_w1[l], cmp_w2=cmp_w2[l], w_branch=w_branch[l],
                     w_out=w_out[l], norm_ffn=norm_ffn[l], w_ffn_in=w_ffn_in[l], w_ffn_out=w_ffn_out[l])
            xb = _layer(xb, p, tables)
        outs.append(_rmsnorm(xb, norm_final, x.dtype))
    return jnp.stack(outs)
```

```python
import functools
import math

import jax
import jax.numpy as jnp
import numpy as np
from jax import lax
from jax.experimental import pallas as pl
from jax.experimental.pallas import tpu as pltpu

F32 = jnp.float32
BF16 = jnp.bfloat16
I32 = jnp.int32

HEAD_DIM = 128
ROPE_THETA = 10000.0
ATTN_SCALE = HEAD_DIM ** -0.5
QB = 128
NEG = -1e30
TINY = 1e-30
DIL_PATTERNS = ((128, 1), (512, 4), (2048, 16))
A_HEADS_PER_GROUP = 4
A_HEADS = A_HEADS_PER_GROUP * len(DIL_PATTERNS)
B_HEADS = 4
C_HEADS = 4
IDX_HEADS = 16
IDX_DIM = 64
DSA_TOPK = 256
D_HEADS = 4
CMP_LEN = 32
CMP_STRIDE = 16
CMP_HIDDEN = 256
SLC_BLOCK = 64
SLC_COUNT = 16
WIN = 512
FORCE = 1e9
N_BRANCH = 4
BRANCH_WIDTH = 4 * HEAD_DIM

LANE = 128
SUBLANE = 8
VMEM_LIMIT = 56 * 1024 * 1024
VT = 256
TQ = 512
TQ_NSA = 256
TK = 512

EXP_C = ATTN_SCALE * math.log2(math.e)
M_INIT = 0.5 * NEG

INT_MIN = -2 ** 31
_NEG_BITS = int(np.array(NEG, np.float32).view(np.int32))
NEG_KEY = _NEG_BITS ^ 0x7FFFFFFF


def _params(sem):
    return pltpu.CompilerParams(dimension_semantics=sem, vmem_limit_bytes=VMEM_LIMIT)


def _resident(shape, index_map):
    return pl.BlockSpec(shape, index_map, pipeline_mode=pl.Buffered(1))


def _dot(a, b):
    return jnp.dot(a, b, preferred_element_type=F32)


def _dot_nt(a, b):
    return lax.dot_general(a, b, (((1,), (1,)), ((), ())), preferred_element_type=F32)


def _dot_tn(a, b):
    return lax.dot_general(a, b, (((0,), (0,)), ((), ())), preferred_element_type=F32)


def _rmsnorm_body(x_ref, g_ref, o_ref):
    x = x_ref[...]
    ms = jnp.mean(x * x, axis=-1, keepdims=True)
    o_ref[...] = (x * lax.rsqrt(ms + 1e-6) * g_ref[...]).astype(o_ref.dtype)


def _rmsnorm(x, g, out_dtype, tm=512):
    S, D = x.shape
    return pl.pallas_call(
        _rmsnorm_body,
        grid=(S // tm,),
        in_specs=[pl.BlockSpec((tm, D), lambda i: (i, 0)),
                  pl.BlockSpec((1, D), lambda i: (0, 0))],
        out_specs=pl.BlockSpec((tm, D), lambda i: (i, 0)),
        out_shape=jax.ShapeDtypeStruct((S, D), out_dtype),
        compiler_params=_params(("parallel",)),
        name="rmsnorm",
    )(x, g.reshape(1, D))


def _rope128(acc, cos, sin):
    parts = []
    for c in range(acc.shape[1] // LANE):
        blk = acc[:, c * LANE:(c + 1) * LANE]
        parts.append(blk * cos + pltpu.roll(blk, LANE // 2, axis=1) * sin)
    return parts[0] if len(parts) == 1 else jnp.concatenate(parts, axis=1)


def _rope64(acc, cos, sin_a, sin_b):
    parts = []
    for c in range(acc.shape[1] // LANE):
        blk = acc[:, c * LANE:(c + 1) * LANE]
        parts.append(blk * cos + pltpu.roll(blk, LANE - IDX_DIM // 2, axis=1) * sin_a
                     + pltpu.roll(blk, IDX_DIM // 2, axis=1) * sin_b)
    return parts[0] if len(parts) == 1 else jnp.concatenate(parts, axis=1)


def _mm_body(*refs, mode, layout):
    a_ref, b_ref = refs[0], refs[1]
    o_ref = refs[-1]
    acc = _dot(a_ref[...], b_ref[...])
    if mode == "rope128":
        acc = _rope128(acc, refs[2][...], refs[3][...])
    elif mode == "rope64":
        acc = _rope64(acc, refs[2][...], refs[3][...], refs[4][...])
    elif mode == "resid":
        acc = refs[2][...] + acc
    if layout == "rows":
        o_ref[...] = acc.astype(o_ref.dtype)
    elif layout == "cols":
        o_ref[...] = acc.T.astype(o_ref.dtype)
    else:
        acc_t = acc.T
        for c in range(acc_t.shape[1] // VT):
            o_ref[c] = acc_t[:, c * VT:(c + 1) * VT].astype(o_ref.dtype)


def _mm(a, b, *, tm, tn, out_dtype, mode="plain", extras=(), layout="rows", name="mm"):
    M, K = a.shape
    N = b.shape[1]
    assert M % tm == 0 and N % tn == 0, (M, N, tm, tn)
    in_specs = [pl.BlockSpec((tm, K), lambda j, i: (i, 0)),
                pl.BlockSpec((K, tn), lambda j, i: (0, j))]
    if mode in ("rope128", "rope64"):
        in_specs += [pl.BlockSpec((tm, LANE), lambda j, i: (i, 0)) for _ in extras]
    elif mode == "resid":
        in_specs += [pl.BlockSpec((tm, tn), lambda j, i: (i, j))]
    if layout == "rows":
        out_spec = pl.BlockSpec((tm, tn), lambda j, i: (i, j))
        out_shape = (M, N)
    elif layout == "cols":
        out_spec = pl.BlockSpec((tn, tm), lambda j, i: (j, i))
        out_shape = (N, M)
    else:
        assert tm % VT == 0
        out_spec = pl.BlockSpec((tm // VT, tn, VT), lambda j, i: (i, j, 0))
        out_shape = (M // VT, N, VT)
    return pl.pallas_call(
        functools.partial(_mm_body, mode=mode, layout=layout),
        grid=(N // tn, M // tm),
        in_specs=in_specs,
        out_specs=out_spec,
        out_shape=jax.ShapeDtypeStruct(out_shape, out_dtype),
        compiler_params=_params(("parallel", "parallel")),
        name=name,
    )(a, b, *extras)


def _swiglu_body(a_ref, bg_ref, bu_ref, o_ref):
    a = a_ref[...]
    g = _dot(a, bg_ref[...])
    u = _dot(a, bu_ref[...])
    o_ref[...] = (g * jax.nn.sigmoid(g) * u).astype(o_ref.dtype)


def _mm_swiglu(a, b, *, tm, tn):
    M, K = a.shape
    H = b.shape[1] // 2
    nj = H // tn
    assert H % tn == 0 and M % tm == 0
    return pl.pallas_call(
        _swiglu_body,
        grid=(nj, M // tm),
        in_specs=[pl.BlockSpec((tm, K), lambda j, i: (i, 0)),
                  pl.BlockSpec((K, tn), lambda j, i: (0, j)),
                  pl.BlockSpec((K, tn), lambda j, i: (0, j + nj))],
        out_specs=pl.BlockSpec((tm, tn), lambda j, i: (i, j)),
        out_shape=jax.ShapeDtypeStruct((M, H), BF16),
        compiler_params=_params(("parallel", "parallel")),
        name="ffn_in_swiglu",
    )(a, b, b)


def _gate_merge_body(h_ref, wg0, wg1, wg2, wg3, bg_ref, br0, br1, br2, br3, wb_ref, o_ref):
    h = h_ref[...]
    acc = None
    for n, (wg, br) in enumerate(((wg0, br0), (wg1, br1), (wg2, br2), (wg3, br3))):
        gl = _dot(h, wg[...]) + bg_ref[n]
        y = _dot(br[...], wb_ref[n]) if n == 0 else _dot_tn(br[...], wb_ref[n])
        t = jax.nn.sigmoid(gl) * y
        acc = t if acc is None else acc + t
    o_ref[...] = acc.astype(o_ref.dtype)


def _gate_merge(h, wg, bg, branches, wb, *, tm, tn):
    S, D = h.shape
    nj = D // tn
    wg_specs = [pl.BlockSpec((D, tn), (lambda j, i, n=n: (0, n * nj + j))) for n in range(N_BRANCH)]
    br_specs = [pl.BlockSpec((tm, BRANCH_WIDTH), lambda j, i: (i, 0))]
    br_specs += [pl.BlockSpec((BRANCH_WIDTH, tm), lambda j, i: (0, i)) for _ in range(N_BRANCH - 1)]
    return pl.pallas_call(
        _gate_merge_body,
        grid=(nj, S // tm),
        in_specs=[pl.BlockSpec((tm, D), lambda j, i: (i, 0))] + wg_specs
        + [pl.BlockSpec((N_BRANCH, 1, tn), lambda j, i: (0, 0, j))] + br_specs
        + [pl.BlockSpec((N_BRANCH, BRANCH_WIDTH, tn), lambda j, i: (0, 0, j))],
        out_specs=pl.BlockSpec((tm, tn), lambda j, i: (i, j)),
        out_shape=jax.ShapeDtypeStruct((S, D), BF16),
        compiler_params=_params(("parallel", "parallel")),
        name="gate_merge",
    )(h, wg, wg, wg, wg, bg, *branches, wb)


def _dilated_body(q_ref, k_ref, v_ref, o_ref, lse_ref, *, ta, blocks_per_residue):
    i = pl.program_id(1)
    nblk = ta // QB
    qi = lax.broadcasted_iota(I32, (QB, QB), 0)
    ki = lax.broadcasted_iota(I32, (QB, QB), 1)
    mask_cur = qi >= ki
    mask_prev = ki >= qi
    for n in range(nblk):
        b = i * nblk + n
        has_prev = (b % blocks_per_residue) != 0
        pb = jnp.maximum(b - 1, 0)
        r_cur = pl.ds(pl.multiple_of(b * QB, QB), QB)
        r_prev = pl.ds(pl.multiple_of(pb * QB, QB), QB)
        qn = q_ref[n * QB:(n + 1) * QB, :]
        s_c = jnp.where(mask_cur, _dot_nt(qn, k_ref[r_cur, :]) * ATTN_SCALE, NEG)
        s_p = jnp.where(mask_prev, _dot_nt(qn, k_ref[r_prev, :]) * ATTN_SCALE, NEG)
        s_p = jnp.where(has_prev, s_p, NEG)
        m = jnp.maximum(jnp.max(s_c, axis=-1, keepdims=True), jnp.max(s_p, axis=-1, keepdims=True))
        p_c = jnp.exp(s_c - m)
        p_p = jnp.exp(s_p - m)
        den = jnp.sum(p_c, axis=-1, keepdims=True) + jnp.sum(p_p, axis=-1, keepdims=True)
        pv = _dot(p_c.astype(BF16), v_ref[r_cur, :]) + _dot(p_p.astype(BF16), v_ref[r_prev, :])
        o_ref[n * QB:(n + 1) * QB, :] = pv / den
        lse_ref[n * QB:(n + 1) * QB, :] = m + jnp.log(den)


def _dilated_group(q_arr, q_off, k_arr, k_off, v_arr, v_off, *, S, dilation, ta=1024):
    ta = min(ta, S)
    L = S // dilation
    assert L % QB == 0
    kern = functools.partial(_dilated_body, ta=ta, blocks_per_residue=L // QB)
    return pl.pallas_call(
        kern,
        grid=(A_HEADS_PER_GROUP, S // ta),
        in_specs=[pl.BlockSpec((ta, HEAD_DIM), lambda h, i: (i, q_off + h)),
                  pl.BlockSpec((S, HEAD_DIM), lambda h, i: (0, k_off + h)),
                  pl.BlockSpec((S, HEAD_DIM), lambda h, i: (0, v_off + h))],
        out_specs=[pl.BlockSpec((ta, HEAD_DIM), lambda h, i: (i, h)),
                   pl.BlockSpec((None, ta, 1), lambda h, i: (h, i, 0))],
        out_shape=[jax.ShapeDtypeStruct((S, A_HEADS_PER_GROUP * HEAD_DIM), F32),
                   jax.ShapeDtypeStruct((A_HEADS_PER_GROUP, S, 1), F32)],
        compiler_params=_params(("parallel", "parallel")),
        name="dilated_attn",
    )(q_arr, k_arr, v_arr)


def _combine_a_body(o0, o1, o2, l0, l1, l2, out_ref):
    for h in range(A_HEADS_PER_GROUP):
        a0, a1, a2 = l0[h], l1[h], l2[h]
        m = jnp.maximum(jnp.maximum(a0, a1), a2)
        e0, e1, e2 = jnp.exp(a0 - m), jnp.exp(a1 - m), jnp.exp(a2 - m)
        den = e0 + e1 + e2
        sl = slice(h * HEAD_DIM, (h + 1) * HEAD_DIM)
        out = (e0 / den) * o0[:, sl] + (e1 / den) * o1[:, sl] + (e2 / den) * o2[:, sl]
        out_ref[:, sl] = out.astype(out_ref.dtype)


def _combine_a(outs, lses, *, tm=512):
    S, W = outs[0].shape
    o_spec = pl.BlockSpec((tm, W), lambda i: (i, 0))
    l_spec = pl.BlockSpec((A_HEADS_PER_GROUP, tm, 1), lambda i: (0, i, 0))
    return pl.pallas_call(
        _combine_a_body,
        grid=(S // tm,),
        in_specs=[o_spec] * 3 + [l_spec] * 3,
        out_specs=o_spec,
        out_shape=jax.ShapeDtypeStruct((S, W), BF16),
        compiler_params=_params(("parallel",)),
        name="dilated_combine",
    )(*outs, *lses)


def _softmax_step(t, m, l):
    m_new = jnp.maximum(m, jnp.max(t, axis=0, keepdims=True))
    alpha = jnp.exp2((m - m_new) * EXP_C)
    p = jnp.exp2((t - m_new) * EXP_C)
    return m_new, alpha, alpha * l + jnp.sum(p, axis=0, keepdims=True), p.astype(BF16)


def _pv(vt_ref, tile0, rows, p):
    out = None
    for c in range(p.shape[0] // VT):
        part = _dot(vt_ref[tile0 + c, rows, :], p[c * VT:(c + 1) * VT])
        out = part if out is None else out + part
    return out


def _causal_bias(j, q0, tk, tq):
    kpos = j * tk + lax.broadcasted_iota(I32, (tk, tq), 0)
    qpos = q0 + lax.broadcasted_iota(I32, (tk, tq), 1)
    return jnp.where(kpos <= qpos, 0.0, NEG)


def _forget_cumsum_body(f_ref, b_ref, g_ref, *, rows_per_head):
    x = f_ref[...] + b_ref[...]
    x = jnp.minimum(x, 0.0) - jnp.log1p(jnp.exp(-jnp.abs(x)))
    lane = lax.broadcasted_iota(I32, x.shape, 1)
    s = 1
    while s < LANE:
        x = x + jnp.where(lane >= s, pltpu.roll(x, s, axis=1), 0.0)
        s *= 2
    tot = jnp.broadcast_to(x[:, LANE - 1:LANE], x.shape)
    row = lax.broadcasted_iota(I32, x.shape, 0) % rows_per_head
    t = tot
    s = 1
    while s < rows_per_head:
        t = t + jnp.where(row >= s, pltpu.roll(t, s, axis=0), 0.0)
        s *= 2
    c = x + (t - tot)
    gam = c * (-1.0 / ATTN_SCALE)
    hi = gam.astype(BF16)
    r1 = gam - hi.astype(F32)
    mid = r1.astype(BF16)
    lo = (r1 - mid.astype(F32)).astype(BF16)
    g_ref[0] = hi
    g_ref[1] = mid
    g_ref[2] = lo


def _forget_decay(f_t, b_f):
    H, S = f_t.shape
    rph = S // LANE
    rows = H * rph
    b_col = jnp.repeat(b_f.astype(F32), rph).reshape(rows, 1)
    g = pl.pallas_call(
        functools.partial(_forget_cumsum_body, rows_per_head=rph),
        out_shape=jax.ShapeDtypeStruct((3, rows, LANE), BF16),
        name="forget_cumsum",
    )(f_t.reshape(rows, LANE), b_col)
    return g.reshape(3, H, S)


def _fox_body(q_ref, k_ref, vt_ref, o_ref, acc_ref, *, tq, tk, ka):
    i = pl.program_id(0)
    q0 = i * tq
    H = B_HEADS
    n_full = q0 // tk
    n_all = (q0 + tq + tk - 1) // tk
    acc_ref[...] = jnp.zeros(acc_ref.shape, F32)

    def tile(j, carry, masked):
        ms, ls = carry
        rows = pl.ds(pl.multiple_of(j * tk, tk), tk)
        ts = [_dot(k_ref[rows, h * ka:(h + 1) * ka], q_ref[h * ka:(h + 1) * ka, :]) for h in range(H)]
        bias = _causal_bias(j, q0, tk, tq) if masked else None
        steps = [_softmax_step(ts[h] + bias if masked else ts[h], ms[h], ls[h]) for h in range(H)]
        for h, (_, alpha, _, p) in enumerate(steps):
            acc_ref[h] = alpha * acc_ref[h] + _pv(vt_ref, j * (tk // VT), slice(h * HEAD_DIM, (h + 1) * HEAD_DIM), p)
        return tuple(s[0] for s in steps), tuple(s[2] for s in steps)

    init = (tuple(jnp.full((1, tq), M_INIT, F32) for _ in range(H)),
            tuple(jnp.zeros((1, tq), F32) for _ in range(H)))
    carry = lax.fori_loop(0, n_full, lambda j, c: tile(j, c, False), init)
    _, ls = lax.fori_loop(n_full, n_all, lambda j, c: tile(j, c, True), carry)
    for h in range(H):
        o_ref[h * HEAD_DIM:(h + 1) * HEAD_DIM, :] = (acc_ref[h] / ls[h]).astype(o_ref.dtype)


def _fox(q_aug_t, k_aug, v_t, v_blk, *, S):
    tq, tk = min(TQ, S), min(TK, S)
    H = B_HEADS
    ka = k_aug.shape[1] // H
    W = H * HEAD_DIM
    return pl.pallas_call(
        functools.partial(_fox_body, tq=tq, tk=tk, ka=ka),
        grid=(S // tq,),
        in_specs=[pl.BlockSpec((H * ka, tq), lambda i: (0, i)),
                  _resident((S, H * ka), lambda i: (0, 0)),
                  _resident((S // VT, W, VT), lambda i: (0, v_blk, 0))],
        out_specs=pl.BlockSpec((W, tq), lambda i: (0, i)),
        out_shape=jax.ShapeDtypeStruct((W, S), BF16),
        scratch_shapes=[pltpu.VMEM((H, HEAD_DIM, tq), F32)],
        compiler_params=_params(("parallel",)),
        name="fox_attn",
    )(q_aug_t, k_aug, v_t)


def _float_key(s):
    bits = pltpu.bitcast(s + 0.0, I32)
    return bits ^ (lax.shift_right_arithmetic(bits, 31) & 0x7FFFFFFF)


def _dsa_body(qi_ref, kidx_ref, misc_ref, q_ref, k_ref, vt_ref, o_ref, keys_ref, acc_ref,
              *, tq, tk, ts, topk, w_row):
    i = pl.program_id(0)
    q0 = i * tq
    H = C_HEADS
    n_tiles = (q0 + tq + tk - 1) // tk
    n_full = q0 // tk
    ns_all = (q0 + tq) // ts
    ns_full = q0 // ts
    w = misc_ref[w_row:w_row + IDX_HEADS, :] * (IDX_DIM ** -0.5 * IDX_HEADS ** -0.5)

    def score_tile(j, masked):
        rows = pl.ds(pl.multiple_of(j * ts, ts), ts)
        kt = kidx_ref[rows, :]
        acc = jnp.zeros((ts, tq), F32)
        for h in range(IDX_HEADS):
            rel = _dot(kt, qi_ref[h])
            acc = acc + jnp.maximum(rel, 0.0) * w[h:h + 1, :]
        key = _float_key(acc)
        if masked:
            kpos = j * ts + lax.broadcasted_iota(I32, (ts, tq), 0)
            qpos = q0 + lax.broadcasted_iota(I32, (ts, tq), 1)
            key = jnp.where(kpos <= qpos, key, NEG_KEY)
        keys_ref[rows, :] = key

    def _s_full(j, c):
        score_tile(j, False)
        return c

    def _s_mask(j, c):
        score_tile(j, True)
        return c

    lax.fori_loop(0, ns_full, _s_full, 0)
    lax.fori_loop(ns_full, ns_all, _s_mask, 0)

    @pl.when(ns_all * ts < n_tiles * tk)
    def _():
        keys_ref[pl.ds(pl.multiple_of(ns_all * ts, ts), ts), :] = jnp.full((ts, tq), NEG_KEY, I32)

    n_acc = 4

    def count_ge(cand):
        def body(c, acc):
            t = keys_ref[pl.ds(pl.multiple_of(c * tk, tk), tk), :]
            ind = jnp.where(t >= cand, 1.0, 0.0)
            part = jnp.sum(ind.reshape(tk // (n_acc * SUBLANE), n_acc * SUBLANE, tq), axis=0)
            return acc + part
        acc = lax.fori_loop(0, n_tiles, body, jnp.zeros((n_acc * SUBLANE, tq), F32))
        return jnp.sum(acc, axis=0, keepdims=True)

    kf = float(topk)
    lo = jnp.where(count_ge(jnp.zeros((1, tq), I32)) >= kf, 0, INT_MIN).astype(I32)

    def bit_step(b, lo):
        cand = lo + lax.shift_left(jnp.int32(1), 30 - b)
        return jnp.where(count_ge(cand) >= kf, cand, lo)

    thr = lax.fori_loop(0, 31, bit_step, lo)

    n_ge = count_ge(thr)
    tied = jnp.where((n_ge > kf) & (thr != NEG_KEY), 1.0, 0.0)
    has_ties = jnp.max(tied) > 0.0

    @pl.when(has_ties)
    def _():
        need = kf - count_ge(thr + 1)
        tri = (lax.broadcasted_iota(I32, (tk, tk), 0) >= lax.broadcasted_iota(I32, (tk, tk), 1)).astype(BF16)

        def demote(j, seen):
            rows = pl.ds(pl.multiple_of(j * tk, tk), tk)
            key = keys_ref[rows, :]
            eq = jnp.where(key == thr, 1.0, 0.0)
            rank = _dot(tri, eq.astype(BF16)) + seen
            keys_ref[rows, :] = jnp.where((eq > 0.5) & (rank > need), INT_MIN, key)
            return seen + jnp.sum(eq, axis=0, keepdims=True)

        lax.fori_loop(0, n_tiles, demote, jnp.zeros((1, tq), F32))

    def to_bias(j, masked):
        rows = pl.ds(pl.multiple_of(j * tk, tk), tk)
        sel = keys_ref[rows, :] >= thr
        if masked:
            kpos = j * tk + lax.broadcasted_iota(I32, (tk, tq), 0)
            qpos = q0 + lax.broadcasted_iota(I32, (tk, tq), 1)
            sel = sel & (kpos <= qpos)
        keys_ref[rows, :] = pltpu.bitcast(jnp.where(sel, 0.0, NEG), I32)

    def _b_full(j, c):
        to_bias(j, False)
        return c

    def _b_mask(j, c):
        to_bias(j, True)
        return c

    lax.fori_loop(0, n_full, _b_full, 0)
    lax.fori_loop(n_full, n_tiles, _b_mask, 0)

    acc_ref[...] = jnp.zeros(acc_ref.shape, F32)
    hd = [slice(h * HEAD_DIM, (h + 1) * HEAD_DIM) for h in range(H)]

    def tile(j, carry):
        ms, ls = carry
        rows = pl.ds(pl.multiple_of(j * tk, tk), tk)
        ts = [_dot(k_ref[rows, hd[h]], q_ref[hd[h], :]) for h in range(H)]
        bias = pltpu.bitcast(keys_ref[rows, :], F32)
        steps = [_softmax_step(ts[h] + bias, ms[h], ls[h]) for h in range(H)]
        for h, (_, alpha, _, p) in enumerate(steps):
            acc_ref[h] = alpha * acc_ref[h] + _pv(vt_ref, j * (tk // VT), hd[h], p)
        return tuple(s[0] for s in steps), tuple(s[2] for s in steps)

    init = (tuple(jnp.full((1, tq), M_INIT, F32) for _ in range(H)),
            tuple(jnp.zeros((1, tq), F32) for _ in range(H)))
    _, ls = lax.fori_loop(0, n_tiles, tile, init)
    for h in range(H):
        o_ref[h * HEAD_DIM:(h + 1) * HEAD_DIM, :] = (acc_ref[h] / jnp.maximum(ls[h], TINY)).astype(o_ref.dtype)


def _dsa(qi_t, kidx, misc_t, w_row, q_t, q_blk, k_arr, k_blk, v_t, v_blk, *, S):
    tq, tk = min(TQ, S), min(TK, S)
    ts = min(256, tk)
    topk = min(DSA_TOPK, S // 4)
    W = C_HEADS * HEAD_DIM
    assert S % tk == 0 and tk % ts == 0 and tq % ts == 0
    kern = functools.partial(_dsa_body, tq=tq, tk=tk, ts=ts, topk=topk, w_row=w_row)
    return pl.pallas_call(
        kern,
        grid=(S // tq,),
        in_specs=[pl.BlockSpec((IDX_HEADS, IDX_DIM, tq), lambda i: (0, 0, i)),
                  _resident((S, IDX_DIM), lambda i: (0, 0)),
                  pl.BlockSpec((LANE, tq), lambda i: (0, i)),
                  pl.BlockSpec((W, tq), lambda i: (q_blk, i)),
                  _resident((S, W), lambda i: (0, k_blk)),
                  _resident((S // VT, W, VT), lambda i: (0, v_blk, 0))],
        out_specs=pl.BlockSpec((W, tq), lambda i: (0, i)),
        out_shape=jax.ShapeDtypeStruct((W, S), BF16),
        scratch_shapes=[pltpu.VMEM((S, tq), I32), pltpu.VMEM((C_HEADS, HEAD_DIM, tq), F32)],
        compiler_params=_params(("arbitrary",)),
        name="dsa_attn",
    )(qi_t, kidx, misc_t, q_t, k_arr, v_t)


def _compress_body(x_ref, pea_ref, peb_ref, w1a_ref, w1b_ref, w2_ref, o_ref, ot_ref):
    x = x_ref[...].astype(F32)
    a = _dot((x + pea_ref[...]).astype(BF16), w1a_ref[...].astype(BF16))
    b = _dot((x + peb_ref[...]).astype(BF16), w1b_ref[...].astype(BF16))
    n = a.shape[0]
    hid = a + pltpu.roll(b, n - 1, axis=0)
    out = _dot(jax.nn.gelu(hid).astype(BF16), w2_ref[...].astype(BF16))
    o_ref[...] = out.astype(o_ref.dtype)
    ot_ref[...] = out.T.astype(ot_ref.dtype)


def _compress(x2, pe, w1, w2):
    _, S, Dh = x2.shape
    n = S // CMP_STRIDE
    half = CMP_STRIDE * Dh
    xr = x2.reshape(2, n, half)
    pe_a = pe[:, :CMP_STRIDE].reshape(2, 1, half)
    pe_b = pe[:, CMP_STRIDE:].reshape(2, 1, half)
    return pl.pallas_call(
        _compress_body,
        grid=(2,),
        in_specs=[pl.BlockSpec((None, n, half), lambda g: (g, 0, 0)),
                  pl.BlockSpec((None, 1, half), lambda g: (g, 0, 0)),
                  pl.BlockSpec((None, 1, half), lambda g: (g, 0, 0)),
                  pl.BlockSpec((None, half, CMP_HIDDEN), lambda g: (g, 0, 0)),
                  pl.BlockSpec((None, half, CMP_HIDDEN), lambda g: (g, 1, 0)),
                  pl.BlockSpec((None, CMP_HIDDEN, Dh), lambda g: (g, 0, 0))],
        out_specs=[pl.BlockSpec((None, n, Dh), lambda g: (g, 0, 0)),
                   pl.BlockSpec((None, Dh, n), lambda g: (g, 0, 0))],
        out_shape=[jax.ShapeDtypeStruct((2, n, Dh), BF16), jax.ShapeDtypeStruct((2, Dh, n), BF16)],
        compiler_params=_params(("parallel",)),
        name="nsa_compress",
    )(xr, pe_a, pe_b, w1, w1, w2)


def _nsa_body(q_ref, kc_ref, vct_ref, ks_ref, kw_ref, vst_ref, vwt_ref, misc_ref, ov_ref, o_ref, acc_ref,
              *, tq, tk, n_slc, n_sel, ncp, gate_row):
    i = pl.program_id(0)
    q0 = i * tq
    H = D_HEADS
    R = H * tq
    blk_shift = int(math.log2(SLC_BLOCK))
    qs = jnp.concatenate([q_ref[h * HEAD_DIM:(h + 1) * HEAD_DIM, :] for h in range(H)], axis=1)

    def qpos_of(shape):
        return q0 + (lax.broadcasted_iota(I32, shape, 1) & (tq - 1))

    lc = _dot(kc_ref[...], qs)
    cend = lax.broadcasted_iota(I32, (ncp, R), 0) * CMP_STRIDE + (CMP_LEN - 1)
    mc = cend <= qpos_of((ncp, R))
    lc = jnp.where(mc, lc, NEG)
    mx = jnp.max(lc, axis=0, keepdims=True)
    pc = jnp.where(mc, jnp.exp2((lc - mx) * EXP_C), 0.0)
    pc = pc * (1.0 / jnp.maximum(jnp.sum(pc, axis=0, keepdims=True), TINY))
    o_cmp = _dot(vct_ref[...], pc.astype(BF16))

    psum = pc[:, 0:tq]
    for h in range(1, H):
        psum = psum + pc[:, h * tq:(h + 1) * tq]
    p_hi = psum.astype(BF16)
    p_lo = (psum - p_hi.astype(F32)).astype(BF16)
    ov = ov_ref[...]
    imp = _dot(ov, p_hi) + _dot(ov, p_lo)
    jblk = lax.broadcasted_iota(I32, (n_slc, tq), 0)
    jblk_f = jblk.astype(F32)
    qpos_l = q0 + lax.broadcasted_iota(I32, (n_slc, tq), 1)
    cur = lax.shift_right_logical(qpos_l, blk_shift)
    forced = (jblk == 0) | (jblk == cur) | (jblk == cur - 1)
    val = jnp.where(forced, FORCE, jnp.where(jblk * SLC_BLOCK <= qpos_l, imp, NEG))
    sel_t = jnp.zeros((n_slc, tq), F32)
    for _ in range(n_sel):
        mxv = jnp.max(val, axis=0, keepdims=True)
        first = jnp.min(jnp.where(val == mxv, jblk_f, float(n_slc)), axis=0, keepdims=True)
        hit = jblk_f == first
        sel_t = jnp.where(hit, 1.0, sel_t)
        val = jnp.where(hit, -jnp.inf, val)
    sel_b = sel_t.astype(BF16)

    bpt = tk // SLC_BLOCK
    n_tiles = (q0 + tq + tk - 1) // tk
    acc_ref[...] = jnp.zeros(acc_ref.shape, F32)

    def slc_tile(j, carry):
        m, l = carry
        rows = pl.ds(pl.multiple_of(j * tk, tk), tk)
        t = _dot(ks_ref[rows, :], qs)
        eb = lax.shift_right_logical(lax.broadcasted_iota(I32, (tk, n_slc), 0), blk_shift) + j * bpt
        expand = jnp.where(eb == lax.broadcasted_iota(I32, (tk, n_slc), 1), 1.0, 0.0).astype(BF16)
        picked = _dot(expand, sel_b)
        kpos = j * tk + lax.broadcasted_iota(I32, (tk, tq), 0)
        qpos = q0 + lax.broadcasted_iota(I32, (tk, tq), 1)
        bias = jnp.where((picked > 0.5) & (kpos <= qpos), 0.0, NEG)
        t = t + jnp.concatenate([bias] * H, axis=1)
        m_new, alpha, l_new, p = _softmax_step(t, m, l)
        acc_ref[...] = alpha * acc_ref[...] + _pv(vst_ref, j * (tk // VT), slice(None), p)
        return m_new, l_new

    init = (jnp.full((1, R), M_INIT, F32), jnp.zeros((1, R), F32))
    _, l_s = lax.fori_loop(0, n_tiles, slc_tile, init)
    o_slc = acc_ref[...] * (1.0 / l_s)

    wlen = WIN + tq
    start = pl.multiple_of(jnp.maximum(q0 - WIN, 0), tq)
    lw = _dot(kw_ref[pl.ds(start, wlen), :], qs)
    dist = qpos_of((wlen, R)) - (start + lax.broadcasted_iota(I32, (wlen, R), 0))
    lw = jnp.where((dist >= 0) & (dist < WIN), lw, NEG)
    mxw = jnp.max(lw, axis=0, keepdims=True)
    pw = jnp.exp2((lw - mxw) * EXP_C)
    den = jnp.sum(pw, axis=0, keepdims=True)
    o_win = _pv(vwt_ref, start // VT, slice(None), pw.astype(BF16)) * (1.0 / den)

    gates = jax.nn.sigmoid(misc_ref[gate_row:gate_row + 3 * H, :])
    for h in range(H):
        c = slice(h * tq, (h + 1) * tq)
        out = (gates[3 * h:3 * h + 1] * o_cmp[:, c] + gates[3 * h + 1:3 * h + 2] * o_slc[:, c]
               + gates[3 * h + 2:3 * h + 3] * o_win[:, c])
        o_ref[h * HEAD_DIM:(h + 1) * HEAD_DIM, :] = out.astype(o_ref.dtype)


def _nsa(q_t, q_blk, kc, vct, k_arr, ks_blk, kw_blk, v_t, vs_blk, vw_blk, misc_t, gate_row, *, S):
    tq, tk = min(TQ_NSA, S), min(TK, S)
    assert S >= WIN + tq and tq & (tq - 1) == 0 and tq == VT and WIN % VT == 0
    n_slc = S // SLC_BLOCK
    n_sel = min(SLC_COUNT, n_slc)
    ncp = S // CMP_STRIDE
    W = D_HEADS * HEAD_DIM
    ci = np.arange(ncp)[None, :]
    sj = np.arange(n_slc)[:, None]
    ov = ((ci * CMP_STRIDE < (sj + 1) * SLC_BLOCK) & (ci * CMP_STRIDE + CMP_LEN > sj * SLC_BLOCK)
          & (ci < ncp - 1))
    ov = jnp.asarray(ov, BF16)
    kern = functools.partial(_nsa_body, tq=tq, tk=tk, n_slc=n_slc, n_sel=n_sel, ncp=ncp, gate_row=gate_row)
    return pl.pallas_call(
        kern,
        grid=(S // tq,),
        in_specs=[pl.BlockSpec((W, tq), lambda i: (q_blk, i)),
                  _resident((ncp, HEAD_DIM), lambda i: (0, 0)),
                  _resident((HEAD_DIM, ncp), lambda i: (0, 0)),
                  _resident((S, HEAD_DIM), lambda i: (0, ks_blk)),
                  _resident((S, HEAD_DIM), lambda i: (0, kw_blk)),
                  _resident((S // VT, HEAD_DIM, VT), lambda i: (0, vs_blk, 0)),
                  _resident((S // VT, HEAD_DIM, VT), lambda i: (0, vw_blk, 0)),
                  pl.BlockSpec((LANE, tq), lambda i: (0, i)),
                  _resident((n_slc, ncp), lambda i: (0, 0))],
        out_specs=pl.BlockSpec((W, tq), lambda i: (0, i)),
        out_shape=jax.ShapeDtypeStruct((W, S), BF16),
        scratch_shapes=[pltpu.VMEM((HEAD_DIM, D_HEADS * tq), F32)],
        compiler_params=_params(("parallel",)),
        name="nsa_attn",
    )(q_t, kc, vct, k_arr, k_arr, v_t, v_t, misc_t, ov)


def _rope_tables(S):
    pos = jnp.arange(S, dtype=F32)[:, None]
    half = HEAD_DIM // 2
    ang = pos * (ROPE_THETA ** (-jnp.arange(half, dtype=F32) / half))[None, :]
    cos, sin = jnp.cos(ang), jnp.sin(ang)
    t128 = (jnp.concatenate([cos, cos], 1), jnp.concatenate([-sin, sin], 1))
    h64 = IDX_DIM // 2
    ang = pos * (ROPE_THETA ** (-jnp.arange(h64, dtype=F32) / h64))[None, :]
    cos, sin = jnp.cos(ang), jnp.sin(ang)
    z = jnp.zeros_like(sin)
    t64 = (jnp.concatenate([cos] * 4, 1), jnp.concatenate([-sin, z, -sin, z], 1),
           jnp.concatenate([z, sin, z, sin], 1))
    return t128, t64


def _split_w_in(w_in):
    offs = np.cumsum([0,
                      A_HEADS * HEAD_DIM, A_HEADS * HEAD_DIM, A_HEADS * HEAD_DIM,
                      B_HEADS * HEAD_DIM, B_HEADS * HEAD_DIM, B_HEADS * HEAD_DIM, B_HEADS,
                      C_HEADS * HEAD_DIM, C_HEADS * HEAD_DIM, C_HEADS * HEAD_DIM,
                      IDX_HEADS * IDX_DIM, IDX_DIM, IDX_HEADS,
                      D_HEADS * HEAD_DIM] + [HEAD_DIM] * 6 + [3 * D_HEADS])
    names = ["aq", "ak", "av", "bq", "bk", "bv", "bf", "cq", "ck", "cv", "cqi", "cki", "cwi",
             "dq", "dkc", "dvc", "dks", "dvs", "dkw", "dvw", "dg"]
    col = {n: w_in[:, offs[k]:offs[k + 1]] for k, n in enumerate(names)}
    D = w_in.shape[0]

    def pack(ns, pad=0):
        parts = [col[n] for n in ns]
        if pad:
            parts.append(jnp.zeros((D, pad), w_in.dtype))
        return jnp.concatenate(parts, axis=1).astype(BF16)

    n_misc = B_HEADS + IDX_HEADS + 3 * D_HEADS
    return dict(
        a_qk=pack(["aq", "ak"]),
        a_v=pack(["av"]),
        b_k=pack(["bk", "dvc"]),
        b_q=pack(["bq"]),
        cd_k=pack(["ck", "dkc", "dks", "dkw"]),
        cd_q=pack(["cq", "dq"]),
        idx=pack(["cqi", "cki"], pad=LANE - IDX_DIM),
        v_t=pack(["bv", "cv", "dvs", "dvw"]),
        misc=pack(["bf", "cwi", "dg"], pad=LANE - n_misc),
    )


def _deinterleave(a, d):
    S, W = a.shape
    return a.reshape(S // d, d, W).transpose(1, 0, 2).reshape(S, W)


def _interleave(a, d):
    S, W = a.shape
    return a.reshape(d, S // d, W).transpose(1, 0, 2).reshape(S, W)


def _pick(n, cands):
    for c in cands:
        if n % c == 0:
            return c
    return n


def _layer(x, p, tables):
    S, D = x.shape
    (cos128, sin128), (cos64, sin64a, sin64b) = tables
    tm = _pick(S, (1024, 512, 256))
    h = _rmsnorm(x, p["norm_mix"], BF16)
    w = _split_w_in(p["w_in"])

    def proj(name, mode="plain", extras=(), out_dtype=BF16, layout="rows"):
        n = w[name].shape[1]
        tn = _pick(n, (1024, 896, 768, 640, 512, 384, 256, 128))
        return _mm(h, w[name], tm=tm, tn=tn, out_dtype=out_dtype, mode=mode, extras=extras, layout=layout,
                   name="in_" + name)

    a_qk = proj("a_qk", "rope128", (cos128, sin128))
    a_v = proj("a_v")
    b_k = proj("b_k")
    b_q_t = proj("b_q", layout="cols")
    cd_k = proj("cd_k", "rope128", (cos128, sin128))
    cd_q_t = proj("cd_q", "rope128", (cos128, sin128), layout="cols")
    idx_t = proj("idx", "rope64", (cos64, sin64a, sin64b), layout="cols")
    v_t = proj("v_t", layout="coltiles")
    misc_t = proj("misc", out_dtype=F32, layout="cols")

    G = A_HEADS_PER_GROUP
    AW = A_HEADS * HEAD_DIM
    outs, lses = [], []
    for g, (window, dilation) in enumerate(DIL_PATTERNS):
        assert window // dilation == QB
        if dilation == 1:
            o, l = _dilated_group(a_qk, g * G, a_qk, A_HEADS + g * G, a_v, g * G, S=S, dilation=1)
        else:
            W = G * HEAD_DIM
            qd = _deinterleave(a_qk[:, g * W:(g + 1) * W], dilation)
            kd = _deinterleave(a_qk[:, AW + g * W:AW + (g + 1) * W], dilation)
            vd = _deinterleave(a_v[:, g * W:(g + 1) * W], dilation)
            o, l = _dilated_group(qd, 0, kd, 0, vd, 0, S=S, dilation=dilation)
            o = _interleave(o, dilation)
            l = l.reshape(G, dilation, S // dilation).transpose(0, 2, 1).reshape(G, S, 1)
        outs.append(o)
        lses.append(l)
    o_a = _combine_a(outs, lses)

    BW = B_HEADS * HEAD_DIM
    gam = _forget_decay(misc_t[:B_HEADS], p["b_f"])
    pad = HEAD_DIM - 3
    bq_t = b_q_t.reshape(B_HEADS, HEAD_DIM, S)
    bk = b_k[:, :BW].reshape(S, B_HEADS, HEAD_DIM)
    q_aug_t = jnp.concatenate([bq_t, jnp.ones((B_HEADS, 3, S), BF16), jnp.zeros((B_HEADS, pad, S), BF16)], axis=1)
    k_aug = jnp.concatenate([bk, gam.transpose(2, 1, 0), jnp.zeros((S, B_HEADS, pad), BF16)], axis=2)
    o_b_t = _fox(q_aug_t.reshape(-1, S), k_aug.reshape(S, -1), v_t, 0, S=S)

    n_qi = IDX_HEADS * IDX_DIM
    qi_t = idx_t[:n_qi].reshape(IDX_HEADS, IDX_DIM, S)
    kidx = idx_t[n_qi:n_qi + IDX_DIM].T
    o_c_t = _dsa(qi_t, kidx, misc_t, B_HEADS, cd_q_t, 0, cd_k, 0, v_t, 1, S=S)

    cw = C_HEADS * HEAD_DIM
    nb = cw // HEAD_DIM
    kv_c = jnp.stack([cd_k[:, cw:cw + HEAD_DIM], b_k[:, BW:BW + HEAD_DIM]])
    kv_cmp, kv_cmp_t = _compress(kv_c, p["cmp_pe"], p["cmp_w1"], p["cmp_w2"])
    o_d_t = _nsa(cd_q_t, 1, kv_cmp[0], kv_cmp_t[1], cd_k, nb + 1, nb + 2, v_t, 2 * nb, 2 * nb + 1,
                 misc_t, B_HEADS + IDX_HEADS, S=S)

    tn = _pick(D, (512, 256, 128))
    wg = p["w_gate"].reshape(D, N_BRANCH * D).astype(BF16)
    bg = p["b_gate"].reshape(N_BRANCH, 1, D)
    merged = _gate_merge(h, wg, bg, (o_a, o_b_t, o_c_t, o_d_t), p["w_branch"].astype(BF16), tm=tm, tn=tn)
    x = _mm(merged, p["w_out"].astype(BF16), tm=tm, tn=tn, out_dtype=F32, mode="resid", extras=(x,), name="out_proj")

    h2 = _rmsnorm(x, p["norm_ffn"], BF16)
    Hf = p["w_ffn_in"].shape[1] // 2
    act = _mm_swiglu(h2, p["w_ffn_in"].astype(BF16), tm=tm, tn=_pick(Hf, (512, 256, 128)))
    x = _mm(act, p["w_ffn_out"].astype(BF16), tm=_pick(S, (512, 256)), tn=tn, out_dtype=F32, mode="resid",
            extras=(x,), name="ffn_out")
    return x


def kernel(x, norm_mix, w_in, w_gate, b_gate, b_f, cmp_pe, cmp_w1, cmp_w2, w_branch, w_out, norm_ffn,
           w_ffn_in, w_ffn_out, norm_final):
    B, S, D = x.shape
    tables = _rope_tables(S)
    outs = []
    for b in range(B):
        xb = x[b]
        for l in range(norm_mix.shape[0]):
            p = dict(norm_mix=norm_mix[l], w_in=w_in[l], w_gate=w_gate[l], b_gate=b_gate[l], b_f=b_f[l],
                     cmp_pe=cmp_pe[l], cmp_w1=cmp_w1[l], cmp_w2=cmp_w2[l], w_branch=w_branch[l],
                     w_out=w_out[l], norm_ffn=norm_ffn[l], w_ffn_in=w_ffn_in[l], w_ffn_out=w_ffn_out[l])
            xb = _layer(xb, p, tables)
        outs.append(_rmsnorm(xb, norm_final, x.dtype))
    return jnp.stack(outs)
```

```python
import functools
import math

import jax
import jax.numpy as jnp
import numpy as np
from jax import lax
from jax.experimental import pallas as pl
from jax.experimental.pallas import tpu as pltpu

F32 = jnp.float32
BF16 = jnp.bfloat16
I32 = jnp.int32

HEAD_DIM = 128
ROPE_THETA = 10000.0
ATTN_SCALE = HEAD_DIM ** -0.5
QB = 128
NEG = -1e30
TINY = 1e-30
DIL_PATTERNS = ((128, 1), (512, 4), (2048, 16))
A_HEADS_PER_GROUP = 4
A_HEADS = A_HEADS_PER_GROUP * len(DIL_PATTERNS)
B_HEADS = 4
C_HEADS = 4
IDX_HEADS = 16
IDX_DIM = 64
DSA_TOPK = 256
D_HEADS = 4
CMP_LEN = 32
CMP_STRIDE = 16
CMP_HIDDEN = 256
SLC_BLOCK = 64
SLC_COUNT = 16
WIN = 512
FORCE = 1e9
N_BRANCH = 4
BRANCH_WIDTH = 4 * HEAD_DIM

LANE = 128
SUBLANE = 8
VMEM_LIMIT = 56 * 1024 * 1024
VT = 256
TQ = 512
TQ_NSA = 256
TK = 512

LOG2E = math.log2(math.e)
EXP_C = ATTN_SCALE * LOG2E
M_INIT = 0.5 * NEG
DEN_ROWS = 16
PV_ROWS = HEAD_DIM + DEN_ROWS

INT_MIN = -2 ** 31
_NEG_BITS = int(np.array(NEG, np.float32).view(np.int32))
NEG_KEY = _NEG_BITS ^ 0x7FFFFFFF


def _params(sem):
    return pltpu.CompilerParams(dimension_semantics=sem, vmem_limit_bytes=VMEM_LIMIT)


def _resident(shape, index_map):
    return pl.BlockSpec(shape, index_map, pipeline_mode=pl.Buffered(1))


def _dot(a, b):
    return jnp.dot(a, b, preferred_element_type=F32)


def _dot_nt(a, b):
    return lax.dot_general(a, b, (((1,), (1,)), ((), ())), preferred_element_type=F32)


def _dot_tn(a, b):
    return lax.dot_general(a, b, (((0,), (0,)), ((), ())), preferred_element_type=F32)


def _rmsnorm_body(x_ref, g_ref, o_ref):
    x = x_ref[...]
    ms = jnp.mean(x * x, axis=-1, keepdims=True)
    o_ref[...] = (x * lax.rsqrt(ms + 1e-6) * g_ref[...]).astype(o_ref.dtype)


def _rmsnorm(x, g, out_dtype, tm=512):
    S, D = x.shape
    return pl.pallas_call(
        _rmsnorm_body,
        grid=(S // tm,),
        in_specs=[pl.BlockSpec((tm, D), lambda i: (i, 0)),
                  pl.BlockSpec((1, D), lambda i: (0, 0))],
        out_specs=pl.BlockSpec((tm, D), lambda i: (i, 0)),
        out_shape=jax.ShapeDtypeStruct((S, D), out_dtype),
        compiler_params=_params(("parallel",)),
        name="rmsnorm",
    )(x, g.reshape(1, D))


def _rope128(acc, cos, sin):
    parts = []
    for c in range(acc.shape[1] // LANE):
        blk = acc[:, c * LANE:(c + 1) * LANE]
        parts.append(blk * cos + pltpu.roll(blk, LANE // 2, axis=1) * sin)
    return parts[0] if len(parts) == 1 else jnp.concatenate(parts, axis=1)


def _rope64(acc, cos, sin_a, sin_b):
    parts = []
    for c in range(acc.shape[1] // LANE):
        blk = acc[:, c * LANE:(c + 1) * LANE]
        parts.append(blk * cos + pltpu.roll(blk, LANE - IDX_DIM // 2, axis=1) * sin_a
                     + pltpu.roll(blk, IDX_DIM // 2, axis=1) * sin_b)
    return parts[0] if len(parts) == 1 else jnp.concatenate(parts, axis=1)


def _cast_once(src_ref, dst_ref):
    @pl.when(pl.program_id(1) == 0)
    def _():
        dst_ref[...] = src_ref[...].astype(BF16)


def _mm_body(*refs, mode, layout, cast_w, scale):
    a_ref, b_ref = refs[0], refs[1]
    if cast_w:
        o_ref, wb_ref = refs[-2], refs[-1]
        _cast_once(b_ref, wb_ref)
        b_ref = wb_ref
    else:
        o_ref = refs[-1]
    acc = _dot(a_ref[...], b_ref[...])
    if mode == "rope128":
        acc = _rope128(acc, refs[2][...], refs[3][...])
    elif mode == "rope64":
        acc = _rope64(acc, refs[2][...], refs[3][...], refs[4][...])
    elif mode == "resid":
        acc = refs[2][...] + acc
    if scale is not None:
        acc = acc * scale
    if layout == "rows":
        o_ref[...] = acc.astype(o_ref.dtype)
    elif layout == "cols":
        o_ref[...] = acc.T.astype(o_ref.dtype)
    else:
        acc_t = acc.T
        for c in range(acc_t.shape[1] // VT):
            o_ref[c] = acc_t[:, c * VT:(c + 1) * VT].astype(o_ref.dtype)


def _mm(a, b, *, tm, tn, out_dtype, mode="plain", extras=(), layout="rows", name="mm", layer=None, scale=None):
    M, K = a.shape
    N = b.shape[-1]
    assert M % tm == 0 and N % tn == 0, (M, N, tm, tn)
    cast_w = layer is not None
    if cast_w:
        b_spec = pl.BlockSpec((None, K, tn), lambda j, i: (layer, 0, j))
    else:
        b_spec = pl.BlockSpec((K, tn), lambda j, i: (0, j))
    in_specs = [pl.BlockSpec((tm, K), lambda j, i: (i, 0)), b_spec]
    if mode in ("rope128", "rope64"):
        in_specs += [pl.BlockSpec((tm, LANE), lambda j, i: (i, 0)) for _ in extras]
    elif mode == "resid":
        in_specs += [pl.BlockSpec((tm, tn), lambda j, i: (i, j))]
    if layout == "rows":
        out_spec = pl.BlockSpec((tm, tn), lambda j, i: (i, j))
        out_shape = (M, N)
    elif layout == "cols":
        out_spec = pl.BlockSpec((tn, tm), lambda j, i: (j, i))
        out_shape = (N, M)
    else:
        assert tm % VT == 0
        out_spec = pl.BlockSpec((tm // VT, tn, VT), lambda j, i: (i, j, 0))
        out_shape = (M // VT, N, VT)
    return pl.pallas_call(
        functools.partial(_mm_body, mode=mode, layout=layout, cast_w=cast_w, scale=scale),
        grid=(N // tn, M // tm),
        in_specs=in_specs,
        out_specs=out_spec,
        out_shape=jax.ShapeDtypeStruct(out_shape, out_dtype),
        scratch_shapes=[pltpu.VMEM((K, tn), BF16)] if cast_w else [],
        compiler_params=_params(("parallel", "arbitrary" if cast_w else "parallel")),
        name=name,
    )(a, b, *extras)


def _swiglu_body(a_ref, bg_ref, bu_ref, o_ref, wg_ref, wu_ref):
    _cast_once(bg_ref, wg_ref)
    _cast_once(bu_ref, wu_ref)
    a = a_ref[...]
    g = _dot(a, wg_ref[...])
    u = _dot(a, wu_ref[...])
    o_ref[...] = (g * jax.nn.sigmoid(g) * u).astype(o_ref.dtype)


def _mm_swiglu(a, w, layer, *, tm, tn):
    M, K = a.shape
    H = w.shape[-1] // 2
    nj = H // tn
    assert H % tn == 0 and M % tm == 0
    return pl.pallas_call(
        _swiglu_body,
        grid=(nj, M // tm),
        in_specs=[pl.BlockSpec((tm, K), lambda j, i: (i, 0)),
                  pl.BlockSpec((None, K, tn), lambda j, i: (layer, 0, j)),
                  pl.BlockSpec((None, K, tn), lambda j, i: (layer, 0, j + nj))],
        out_specs=pl.BlockSpec((tm, tn), lambda j, i: (i, j)),
        out_shape=jax.ShapeDtypeStruct((M, H), BF16),
        scratch_shapes=[pltpu.VMEM((K, tn), BF16), pltpu.VMEM((K, tn), BF16)],
        compiler_params=_params(("parallel", "arbitrary")),
        name="ffn_in_swiglu",
    )(a, w, w)


def _gate_merge_body(h_ref, wg0, wg1, wg2, wg3, bg_ref, br0, br1, br2, br3, wb_ref, o_ref, wgs_ref, wbs_ref):
    wgs = (wg0, wg1, wg2, wg3)

    @pl.when(pl.program_id(1) == 0)
    def _():
        for n in range(N_BRANCH):
            wgs_ref[n] = wgs[n][...].astype(BF16)
        wbs_ref[...] = wb_ref[...].astype(BF16)

    h = h_ref[...]
    acc = None
    for n, br in enumerate((br0, br1, br2, br3)):
        gl = _dot(h, wgs_ref[n]) + bg_ref[n]
        y = _dot(br[...], wbs_ref[n]) if n == 0 else _dot_tn(br[...], wbs_ref[n])
        t = jax.nn.sigmoid(gl) * y
        acc = t if acc is None else acc + t
    o_ref[...] = acc.astype(o_ref.dtype)


def _gate_merge(h, wg, bg, branches, wb, layer, *, tm, tn):
    S, D = h.shape
    nj = D // tn
    wg_specs = [pl.BlockSpec((None, D, tn), (lambda j, i, n=n: (layer, 0, n * nj + j))) for n in range(N_BRANCH)]
    br_specs = [pl.BlockSpec((tm, BRANCH_WIDTH), lambda j, i: (i, 0))]
    br_specs += [pl.BlockSpec((BRANCH_WIDTH, tm), lambda j, i: (0, i)) for _ in range(N_BRANCH - 1)]
    return pl.pallas_call(
        _gate_merge_body,
        grid=(nj, S // tm),
        in_specs=[pl.BlockSpec((tm, D), lambda j, i: (i, 0))] + wg_specs
        + [pl.BlockSpec((None, N_BRANCH, 1, tn), lambda j, i: (layer, 0, 0, j))] + br_specs
        + [pl.BlockSpec((None, N_BRANCH, BRANCH_WIDTH, tn), lambda j, i: (layer, 0, 0, j))],
        out_specs=pl.BlockSpec((tm, tn), lambda j, i: (i, j)),
        out_shape=jax.ShapeDtypeStruct((S, D), BF16),
        scratch_shapes=[pltpu.VMEM((N_BRANCH, D, tn), BF16), pltpu.VMEM((N_BRANCH, BRANCH_WIDTH, tn), BF16)],
        compiler_params=_params(("parallel", "arbitrary")),
        name="gate_merge",
    )(h, wg, wg, wg, wg, bg, *branches, wb)


def _dilated_body(q_ref, k_ref, v_ref, o_ref, lse_ref, *, ta):
    i = pl.program_id(2)
    nblk = ta // QB
    qi = lax.broadcasted_iota(I32, (QB, 2 * QB), 0)
    kj = lax.broadcasted_iota(I32, (QB, 2 * QB), 1)
    bias_first = jnp.where(kj <= qi, 0.0, NEG)
    bias_rest = jnp.where((kj >= qi) & (kj <= qi + QB), 0.0, NEG)
    wins, logits = [], []
    for n in range(nblk):
        b = i * nblk + n
        win = pl.ds(pl.multiple_of(jnp.maximum(b - 1, 0) * QB, QB), 2 * QB)
        s = _dot_nt(q_ref[n * QB:(n + 1) * QB, :], k_ref[win, :]) * ATTN_SCALE
        logits.append(s + jnp.where(b == 0, bias_first, bias_rest))
        wins.append(win)
    stats = []
    for s in logits:
        m = jnp.max(s, axis=-1, keepdims=True)
        p = jnp.exp(s - m)
        stats.append((m, p, jnp.sum(p, axis=-1, keepdims=True)))
    for n, (m, p, den) in enumerate(stats):
        pv = _dot(p.astype(BF16), v_ref[wins[n], :])
        o_ref[n * QB:(n + 1) * QB, :] = pv / den
        lse_ref[n * QB:(n + 1) * QB, :] = jnp.broadcast_to(m + jnp.log(den), (QB, HEAD_DIM))


def _dilated_group(qk_arr, v_arr, g, *, S, dilation, ta=1024):
    d = dilation
    L = S // d
    ta = min(ta, L)
    G = A_HEADS_PER_GROUP
    assert L % ta == 0 and L >= 2 * QB
    nq = qk_arr.shape[1] // HEAD_DIM
    nv = v_arr.shape[1] // HEAD_DIM
    out_sds = jax.ShapeDtypeStruct((L, d * G * HEAD_DIM), F32)
    o, lse = pl.pallas_call(
        functools.partial(_dilated_body, ta=ta),
        grid=(G, d, L // ta),
        in_specs=[pl.BlockSpec((ta, HEAD_DIM), lambda h, r, i: (i, r * nq + g * G + h)),
                  pl.BlockSpec((L, HEAD_DIM), lambda h, r, i: (0, r * nq + A_HEADS + g * G + h)),
                  pl.BlockSpec((L, HEAD_DIM), lambda h, r, i: (0, r * nv + g * G + h))],
        out_specs=[pl.BlockSpec((ta, HEAD_DIM), lambda h, r, i: (i, r * G + h)),
                   pl.BlockSpec((ta, HEAD_DIM), lambda h, r, i: (i, r * G + h))],
        out_shape=[out_sds, out_sds],
        compiler_params=_params(("parallel", "parallel", "parallel")),
        name="dilated_attn",
    )(qk_arr.reshape(L, d * qk_arr.shape[1]), qk_arr.reshape(L, d * qk_arr.shape[1]),
      v_arr.reshape(L, d * v_arr.shape[1]))
    return o.reshape(S, G * HEAD_DIM), lse.reshape(S, G * HEAD_DIM)


def _combine_a_body(o0, o1, o2, l0, l1, l2, out_ref):
    a0, a1, a2 = l0[...], l1[...], l2[...]
    m = jnp.maximum(jnp.maximum(a0, a1), a2)
    e0, e1, e2 = jnp.exp(a0 - m), jnp.exp(a1 - m), jnp.exp(a2 - m)
    out = (e0 * o0[...] + e1 * o1[...] + e2 * o2[...]) / (e0 + e1 + e2)
    out_ref[...] = out.astype(out_ref.dtype)


def _combine_a(outs, lses, *, tm=512):
    S, W = outs[0].shape
    spec = pl.BlockSpec((tm, W), lambda i: (i, 0))
    return pl.pallas_call(
        _combine_a_body,
        grid=(S // tm,),
        in_specs=[spec] * 6,
        out_specs=spec,
        out_shape=jax.ShapeDtypeStruct((S, W), BF16),
        compiler_params=_params(("parallel",)),
        name="dilated_combine",
    )(*outs, *lses)


def _softmax_step(t, m):
    m_new = jnp.maximum(m, jnp.max(t, axis=0, keepdims=True))
    return m_new, jnp.exp2(m - m_new), jnp.exp2(t - m_new).astype(BF16)


def _pv(vt_ref, tile0, rows, p):
    ones = jnp.ones((DEN_ROWS, VT), BF16)
    out = None
    for c in range(p.shape[0] // VT):
        v_aug = jnp.concatenate([vt_ref[tile0 + c, rows, :], ones], axis=0)
        part = _dot(v_aug, p[c * VT:(c + 1) * VT])
        out = part if out is None else out + part
    return out


def _causal_bias(j, q0, tk, tq):
    kpos = j * tk + lax.broadcasted_iota(I32, (tk, tq), 0)
    qpos = q0 + lax.broadcasted_iota(I32, (tk, tq), 1)
    return jnp.where(kpos <= qpos, 0.0, NEG)


def _forget_cumsum_body(f_ref, b_ref, g_ref, *, rows_per_head):
    x = f_ref[...] + b_ref[...]
    x = jnp.minimum(x, 0.0) - jnp.log1p(jnp.exp(-jnp.abs(x)))
    lane = lax.broadcasted_iota(I32, x.shape, 1)
    s = 1
    while s < LANE:
        x = x + jnp.where(lane >= s, pltpu.roll(x, s, axis=1), 0.0)
        s *= 2
    tot = jnp.broadcast_to(x[:, LANE - 1:LANE], x.shape)
    row = lax.broadcasted_iota(I32, x.shape, 0) % rows_per_head
    t = tot
    s = 1
    while s < rows_per_head:
        t = t + jnp.where(row >= s, pltpu.roll(t, s, axis=0), 0.0)
        s *= 2
    c = x + (t - tot)
    gam = c * (-LOG2E)
    hi = gam.astype(BF16)
    r1 = gam - hi.astype(F32)
    mid = r1.astype(BF16)
    lo = (r1 - mid.astype(F32)).astype(BF16)
    g_ref[0] = hi
    g_ref[1] = mid
    g_ref[2] = lo


def _forget_decay(f_t, b_f):
    H, S = f_t.shape
    rph = S // LANE
    rows = H * rph
    b_col = jnp.repeat(b_f.astype(F32), rph).reshape(rows, 1)
    g = pl.pallas_call(
        functools.partial(_forget_cumsum_body, rows_per_head=rph),
        out_shape=jax.ShapeDtypeStruct((3, rows, LANE), BF16),
        name="forget_cumsum",
    )(f_t.reshape(rows, LANE), b_col)
    return g.reshape(3, H, S)


def _fox_body(q_ref, k_ref, vt_ref, o_ref, acc_ref, *, tq, tk, ka):
    i = pl.program_id(0)
    q0 = i * tq
    H = B_HEADS
    n_full = q0 // tk
    n_all = (q0 + tq + tk - 1) // tk
    acc_ref[...] = jnp.zeros(acc_ref.shape, F32)

    def tile(j, ms, masked):
        rows = pl.ds(pl.multiple_of(j * tk, tk), tk)
        ts = [_dot(k_ref[rows, h * ka:(h + 1) * ka], q_ref[h * ka:(h + 1) * ka, :]) for h in range(H)]
        bias = _causal_bias(j, q0, tk, tq) if masked else None
        steps = [_softmax_step(ts[h] + bias if masked else ts[h], ms[h]) for h in range(H)]
        for h, (_, alpha, p) in enumerate(steps):
            acc_ref[h] = alpha * acc_ref[h] + _pv(vt_ref, j * (tk // VT), slice(h * HEAD_DIM, (h + 1) * HEAD_DIM), p)
        return tuple(s[0] for s in steps)

    init = tuple(jnp.full((1, tq), M_INIT, F32) for _ in range(H))
    ms = lax.fori_loop(0, n_full, lambda j, c: tile(j, c, False), init)
    lax.fori_loop(n_full, n_all, lambda j, c: tile(j, c, True), ms)
    for h in range(H):
        o_ref[h * HEAD_DIM:(h + 1) * HEAD_DIM, :] = (
            acc_ref[h, :HEAD_DIM, :] / acc_ref[h, HEAD_DIM:HEAD_DIM + 1, :]).astype(o_ref.dtype)


def _fox(q_aug_t, k_aug, v_t, v_blk, *, S):
    tq, tk = min(TQ, S), min(TK, S)
    H = B_HEADS
    ka = k_aug.shape[1] // H
    W = H * HEAD_DIM
    return pl.pallas_call(
        functools.partial(_fox_body, tq=tq, tk=tk, ka=ka),
        grid=(S // tq,),
        in_specs=[pl.BlockSpec((H * ka, tq), lambda i: (0, i)),
                  _resident((S, H * ka), lambda i: (0, 0)),
                  _resident((S // VT, W, VT), lambda i: (0, v_blk, 0))],
        out_specs=pl.BlockSpec((W, tq), lambda i: (0, i)),
        out_shape=jax.ShapeDtypeStruct((W, S), BF16),
        scratch_shapes=[pltpu.VMEM((H, PV_ROWS, tq), F32)],
        compiler_params=_params(("parallel",)),
        name="fox_attn",
    )(q_aug_t, k_aug, v_t)


def _float_key(s):
    bits = pltpu.bitcast(s + 0.0, I32)
    return bits ^ (lax.shift_right_arithmetic(bits, 31) & 0x7FFFFFFF)


def _dsa_body(qi_ref, kidx_ref, misc_ref, q_ref, k_ref, vt_ref, o_ref, keys_ref, acc_ref,
              *, tq, tk, ts, topk, w_row):
    i = pl.program_id(0)
    q0 = i * tq
    H = C_HEADS
    n_tiles = (q0 + tq + tk - 1) // tk
    n_full = q0 // tk
    ns_all = (q0 + tq) // ts
    ns_full = q0 // ts
    w = misc_ref[w_row:w_row + IDX_HEADS, :] * (IDX_DIM ** -0.5 * IDX_HEADS ** -0.5)

    def score_tile(j, masked):
        rows = pl.ds(pl.multiple_of(j * ts, ts), ts)
        kt = kidx_ref[rows, :]
        acc = jnp.zeros((ts, tq), F32)
        for h in range(IDX_HEADS):
            rel = _dot(kt, qi_ref[h])
            acc = acc + jnp.maximum(rel, 0.0) * w[h:h + 1, :]
        key = _float_key(acc)
        if masked:
            kpos = j * ts + lax.broadcasted_iota(I32, (ts, tq), 0)
            qpos = q0 + lax.broadcasted_iota(I32, (ts, tq), 1)
            key = jnp.where(kpos <= qpos, key, NEG_KEY)
        keys_ref[rows, :] = key

    def _s_full(j, c):
        score_tile(j, False)
        return c

    def _s_mask(j, c):
        score_tile(j, True)
        return c

    lax.fori_loop(0, ns_full, _s_full, 0)
    lax.fori_loop(ns_full, ns_all, _s_mask, 0)

    @pl.when(ns_all * ts < n_tiles * tk)
    def _():
        keys_ref[pl.ds(pl.multiple_of(ns_all * ts, ts), ts), :] = jnp.full((ts, tq), NEG_KEY, I32)

    n_acc = 4

    def count_ge(cand):
        def body(c, acc):
            t = keys_ref[pl.ds(pl.multiple_of(c * tk, tk), tk), :]
            ind = jnp.where(t >= cand, 1.0, 0.0)
            part = jnp.sum(ind.reshape(tk // (n_acc * SUBLANE), n_acc * SUBLANE, tq), axis=0)
            return acc + part
        acc = lax.fori_loop(0, n_tiles, body, jnp.zeros((n_acc * SUBLANE, tq), F32))
        return jnp.sum(acc, axis=0, keepdims=True)

    kf = float(topk)
    lo = jnp.where(count_ge(jnp.zeros((1, tq), I32)) >= kf, 0, INT_MIN).astype(I32)

    def bit_step(b, lo):
        cand = lo + lax.shift_left(jnp.int32(1), 30 - b)
        return jnp.where(count_ge(cand) >= kf, cand, lo)

    thr = lax.fori_loop(0, 31, bit_step, lo)

    n_ge = count_ge(thr)
    tied = jnp.where((n_ge > kf) & (thr != NEG_KEY), 1.0, 0.0)
    has_ties = jnp.max(tied) > 0.0

    @pl.when(has_ties)
    def _():
        need = kf - count_ge(thr + 1)
        tri = (lax.broadcasted_iota(I32, (tk, tk), 0) >= lax.broadcasted_iota(I32, (tk, tk), 1)).astype(BF16)

        def demote(j, seen):
            rows = pl.ds(pl.multiple_of(j * tk, tk), tk)
            key = keys_ref[rows, :]
            eq = jnp.where(key == thr, 1.0, 0.0)
            rank = _dot(tri, eq.astype(BF16)) + seen
            keys_ref[rows, :] = jnp.where((eq > 0.5) & (rank > need), INT_MIN, key)
            return seen + jnp.sum(eq, axis=0, keepdims=True)

        lax.fori_loop(0, n_tiles, demote, jnp.zeros((1, tq), F32))

    def to_bias(j, masked):
        rows = pl.ds(pl.multiple_of(j * tk, tk), tk)
        sel = keys_ref[rows, :] >= thr
        if masked:
            kpos = j * tk + lax.broadcasted_iota(I32, (tk, tq), 0)
            qpos = q0 + lax.broadcasted_iota(I32, (tk, tq), 1)
            sel = sel & (kpos <= qpos)
        keys_ref[rows, :] = pltpu.bitcast(jnp.where(sel, 0.0, NEG), I32)

    def _b_full(j, c):
        to_bias(j, False)
        return c

    def _b_mask(j, c):
        to_bias(j, True)
        return c

    lax.fori_loop(0, n_full, _b_full, 0)
    lax.fori_loop(n_full, n_tiles, _b_mask, 0)

    acc_ref[...] = jnp.zeros(acc_ref.shape, F32)
    hd = [slice(h * HEAD_DIM, (h + 1) * HEAD_DIM) for h in range(H)]

    def tile(j, ms):
        rows = pl.ds(pl.multiple_of(j * tk, tk), tk)
        ts = [_dot(k_ref[rows, hd[h]], q_ref[hd[h], :]) for h in range(H)]
        bias = pltpu.bitcast(keys_ref[rows, :], F32)
        steps = [_softmax_step(ts[h] + bias, ms[h]) for h in range(H)]
        for h, (_, alpha, p) in enumerate(steps):
            acc_ref[h] = alpha * acc_ref[h] + _pv(vt_ref, j * (tk // VT), hd[h], p)
        return tuple(s[0] for s in steps)

    lax.fori_loop(0, n_tiles, tile, tuple(jnp.full((1, tq), M_INIT, F32) for _ in range(H)))
    for h in range(H):
        den = jnp.maximum(acc_ref[h, HEAD_DIM:HEAD_DIM + 1, :], TINY)
        o_ref[h * HEAD_DIM:(h + 1) * HEAD_DIM, :] = (acc_ref[h, :HEAD_DIM, :] / den).astype(o_ref.dtype)


def _dsa(qi_t, kidx, misc_t, w_row, q_t, q_blk, k_arr, k_blk, v_t, v_blk, *, S):
    tq, tk = min(TQ, S), min(TK, S)
    ts = min(256, tk)
    topk = min(DSA_TOPK, S // 4)
    W = C_HEADS * HEAD_DIM
    assert S % tk == 0 and tk % ts == 0 and tq % ts == 0
    kern = functools.partial(_dsa_body, tq=tq, tk=tk, ts=ts, topk=topk, w_row=w_row)
    return pl.pallas_call(
        kern,
        grid=(S // tq,),
        in_specs=[pl.BlockSpec((IDX_HEADS, IDX_DIM, tq), lambda i: (0, 0, i)),
                  _resident((S, IDX_DIM), lambda i: (0, 0)),
                  pl.BlockSpec((LANE, tq), lambda i: (0, i)),
                  pl.BlockSpec((W, tq), lambda i: (q_blk, i)),
                  _resident((S, W), lambda i: (0, k_blk)),
                  _resident((S // VT, W, VT), lambda i: (0, v_blk, 0))],
        out_specs=pl.BlockSpec((W, tq), lambda i: (0, i)),
        out_shape=jax.ShapeDtypeStruct((W, S), BF16),
        scratch_shapes=[pltpu.VMEM((S, tq), I32), pltpu.VMEM((C_HEADS, PV_ROWS, tq), F32)],
        compiler_params=_params(("arbitrary",)),
        name="dsa_attn",
    )(qi_t, kidx, misc_t, q_t, k_arr, v_t)


def _compress_body(x_ref, pea_ref, peb_ref, w1a_ref, w1b_ref, w2_ref, o_ref, ot_ref):
    x = x_ref[...].astype(F32)
    a = _dot((x + pea_ref[...]).astype(BF16), w1a_ref[...].astype(BF16))
    b = _dot((x + peb_ref[...]).astype(BF16), w1b_ref[...].astype(BF16))
    n = a.shape[0]
    hid = a + pltpu.roll(b, n - 1, axis=0)
    out = _dot(jax.nn.gelu(hid).astype(BF16), w2_ref[...].astype(BF16))
    o_ref[...] = out.astype(o_ref.dtype)
    ot_ref[...] = out.T.astype(ot_ref.dtype)


def _compress(x2, pe, w1, w2):
    _, S, Dh = x2.shape
    n = S // CMP_STRIDE
    half = CMP_STRIDE * Dh
    xr = x2.reshape(2, n, half)
    pe_a = pe[:, :CMP_STRIDE].reshape(2, 1, half)
    pe_b = pe[:, CMP_STRIDE:].reshape(2, 1, half)
    return pl.pallas_call(
        _compress_body,
        grid=(2,),
        in_specs=[pl.BlockSpec((None, n, half), lambda g: (g, 0, 0)),
                  pl.BlockSpec((None, 1, half), lambda g: (g, 0, 0)),
                  pl.BlockSpec((None, 1, half), lambda g: (g, 0, 0)),
                  pl.BlockSpec((None, half, CMP_HIDDEN), lambda g: (g, 0, 0)),
                  pl.BlockSpec((None, half, CMP_HIDDEN), lambda g: (g, 1, 0)),
                  pl.BlockSpec((None, CMP_HIDDEN, Dh), lambda g: (g, 0, 0))],
        out_specs=[pl.BlockSpec((None, n, Dh), lambda g: (g, 0, 0)),
                   pl.BlockSpec((None, Dh, n), lambda g: (g, 0, 0))],
        out_shape=[jax.ShapeDtypeStruct((2, n, Dh), BF16), jax.ShapeDtypeStruct((2, Dh, n), BF16)],
        compiler_params=_params(("parallel",)),
        name="nsa_compress",
    )(xr, pe_a, pe_b, w1, w1, w2)


def _nsa_body(q_ref, kc_ref, vct_ref, ks_ref, kw_ref, vst_ref, vwt_ref, misc_ref, ov_ref, o_ref, acc_ref, qs_ref,
              *, tq, tk, n_slc, n_sel, ncp, gate_row):
    i = pl.program_id(0)
    q0 = i * tq
    H = D_HEADS
    R = H * tq
    blk_shift = int(math.log2(SLC_BLOCK))
    qs = jnp.concatenate([q_ref[h * HEAD_DIM:(h + 1) * HEAD_DIM, :] for h in range(H)], axis=1)

    def qpos_of(shape):
        return q0 + (lax.broadcasted_iota(I32, shape, 1) & (tq - 1))

    lc = _dot(kc_ref[...], qs)
    cend = lax.broadcasted_iota(I32, (ncp, R), 0) * CMP_STRIDE + (CMP_LEN - 1)
    mc = cend <= qpos_of((ncp, R))
    lc = jnp.where(mc, lc, NEG)
    mx = jnp.max(lc, axis=0, keepdims=True)
    pc = jnp.where(mc, jnp.exp2(lc - mx), 0.0)
    pc = pc * (1.0 / jnp.maximum(jnp.sum(pc, axis=0, keepdims=True), TINY))
    o_cmp = _dot(vct_ref[...], pc.astype(BF16))

    psum = pc[:, 0:tq]
    for h in range(1, H):
        psum = psum + pc[:, h * tq:(h + 1) * tq]
    p_hi = psum.astype(BF16)
    p_lo = (psum - p_hi.astype(F32)).astype(BF16)
    ov = ov_ref[...]
    imp = _dot(ov, p_hi) + _dot(ov, p_lo)
    jblk = lax.broadcasted_iota(I32, (n_slc, tq), 0)
    jblk_f = jblk.astype(F32)
    qpos_l = q0 + lax.broadcasted_iota(I32, (n_slc, tq), 1)
    cur = lax.shift_right_logical(qpos_l, blk_shift)
    forced = (jblk == 0) | (jblk == cur) | (jblk == cur - 1)
    val = jnp.where(forced, FORCE, jnp.where(jblk * SLC_BLOCK <= qpos_l, imp, NEG))
    sel_t = jnp.zeros((n_slc, tq), F32)
    for _ in range(n_sel):
        mxv = jnp.max(val, axis=0, keepdims=True)
        first = jnp.min(jnp.where(val == mxv, jblk_f, float(n_slc)), axis=0, keepdims=True)
        hit = jblk_f == first
        sel_t = jnp.where(hit, 1.0, sel_t)
        val = jnp.where(hit, -jnp.inf, val)
    sel_b = ((sel_t - 1.0) * (-NEG)).astype(BF16)

    bpt = tk // SLC_BLOCK
    n_tiles = (q0 + tq + tk - 1) // tk
    n_full = q0 // tk
    acc_ref[...] = jnp.zeros(acc_ref.shape, F32)
    row_blk = lax.shift_right_logical(lax.broadcasted_iota(I32, (tk, n_slc), 0), blk_shift)
    col_blk = lax.broadcasted_iota(I32, (tk, n_slc), 1)

    n_ch = 2
    cw = R // n_ch
    chains = [slice(c * cw, (c + 1) * cw) for c in range(n_ch)]
    qs_ref[...] = qs

    def slc_tile(j, ms, masked):
        rows = pl.ds(pl.multiple_of(j * tk, tk), tk)
        ks = ks_ref[rows, :]
        ts = [_dot(ks, qs_ref[:, c]) for c in chains]
        expand = jnp.where(row_blk == col_blk - j * bpt, 1.0, 0.0).astype(BF16)
        bias = _dot(expand, sel_b)
        if masked:
            kpos = j * tk + lax.broadcasted_iota(I32, (tk, tq), 0)
            qpos = q0 + lax.broadcasted_iota(I32, (tk, tq), 1)
            bias = jnp.where(kpos <= qpos, bias, NEG)
        bias = jnp.concatenate([bias] * (H // n_ch), axis=1)
        steps = [_softmax_step(ts[c] + bias, ms[c]) for c in range(n_ch)]
        for c, (_, alpha, p) in enumerate(steps):
            acc_ref[:, chains[c]] = alpha * acc_ref[:, chains[c]] + _pv(vst_ref, j * (tk // VT), slice(None), p)
        return tuple(s[0] for s in steps)

    init = tuple(jnp.full((1, cw), M_INIT, F32) for _ in range(n_ch))
    ms = lax.fori_loop(0, n_full, lambda j, c: slc_tile(j, c, False), init)
    lax.fori_loop(n_full, n_tiles, lambda j, c: slc_tile(j, c, True), ms)
    o_slc = acc_ref[:HEAD_DIM, :] * (1.0 / acc_ref[HEAD_DIM:HEAD_DIM + 1, :])

    wlen = WIN + tq
    start = pl.multiple_of(jnp.maximum(q0 - WIN, 0), tq)
    lw = _dot(kw_ref[pl.ds(start, wlen), :], qs)
    dist = qpos_of((wlen, R)) - (start + lax.broadcasted_iota(I32, (wlen, R), 0))
    lw = jnp.where((dist >= 0) & (dist < WIN), lw, NEG)
    mxw = jnp.max(lw, axis=0, keepdims=True)
    pw = jnp.exp2(lw - mxw).astype(BF16)
    o_win = _pv(vwt_ref, start // VT, slice(None), pw)
    o_win = o_win[:HEAD_DIM] * (1.0 / o_win[HEAD_DIM:HEAD_DIM + 1])

    gates = jax.nn.sigmoid(misc_ref[gate_row:gate_row + 3 * H, :])
    for h in range(H):
        c = slice(h * tq, (h + 1) * tq)
        out = (gates[3 * h:3 * h + 1] * o_cmp[:, c] + gates[3 * h + 1:3 * h + 2] * o_slc[:, c]
               + gates[3 * h + 2:3 * h + 3] * o_win[:, c])
        o_ref[h * HEAD_DIM:(h + 1) * HEAD_DIM, :] = out.astype(o_ref.dtype)


def _nsa(q_t, q_blk, kc, vct, k_arr, ks_blk, kw_blk, v_t, vs_blk, vw_blk, misc_t, gate_row, *, S):
    tq, tk = min(TQ_NSA, S), min(TK, S)
    assert S >= WIN + tq and tq & (tq - 1) == 0 and tq == VT and WIN % VT == 0
    n_slc = S // SLC_BLOCK
    n_sel = min(SLC_COUNT, n_slc)
    ncp = S // CMP_STRIDE
    W = D_HEADS * HEAD_DIM
    ci = np.arange(ncp)[None, :]
    sj = np.arange(n_slc)[:, None]
    ov = ((ci * CMP_STRIDE < (sj + 1) * SLC_BLOCK) & (ci * CMP_STRIDE + CMP_LEN > sj * SLC_BLOCK)
          & (ci < ncp - 1))
    ov = jnp.asarray(ov, BF16)
    kern = functools.partial(_nsa_body, tq=tq, tk=tk, n_slc=n_slc, n_sel=n_sel, ncp=ncp, gate_row=gate_row)
    return pl.pallas_call(
        kern,
        grid=(S // tq,),
        in_specs=[pl.BlockSpec((W, tq), lambda i: (q_blk, i)),
                  _resident((ncp, HEAD_DIM), lambda i: (0, 0)),
                  _resident((HEAD_DIM, ncp), lambda i: (0, 0)),
                  _resident((S, HEAD_DIM), lambda i: (0, ks_blk)),
                  _resident((S, HEAD_DIM), lambda i: (0, kw_blk)),
                  _resident((S // VT, HEAD_DIM, VT), lambda i: (0, vs_blk, 0)),
                  _resident((S // VT, HEAD_DIM, VT), lambda i: (0, vw_blk, 0)),
                  pl.BlockSpec((LANE, tq), lambda i: (0, i)),
                  _resident((n_slc, ncp), lambda i: (0, 0))],
        out_specs=pl.BlockSpec((W, tq), lambda i: (0, i)),
        out_shape=jax.ShapeDtypeStruct((W, S), BF16),
        scratch_shapes=[pltpu.VMEM((PV_ROWS, D_HEADS * tq), F32), pltpu.VMEM((HEAD_DIM, D_HEADS * tq), BF16)],
        compiler_params=_params(("parallel",)),
        name="nsa_attn",
    )(q_t, kc, vct, k_arr, k_arr, v_t, v_t, misc_t, ov)


def _rope_tables(S):
    pos = jnp.arange(S, dtype=F32)[:, None]
    half = HEAD_DIM // 2
    ang = pos * (ROPE_THETA ** (-jnp.arange(half, dtype=F32) / half))[None, :]
    cos, sin = jnp.cos(ang), jnp.sin(ang)
    t128 = (jnp.concatenate([cos, cos], 1), jnp.concatenate([-sin, sin], 1))
    h64 = IDX_DIM // 2
    ang = pos * (ROPE_THETA ** (-jnp.arange(h64, dtype=F32) / h64))[None, :]
    cos, sin = jnp.cos(ang), jnp.sin(ang)
    z = jnp.zeros_like(sin)
    t64 = (jnp.concatenate([cos] * 4, 1), jnp.concatenate([-sin, z, -sin, z], 1),
           jnp.concatenate([z, sin, z, sin], 1))
    return t128, t64


def _split_w_in(w_in):
    offs = np.cumsum([0,
                      A_HEADS * HEAD_DIM, A_HEADS * HEAD_DIM, A_HEADS * HEAD_DIM,
                      B_HEADS * HEAD_DIM, B_HEADS * HEAD_DIM, B_HEADS * HEAD_DIM, B_HEADS,
                      C_HEADS * HEAD_DIM, C_HEADS * HEAD_DIM, C_HEADS * HEAD_DIM,
                      IDX_HEADS * IDX_DIM, IDX_DIM, IDX_HEADS,
                      D_HEADS * HEAD_DIM] + [HEAD_DIM] * 6 + [3 * D_HEADS])
    names = ["aq", "ak", "av", "bq", "bk", "bv", "bf", "cq", "ck", "cv", "cqi", "cki", "cwi",
             "dq", "dkc", "dvc", "dks", "dvs", "dkw", "dvw", "dg"]
    col = {n: w_in[:, offs[k]:offs[k + 1]] for k, n in enumerate(names)}
    D = w_in.shape[0]

    def pack(ns, pad=0):
        parts = [col[n] for n in ns]
        if pad:
            parts.append(jnp.zeros((D, pad), w_in.dtype))
        return jnp.concatenate(parts, axis=1).astype(BF16)

    n_misc = B_HEADS + IDX_HEADS + 3 * D_HEADS
    return dict(
        a_qk=pack(["aq", "ak"]),
        a_v=pack(["av"]),
        b_k=pack(["bk", "dvc"]),
        b_q=pack(["bq"]),
        cd_k=pack(["ck", "dkc", "dks", "dkw"]),
        cd_q=pack(["cq", "dq"]),
        idx=pack(["cqi", "cki"], pad=LANE - IDX_DIM),
        v_t=pack(["bv", "cv", "dvs", "dvw"]),
        misc=pack(["bf", "cwi", "dg"], pad=LANE - n_misc),
    )


def _pick(n, cands):
    for c in cands:
        if n % c == 0:
            return c
    return n


def _layer(x, p, stacked, layer, tables):
    S, D = x.shape
    (cos128, sin128), (cos64, sin64a, sin64b) = tables
    tm = _pick(S, (1024, 512, 256))
    h = _rmsnorm(x, p["norm_mix"], BF16)
    w = _split_w_in(p["w_in"])

    def proj(name, mode="plain", extras=(), out_dtype=BF16, layout="rows", scale=None):
        n = w[name].shape[1]
        tn = _pick(n, (1024, 896, 768, 640, 512, 384, 256, 128))
        return _mm(h, w[name], tm=tm, tn=tn, out_dtype=out_dtype, mode=mode, extras=extras, layout=layout,
                   name="in_" + name, scale=scale)

    a_qk = proj("a_qk", "rope128", (cos128, sin128))
    a_v = proj("a_v")
    b_k = proj("b_k")
    b_q_t = proj("b_q", layout="cols", scale=EXP_C)
    cd_k = proj("cd_k", "rope128", (cos128, sin128))
    cd_q_t = proj("cd_q", "rope128", (cos128, sin128), layout="cols", scale=EXP_C)
    idx_t = proj("idx", "rope64", (cos64, sin64a, sin64b), layout="cols")
    v_t = proj("v_t", layout="coltiles")
    misc_t = proj("misc", out_dtype=F32, layout="cols")

    outs, lses = [], []
    for g, (window, dilation) in enumerate(DIL_PATTERNS):
        assert window // dilation == QB
        o, l = _dilated_group(a_qk, a_v, g, S=S, dilation=dilation)
        outs.append(o)
        lses.append(l)
    o_a = _combine_a(outs, lses)

    BW = B_HEADS * HEAD_DIM
    gam = _forget_decay(misc_t[:B_HEADS], p["b_f"])
    pad = HEAD_DIM - 3
    bq_t = b_q_t.reshape(B_HEADS, HEAD_DIM, S)
    bk = b_k[:, :BW].reshape(S, B_HEADS, HEAD_DIM)
    q_aug_t = jnp.concatenate([bq_t, jnp.ones((B_HEADS, 3, S), BF16), jnp.zeros((B_HEADS, pad, S), BF16)], axis=1)
    k_aug = jnp.concatenate([bk, gam.transpose(2, 1, 0), jnp.zeros((S, B_HEADS, pad), BF16)], axis=2)
    o_b_t = _fox(q_aug_t.reshape(-1, S), k_aug.reshape(S, -1), v_t, 0, S=S)

    n_qi = IDX_HEADS * IDX_DIM
    qi_t = idx_t[:n_qi].reshape(IDX_HEADS, IDX_DIM, S)
    kidx = idx_t[n_qi:n_qi + IDX_DIM].T
    o_c_t = _dsa(qi_t, kidx, misc_t, B_HEADS, cd_q_t, 0, cd_k, 0, v_t, 1, S=S)

    cw = C_HEADS * HEAD_DIM
    nb = cw // HEAD_DIM
    kv_c = jnp.stack([cd_k[:, cw:cw + HEAD_DIM], b_k[:, BW:BW + HEAD_DIM]])
    kv_cmp, kv_cmp_t = _compress(kv_c, p["cmp_pe"], p["cmp_w1"], p["cmp_w2"])
    o_d_t = _nsa(cd_q_t, 1, kv_cmp[0], kv_cmp_t[1], cd_k, nb + 1, nb + 2, v_t, 2 * nb, 2 * nb + 1,
                 misc_t, B_HEADS + IDX_HEADS, S=S)

    tn = _pick(D, (512, 256, 128))
    nl = stacked["w_gate"].shape[0]
    wg = stacked["w_gate"].reshape(nl, D, N_BRANCH * D)
    bg = stacked["b_gate"].reshape(nl, N_BRANCH, 1, D)
    merged = _gate_merge(h, wg, bg, (o_a, o_b_t, o_c_t, o_d_t), stacked["w_branch"], layer,
                         tm=tm, tn=_pick(D, (256, 128)))
    x = _mm(merged, stacked["w_out"], tm=tm, tn=tn, out_dtype=F32, mode="resid", extras=(x,), name="out_proj",
            layer=layer)

    h2 = _rmsnorm(x, p["norm_ffn"], BF16)
    Hf = stacked["w_ffn_in"].shape[-1] // 2
    act = _mm_swiglu(h2, stacked["w_ffn_in"], layer, tm=tm, tn=_pick(Hf, (512, 256, 128)))
    x = _mm(act, stacked["w_ffn_out"], tm=_pick(S, (512, 256)), tn=tn, out_dtype=F32, mode="resid",
            extras=(x,), name="ffn_out", layer=layer)
    return x


def kernel(x, norm_mix, w_in, w_gate, b_gate, b_f, cmp_pe, cmp_w1, cmp_w2, w_branch, w_out, norm_ffn,
           w_ffn_in, w_ffn_out, norm_final):
    B, S, D = x.shape
    tables = _rope_tables(S)
    stacked = dict(w_gate=w_gate, b_gate=b_gate, w_branch=w_branch, w_out=w_out, w_ffn_in=w_ffn_in,
                   w_ffn_out=w_ffn_out)
    outs = []
    for b in range(B):
        xb = x[b]
        for l in range(norm_mix.shape[0]):
            p = dict(norm_mix=norm_mix[l], w_in=w_in[l], b_f=b_f[l], cmp_pe=cmp_pe[l], cmp_w1=cmp_w1[l],
                     cmp_w2=cmp_w2[l], norm_ffn=norm_ffn[l])
            xb = _layer(xb, p, stacked, l, tables)
        outs.append(_rmsnorm(xb, norm_final, x.dtype))
    return jnp.stack(outs)
```

```python
import functools
import math

import jax
import jax.numpy as jnp
import numpy as np
from jax import lax
from jax.experimental import pallas as pl
from jax.experimental.pallas import tpu as pltpu

F32 = jnp.float32
BF16 = jnp.bfloat16
I32 = jnp.int32

HEAD_DIM = 128
ROPE_THETA = 10000.0
ATTN_SCALE = HEAD_DIM ** -0.5
QB = 128
NEG = -1e30
TINY = 1e-30
DIL_PATTERNS = ((128, 1), (512, 4), (2048, 16))
A_HEADS_PER_GROUP = 4
A_HEADS = A_HEADS_PER_GROUP * len(DIL_PATTERNS)
B_HEADS = 4
C_HEADS = 4
IDX_HEADS = 16
IDX_DIM = 64
DSA_TOPK = 256
D_HEADS = 4
CMP_LEN = 32
CMP_STRIDE = 16
CMP_HIDDEN = 256
SLC_BLOCK = 64
SLC_COUNT = 16
WIN = 512
FORCE = 1e9
N_BRANCH = 4
BRANCH_WIDTH = 4 * HEAD_DIM

LANE = 128
SUBLANE = 8
VMEM_LIMIT = 56 * 1024 * 1024
VT = 256
TQ = 512
TQ_NSA = 256
TK = 512

LOG2E = math.log2(math.e)
EXP_C = ATTN_SCALE * LOG2E
M_INIT = 0.5 * NEG
DEN_ROWS = 16
PV_ROWS = HEAD_DIM + DEN_ROWS

INT_MIN = -2 ** 31
_NEG_BITS = int(np.array(NEG, np.float32).view(np.int32))
NEG_KEY = _NEG_BITS ^ 0x7FFFFFFF


def _params(sem):
    return pltpu.CompilerParams(dimension_semantics=sem, vmem_limit_bytes=VMEM_LIMIT)


def _resident(shape, index_map):
    return pl.BlockSpec(shape, index_map, pipeline_mode=pl.Buffered(1))


def _dot(a, b):
    return jnp.dot(a, b, preferred_element_type=F32)


def _dot_nt(a, b):
    return lax.dot_general(a, b, (((1,), (1,)), ((), ())), preferred_element_type=F32)


def _dot_tn(a, b):
    return lax.dot_general(a, b, (((0,), (0,)), ((), ())), preferred_element_type=F32)


def _rmsnorm_body(x_ref, g_ref, o_ref):
    x = x_ref[...]
    ms = jnp.mean(x * x, axis=-1, keepdims=True)
    o_ref[...] = (x * lax.rsqrt(ms + 1e-6) * g_ref[...]).astype(o_ref.dtype)


def _rmsnorm(x, g, out_dtype, tm=512):
    S, D = x.shape
    return pl.pallas_call(
        _rmsnorm_body,
        grid=(S // tm,),
        in_specs=[pl.BlockSpec((tm, D), lambda i: (i, 0)),
                  pl.BlockSpec((1, D), lambda i: (0, 0))],
        out_specs=pl.BlockSpec((tm, D), lambda i: (i, 0)),
        out_shape=jax.ShapeDtypeStruct((S, D), out_dtype),
        compiler_params=_params(("parallel",)),
        name="rmsnorm",
    )(x, g.reshape(1, D))


def _rope128(acc, cos, sin):
    parts = []
    for c in range(acc.shape[1] // LANE):
        blk = acc[:, c * LANE:(c + 1) * LANE]
        parts.append(blk * cos + pltpu.roll(blk, LANE // 2, axis=1) * sin)
    return parts[0] if len(parts) == 1 else jnp.concatenate(parts, axis=1)


def _rope64(acc, cos, sin_a, sin_b):
    parts = []
    for c in range(acc.shape[1] // LANE):
        blk = acc[:, c * LANE:(c + 1) * LANE]
        parts.append(blk * cos + pltpu.roll(blk, LANE - IDX_DIM // 2, axis=1) * sin_a
                     + pltpu.roll(blk, IDX_DIM // 2, axis=1) * sin_b)
    return parts[0] if len(parts) == 1 else jnp.concatenate(parts, axis=1)


def _cast_once(src_ref, dst_ref):
    @pl.when(pl.program_id(1) == 0)
    def _():
        dst_ref[...] = src_ref[...].astype(BF16)


def _mm_body(*refs, mode, layout, cast_w, scale):
    a_ref, b_ref = refs[0], refs[1]
    if cast_w:
        o_ref, wb_ref = refs[-2], refs[-1]
        _cast_once(b_ref, wb_ref)
        b_ref = wb_ref
    else:
        o_ref = refs[-1]
    acc = _dot(a_ref[...], b_ref[...])
    if mode == "rope128":
        acc = _rope128(acc, refs[2][...], refs[3][...])
    elif mode == "rope64":
        acc = _rope64(acc, refs[2][...], refs[3][...], refs[4][...])
    elif mode == "resid":
        acc = refs[2][...] + acc
    if scale is not None:
        acc = acc * scale
    if layout == "rows":
        o_ref[...] = acc.astype(o_ref.dtype)
    elif layout == "cols":
        o_ref[...] = acc.T.astype(o_ref.dtype)
    else:
        acc_t = acc.T
        for c in range(acc_t.shape[1] // VT):
            o_ref[c] = acc_t[:, c * VT:(c + 1) * VT].astype(o_ref.dtype)


def _mm(a, b, *, tm, tn, out_dtype, mode="plain", extras=(), layout="rows", name="mm", layer=None, scale=None,
        cols=None):
    M, K = a.shape
    c0, N = (0, b.shape[-1]) if cols is None else cols
    assert M % tm == 0 and N % tn == 0 and c0 % tn == 0, (M, N, tm, tn, c0)
    j0 = c0 // tn
    cast_w = layer is not None
    if cast_w:
        b_spec = pl.BlockSpec((None, K, tn), lambda j, i: (layer, 0, j0 + j))
    else:
        b_spec = pl.BlockSpec((K, tn), lambda j, i: (0, j))
    in_specs = [pl.BlockSpec((tm, K), lambda j, i: (i, 0)), b_spec]
    if mode in ("rope128", "rope64"):
        in_specs += [pl.BlockSpec((tm, LANE), lambda j, i: (i, 0)) for _ in extras]
    elif mode == "resid":
        in_specs += [pl.BlockSpec((tm, tn), lambda j, i: (i, j))]
    if layout == "rows":
        out_spec = pl.BlockSpec((tm, tn), lambda j, i: (i, j))
        out_shape = (M, N)
    elif layout == "cols":
        out_spec = pl.BlockSpec((tn, tm), lambda j, i: (j, i))
        out_shape = (N, M)
    else:
        assert tm % VT == 0
        out_spec = pl.BlockSpec((tm // VT, tn, VT), lambda j, i: (i, j, 0))
        out_shape = (M // VT, N, VT)
    return pl.pallas_call(
        functools.partial(_mm_body, mode=mode, layout=layout, cast_w=cast_w, scale=scale),
        grid=(N // tn, M // tm),
        in_specs=in_specs,
        out_specs=out_spec,
        out_shape=jax.ShapeDtypeStruct(out_shape, out_dtype),
        scratch_shapes=[pltpu.VMEM((K, tn), BF16)] if cast_w else [],
        compiler_params=_params(("parallel", "arbitrary" if cast_w else "parallel")),
        name=name,
    )(a, b, *extras)


def _swiglu_body(a_ref, bg_ref, bu_ref, o_ref, wg_ref, wu_ref):
    _cast_once(bg_ref, wg_ref)
    _cast_once(bu_ref, wu_ref)
    a = a_ref[...]
    g = _dot(a, wg_ref[...])
    u = _dot(a, wu_ref[...])
    o_ref[...] = (g * jax.nn.sigmoid(g) * u).astype(o_ref.dtype)


def _mm_swiglu(a, w, layer, *, tm, tn):
    M, K = a.shape
    H = w.shape[-1] // 2
    nj = H // tn
    assert H % tn == 0 and M % tm == 0
    return pl.pallas_call(
        _swiglu_body,
        grid=(nj, M // tm),
        in_specs=[pl.BlockSpec((tm, K), lambda j, i: (i, 0)),
                  pl.BlockSpec((None, K, tn), lambda j, i: (layer, 0, j)),
                  pl.BlockSpec((None, K, tn), lambda j, i: (layer, 0, j + nj))],
        out_specs=pl.BlockSpec((tm, tn), lambda j, i: (i, j)),
        out_shape=jax.ShapeDtypeStruct((M, H), BF16),
        scratch_shapes=[pltpu.VMEM((K, tn), BF16), pltpu.VMEM((K, tn), BF16)],
        compiler_params=_params(("parallel", "arbitrary")),
        name="ffn_in_swiglu",
    )(a, w, w)


def _gate_merge_body(h_ref, wg0, wg1, wg2, wg3, bg_ref, br0, br1, br2, br3, wb_ref, o_ref, wgs_ref, wbs_ref):
    wgs = (wg0, wg1, wg2, wg3)

    @pl.when(pl.program_id(1) == 0)
    def _():
        for n in range(N_BRANCH):
            wgs_ref[n] = wgs[n][...].astype(BF16)
        wbs_ref[...] = wb_ref[...].astype(BF16)

    h = h_ref[...]
    acc = None
    for n, br in enumerate((br0, br1, br2, br3)):
        gl = _dot(h, wgs_ref[n]) + bg_ref[n]
        y = _dot(br[...], wbs_ref[n]) if n == 0 else _dot_tn(br[...], wbs_ref[n])
        t = jax.nn.sigmoid(gl) * y
        acc = t if acc is None else acc + t
    o_ref[...] = acc.astype(o_ref.dtype)


def _gate_merge(h, wg, bg, branches, wb, layer, *, tm, tn):
    S, D = h.shape
    nj = D // tn
    wg_specs = [pl.BlockSpec((None, D, tn), (lambda j, i, n=n: (layer, 0, n * nj + j))) for n in range(N_BRANCH)]
    br_specs = [pl.BlockSpec((tm, BRANCH_WIDTH), lambda j, i: (i, 0))]
    br_specs += [pl.BlockSpec((BRANCH_WIDTH, tm), lambda j, i: (0, i)) for _ in range(N_BRANCH - 1)]
    return pl.pallas_call(
        _gate_merge_body,
        grid=(nj, S // tm),
        in_specs=[pl.BlockSpec((tm, D), lambda j, i: (i, 0))] + wg_specs
        + [pl.BlockSpec((None, N_BRANCH, 1, tn), lambda j, i: (layer, 0, 0, j))] + br_specs
        + [pl.BlockSpec((None, N_BRANCH, BRANCH_WIDTH, tn), lambda j, i: (layer, 0, 0, j))],
        out_specs=pl.BlockSpec((tm, tn), lambda j, i: (i, j)),
        out_shape=jax.ShapeDtypeStruct((S, D), BF16),
        scratch_shapes=[pltpu.VMEM((N_BRANCH, D, tn), BF16), pltpu.VMEM((N_BRANCH, BRANCH_WIDTH, tn), BF16)],
        compiler_params=_params(("parallel", "arbitrary")),
        name="gate_merge",
    )(h, wg, wg, wg, wg, bg, *branches, wb)


A_CHUNK = QB * max(d for _, d in DIL_PATTERNS)


def _mixer_a_body(*refs, ch):
    ng = len(DIL_PATTERNS)
    ins = [refs[5 * g:5 * g + 5] for g in range(ng)]
    o_ref = refs[5 * ng]
    qf, kf, vf, of, lf = refs[5 * ng + 1:]
    c = pl.program_id(1)
    qi = lax.broadcasted_iota(I32, (QB, 2 * QB), 0)
    kj = lax.broadcasted_iota(I32, (QB, 2 * QB), 1)
    bias_rest = jnp.where((kj >= qi) & (kj <= qi + QB), 0.0, NEG)
    bias_first = jnp.where((kj >= QB) & (kj <= qi + QB), 0.0, NEG)
    bias_n0 = jnp.where(c == 0, bias_first, bias_rest)

    for g, (_, d) in enumerate(DIL_PATTERNS):
        q_ref, k_ref, kp_ref, v_ref, vp_ref = ins[g]
        nb = ch // (QB * d)
        if d > 1:
            qf[...] = q_ref[...].astype(F32)
            kf[0:ch] = kp_ref[...].astype(F32)
            kf[ch:2 * ch] = k_ref[...].astype(F32)
            vf[0:ch] = vp_ref[...].astype(F32)
            vf[ch:2 * ch] = v_ref[...].astype(F32)

        def rows(r, n):
            return pl.ds(r + n * QB * d, QB, stride=d) if d > 1 else pl.ds(n * QB, QB)

        def window(src, prev, cur, r, n):
            if d > 1:
                return src[pl.ds(ch + r + (n - 1) * QB * d, 2 * QB, stride=d), :].astype(BF16)
            if n == 0:
                return jnp.concatenate([prev[ch - QB:ch, :], cur[0:QB, :]], axis=0)
            return cur[(n - 1) * QB:(n + 1) * QB, :]

        blocks = [(r, n) for r in range(d) for n in range(nb)]
        logits = []
        for r, n in blocks:
            qb = qf[rows(r, n), :].astype(BF16) if d > 1 else q_ref[rows(r, n), :]
            s = _dot_nt(qb, window(kf, kp_ref, k_ref, r, n)) * ATTN_SCALE
            logits.append(s + (bias_n0 if n == 0 else bias_rest))
        stats = []
        for s in logits:
            m = jnp.max(s, axis=-1, keepdims=True)
            p = jnp.exp(s - m)
            stats.append((m, p, jnp.sum(p, axis=-1, keepdims=True)))
        for (r, n), (m, p, den) in zip(blocks, stats):
            pv = _dot(p.astype(BF16), window(vf, vp_ref, v_ref, r, n))
            of[g, rows(r, n), :] = pv / den
            lf[g, rows(r, n), :] = jnp.broadcast_to(m + jnp.log(den), (QB, HEAD_DIM))

    a = [lf[g] for g in range(ng)]
    mx = functools.reduce(jnp.maximum, a)
    e = [jnp.exp(x - mx) for x in a]
    num = sum(e[g] * of[g] for g in range(ng))
    o_ref[...] = (num / sum(e)).astype(o_ref.dtype)


def _mixer_a(qk_arr, v_arr, *, S):
    G = A_HEADS_PER_GROUP
    ch = min(A_CHUNK, S)
    assert S % ch == 0 and all(w // d == QB and ch % (QB * d) == 0 for w, d in DIL_PATTERNS)
    in_specs, args = [], []
    for g in range(len(DIL_PATTERNS)):
        qc, kc, vc = g * G, A_HEADS + g * G, g * G
        in_specs += [pl.BlockSpec((ch, HEAD_DIM), lambda h, c, o=qc: (c, o + h)),
                     pl.BlockSpec((ch, HEAD_DIM), lambda h, c, o=kc: (c, o + h)),
                     pl.BlockSpec((ch, HEAD_DIM), lambda h, c, o=kc: (jnp.maximum(c - 1, 0), o + h)),
                     pl.BlockSpec((ch, HEAD_DIM), lambda h, c, o=vc: (c, o + h)),
                     pl.BlockSpec((ch, HEAD_DIM), lambda h, c, o=vc: (jnp.maximum(c - 1, 0), o + h))]
        args += [qk_arr, qk_arr, qk_arr, v_arr, v_arr]
    ng = len(DIL_PATTERNS)
    return pl.pallas_call(
        functools.partial(_mixer_a_body, ch=ch),
        grid=(G, S // ch),
        in_specs=in_specs,
        out_specs=pl.BlockSpec((ch, HEAD_DIM), lambda h, c: (c, h)),
        out_shape=jax.ShapeDtypeStruct((S, G * HEAD_DIM), BF16),
        scratch_shapes=[pltpu.VMEM((ch, HEAD_DIM), F32), pltpu.VMEM((2 * ch, HEAD_DIM), F32),
                        pltpu.VMEM((2 * ch, HEAD_DIM), F32), pltpu.VMEM((ng, ch, HEAD_DIM), F32),
                        pltpu.VMEM((ng, ch, HEAD_DIM), F32)],
        compiler_params=_params(("parallel", "parallel")),
        name="dilated_attn",
    )(*args)


def _softmax_step(t, m):
    m_new = jnp.maximum(m, jnp.max(t, axis=0, keepdims=True))
    return m_new, jnp.exp2(m - m_new), jnp.exp2(t - m_new).astype(BF16)


def _pv(vt_ref, tile0, rows, p):
    ones = jnp.ones((DEN_ROWS, VT), BF16)
    out = None
    for c in range(p.shape[0] // VT):
        v_aug = jnp.concatenate([vt_ref[tile0 + c, rows, :], ones], axis=0)
        part = _dot(v_aug, p[c * VT:(c + 1) * VT])
        out = part if out is None else out + part
    return out


def _causal_bias(j, q0, tk, tq):
    kpos = j * tk + lax.broadcasted_iota(I32, (tk, tq), 0)
    qpos = q0 + lax.broadcasted_iota(I32, (tk, tq), 1)
    return jnp.where(kpos <= qpos, 0.0, NEG)


def _forget_cumsum_body(f_ref, b_ref, g_ref, *, rows_per_head):
    x = f_ref[...] + b_ref[...]
    x = jnp.minimum(x, 0.0) - jnp.log1p(jnp.exp(-jnp.abs(x)))
    lane = lax.broadcasted_iota(I32, x.shape, 1)
    s = 1
    while s < LANE:
        x = x + jnp.where(lane >= s, pltpu.roll(x, s, axis=1), 0.0)
        s *= 2
    tot = jnp.broadcast_to(x[:, LANE - 1:LANE], x.shape)
    row = lax.broadcasted_iota(I32, x.shape, 0) % rows_per_head
    t = tot
    s = 1
    while s < rows_per_head:
        t = t + jnp.where(row >= s, pltpu.roll(t, s, axis=0), 0.0)
        s *= 2
    c = x + (t - tot)
    gam = c * (-LOG2E)
    hi = gam.astype(BF16)
    r1 = gam - hi.astype(F32)
    mid = r1.astype(BF16)
    lo = (r1 - mid.astype(F32)).astype(BF16)
    g_ref[0] = hi
    g_ref[1] = mid
    g_ref[2] = lo


def _forget_decay(f_t, b_f):
    H, S = f_t.shape
    rph = S // LANE
    rows = H * rph
    b_col = jnp.repeat(b_f.astype(F32), rph).reshape(rows, 1)
    g = pl.pallas_call(
        functools.partial(_forget_cumsum_body, rows_per_head=rph),
        out_shape=jax.ShapeDtypeStruct((3, rows, LANE), BF16),
        name="forget_cumsum",
    )(f_t.reshape(rows, LANE), b_col)
    return g.reshape(3, H, S)


def _fox_body(q_ref, k_ref, vt_ref, o_ref, acc_ref, *, tq, tk, ka):
    i = pl.program_id(0)
    q0 = i * tq
    H = B_HEADS
    n_full = q0 // tk
    n_all = (q0 + tq + tk - 1) // tk
    acc_ref[...] = jnp.zeros(acc_ref.shape, F32)

    def tile(j, ms, masked):
        rows = pl.ds(pl.multiple_of(j * tk, tk), tk)
        ts = [_dot(k_ref[rows, h * ka:(h + 1) * ka], q_ref[h * ka:(h + 1) * ka, :]) for h in range(H)]
        bias = _causal_bias(j, q0, tk, tq) if masked else None
        steps = [_softmax_step(ts[h] + bias if masked else ts[h], ms[h]) for h in range(H)]
        for h, (_, alpha, p) in enumerate(steps):
            acc_ref[h] = alpha * acc_ref[h] + _pv(vt_ref, j * (tk // VT), slice(h * HEAD_DIM, (h + 1) * HEAD_DIM), p)
        return tuple(s[0] for s in steps)

    init = tuple(jnp.full((1, tq), M_INIT, F32) for _ in range(H))
    ms = lax.fori_loop(0, n_full, lambda j, c: tile(j, c, False), init)
    lax.fori_loop(n_full, n_all, lambda j, c: tile(j, c, True), ms)
    for h in range(H):
        o_ref[h * HEAD_DIM:(h + 1) * HEAD_DIM, :] = (
            acc_ref[h, :HEAD_DIM, :] / acc_ref[h, HEAD_DIM:HEAD_DIM + 1, :]).astype(o_ref.dtype)


def _fox(q_aug_t, k_aug, v_t, v_blk, *, S):
    tq, tk = min(TQ, S), min(TK, S)
    H = B_HEADS
    ka = k_aug.shape[1] // H
    W = H * HEAD_DIM
    return pl.pallas_call(
        functools.partial(_fox_body, tq=tq, tk=tk, ka=ka),
        grid=(S // tq,),
        in_specs=[pl.BlockSpec((H * ka, tq), lambda i: (0, i)),
                  _resident((S, H * ka), lambda i: (0, 0)),
                  _resident((S // VT, W, VT), lambda i: (0, v_blk, 0))],
        out_specs=pl.BlockSpec((W, tq), lambda i: (0, i)),
        out_shape=jax.ShapeDtypeStruct((W, S), BF16),
        scratch_shapes=[pltpu.VMEM((H, PV_ROWS, tq), F32)],
        compiler_params=_params(("parallel",)),
        name="fox_attn",
    )(q_aug_t, k_aug, v_t)


def _float_key(s):
    bits = pltpu.bitcast(s + 0.0, I32)
    return bits ^ (lax.shift_right_arithmetic(bits, 31) & 0x7FFFFFFF)


def _dsa_body(qi_ref, kidx_ref, misc_ref, q_ref, k_ref, vt_ref, o_ref, keys_ref, acc_ref,
              *, tq, tk, ts, topk, w_row):
    i = pl.program_id(0)
    q0 = i * tq
    H = C_HEADS
    n_tiles = (q0 + tq + tk - 1) // tk
    n_full = q0 // tk
    ns_all = (q0 + tq) // ts
    ns_full = q0 // ts
    w = misc_ref[w_row:w_row + IDX_HEADS, :] * (IDX_DIM ** -0.5 * IDX_HEADS ** -0.5)

    def score_tile(j, masked):
        rows = pl.ds(pl.multiple_of(j * ts, ts), ts)
        kt = kidx_ref[rows, :]
        acc = jnp.zeros((ts, tq), F32)
        for h in range(IDX_HEADS):
            rel = _dot(kt, qi_ref[h])
            acc = acc + jnp.maximum(rel, 0.0) * w[h:h + 1, :]
        key = _float_key(acc)
        if masked:
            kpos = j * ts + lax.broadcasted_iota(I32, (ts, tq), 0)
            qpos = q0 + lax.broadcasted_iota(I32, (ts, tq), 1)
            key = jnp.where(kpos <= qpos, key, NEG_KEY)
        keys_ref[rows, :] = key

    def _s_full(j, c):
        score_tile(j, False)
        return c

    def _s_mask(j, c):
        score_tile(j, True)
        return c

    lax.fori_loop(0, ns_full, _s_full, 0)
    lax.fori_loop(ns_full, ns_all, _s_mask, 0)

    @pl.when(ns_all * ts < n_tiles * tk)
    def _():
        keys_ref[pl.ds(pl.multiple_of(ns_all * ts, ts), ts), :] = jnp.full((ts, tq), NEG_KEY, I32)

    n_acc = 4

    def count_ge(cand):
        def body(c, acc):
            t = keys_ref[pl.ds(pl.multiple_of(c * tk, tk), tk), :]
            ind = jnp.where(t >= cand, 1.0, 0.0)
            part = jnp.sum(ind.reshape(tk // (n_acc * SUBLANE), n_acc * SUBLANE, tq), axis=0)
            return acc + part
        acc = lax.fori_loop(0, n_tiles, body, jnp.zeros((n_acc * SUBLANE, tq), F32))
        return jnp.sum(acc, axis=0, keepdims=True)

    kf = float(topk)
    lo = jnp.where(count_ge(jnp.zeros((1, tq), I32)) >= kf, 0, INT_MIN).astype(I32)

    def bit_step(b, lo):
        cand = lo + lax.shift_left(jnp.int32(1), 30 - b)
        return jnp.where(count_ge(cand) >= kf, cand, lo)

    thr = lax.fori_loop(0, 31, bit_step, lo)

    n_ge = count_ge(thr)
    tied = jnp.where((n_ge > kf) & (thr != NEG_KEY), 1.0, 0.0)
    has_ties = jnp.max(tied) > 0.0

    @pl.when(has_ties)
    def _():
        need = kf - count_ge(thr + 1)
        tri = (lax.broadcasted_iota(I32, (tk, tk), 0) >= lax.broadcasted_iota(I32, (tk, tk), 1)).astype(BF16)

        def demote(j, seen):
            rows = pl.ds(pl.multiple_of(j * tk, tk), tk)
            key = keys_ref[rows, :]
            eq = jnp.where(key == thr, 1.0, 0.0)
            rank = _dot(tri, eq.astype(BF16)) + seen
            keys_ref[rows, :] = jnp.where((eq > 0.5) & (rank > need), INT_MIN, key)
            return seen + jnp.sum(eq, axis=0, keepdims=True)

        lax.fori_loop(0, n_tiles, demote, jnp.zeros((1, tq), F32))

    def to_bias(j, masked):
        rows = pl.ds(pl.multiple_of(j * tk, tk), tk)
        sel = keys_ref[rows, :] >= thr
        if masked:
            kpos = j * tk + lax.broadcasted_iota(I32, (tk, tq), 0)
            qpos = q0 + lax.broadcasted_iota(I32, (tk, tq), 1)
            sel = sel & (kpos <= qpos)
        keys_ref[rows, :] = pltpu.bitcast(jnp.where(sel, 0.0, NEG), I32)

    def _b_full(j, c):
        to_bias(j, False)
        return c

    def _b_mask(j, c):
        to_bias(j, True)
        return c

    lax.fori_loop(0, n_full, _b_full, 0)
    lax.fori_loop(n_full, n_tiles, _b_mask, 0)

    acc_ref[...] = jnp.zeros(acc_ref.shape, F32)
    hd = [slice(h * HEAD_DIM, (h + 1) * HEAD_DIM) for h in range(H)]

    def tile(j, ms):
        rows = pl.ds(pl.multiple_of(j * tk, tk), tk)
        ts = [_dot(k_ref[rows, hd[h]], q_ref[hd[h], :]) for h in range(H)]
        bias = pltpu.bitcast(keys_ref[rows, :], F32)
        steps = [_softmax_step(ts[h] + bias, ms[h]) for h in range(H)]
        for h, (_, alpha, p) in enumerate(steps):
            acc_ref[h] = alpha * acc_ref[h] + _pv(vt_ref, j * (tk // VT), hd[h], p)
        return tuple(s[0] for s in steps)

    lax.fori_loop(0, n_tiles, tile, tuple(jnp.full((1, tq), M_INIT, F32) for _ in range(H)))
    for h in range(H):
        den = jnp.maximum(acc_ref[h, HEAD_DIM:HEAD_DIM + 1, :], TINY)
        o_ref[h * HEAD_DIM:(h + 1) * HEAD_DIM, :] = (acc_ref[h, :HEAD_DIM, :] / den).astype(o_ref.dtype)


def _dsa(qi_t, kidx, misc_t, w_row, q_t, q_blk, k_arr, k_blk, v_t, v_blk, *, S):
    tq, tk = min(TQ, S), min(TK, S)
    ts = min(256, tk)
    topk = min(DSA_TOPK, S // 4)
    W = C_HEADS * HEAD_DIM
    assert S % tk == 0 and tk % ts == 0 and tq % ts == 0
    kern = functools.partial(_dsa_body, tq=tq, tk=tk, ts=ts, topk=topk, w_row=w_row)
    return pl.pallas_call(
        kern,
        grid=(S // tq,),
        in_specs=[pl.BlockSpec((IDX_HEADS, IDX_DIM, tq), lambda i: (0, 0, i)),
                  _resident((S, IDX_DIM), lambda i: (0, 0)),
                  pl.BlockSpec((LANE, tq), lambda i: (0, i)),
                  pl.BlockSpec((W, tq), lambda i: (q_blk, i)),
                  _resident((S, W), lambda i: (0, k_blk)),
                  _resident((S // VT, W, VT), lambda i: (0, v_blk, 0))],
        out_specs=pl.BlockSpec((W, tq), lambda i: (0, i)),
        out_shape=jax.ShapeDtypeStruct((W, S), BF16),
        scratch_shapes=[pltpu.VMEM((S, tq), I32), pltpu.VMEM((C_HEADS, PV_ROWS, tq), F32)],
        compiler_params=_params(("arbitrary",)),
        name="dsa_attn",
    )(qi_t, kidx, misc_t, q_t, k_arr, v_t)


def _compress_body(x_ref, pea_ref, peb_ref, w1a_ref, w1b_ref, w2_ref, o_ref, ot_ref):
    x = x_ref[...].astype(F32)
    a = _dot((x + pea_ref[...]).astype(BF16), w1a_ref[...].astype(BF16))
    b = _dot((x + peb_ref[...]).astype(BF16), w1b_ref[...].astype(BF16))
    n = a.shape[0]
    hid = a + pltpu.roll(b, n - 1, axis=0)
    out = _dot(jax.nn.gelu(hid).astype(BF16), w2_ref[...].astype(BF16))
    o_ref[...] = out.astype(o_ref.dtype)
    ot_ref[...] = out.T.astype(ot_ref.dtype)


def _compress(x2, pe, w1, w2):
    _, S, Dh = x2.shape
    n = S // CMP_STRIDE
    half = CMP_STRIDE * Dh
    xr = x2.reshape(2, n, half)
    pe_a = pe[:, :CMP_STRIDE].reshape(2, 1, half)
    pe_b = pe[:, CMP_STRIDE:].reshape(2, 1, half)
    return pl.pallas_call(
        _compress_body,
        grid=(2,),
        in_specs=[pl.BlockSpec((None, n, half), lambda g: (g, 0, 0)),
                  pl.BlockSpec((None, 1, half), lambda g: (g, 0, 0)),
                  pl.BlockSpec((None, 1, half), lambda g: (g, 0, 0)),
                  pl.BlockSpec((None, half, CMP_HIDDEN), lambda g: (g, 0, 0)),
                  pl.BlockSpec((None, half, CMP_HIDDEN), lambda g: (g, 1, 0)),
                  pl.BlockSpec((None, CMP_HIDDEN, Dh), lambda g: (g, 0, 0))],
        out_specs=[pl.BlockSpec((None, n, Dh), lambda g: (g, 0, 0)),
                   pl.BlockSpec((None, Dh, n), lambda g: (g, 0, 0))],
        out_shape=[jax.ShapeDtypeStruct((2, n, Dh), BF16), jax.ShapeDtypeStruct((2, Dh, n), BF16)],
        compiler_params=_params(("parallel",)),
        name="nsa_compress",
    )(xr, pe_a, pe_b, w1, w1, w2)


def _nsa_body(q_ref, kc_ref, vct_ref, ks_ref, kw_ref, vst_ref, vwt_ref, misc_ref, ov_ref, o_ref, acc_ref, qs_ref,
              *, tq, tk, n_slc, n_sel, ncp, gate_row):
    i = pl.program_id(0)
    q0 = i * tq
    H = D_HEADS
    R = H * tq
    blk_shift = int(math.log2(SLC_BLOCK))
    qs = jnp.concatenate([q_ref[h * HEAD_DIM:(h + 1) * HEAD_DIM, :] for h in range(H)], axis=1)

    def qpos_of(shape):
        return q0 + (lax.broadcasted_iota(I32, shape, 1) & (tq - 1))

    lc = _dot(kc_ref[...], qs)
    cend = lax.broadcasted_iota(I32, (ncp, R), 0) * CMP_STRIDE + (CMP_LEN - 1)
    mc = cend <= qpos_of((ncp, R))
    lc = jnp.where(mc, lc, NEG)
    mx = jnp.max(lc, axis=0, keepdims=True)
    pc = jnp.where(mc, jnp.exp2(lc - mx), 0.0)
    pc = pc * (1.0 / jnp.maximum(jnp.sum(pc, axis=0, keepdims=True), TINY))
    o_cmp = _dot(vct_ref[...], pc.astype(BF16))

    psum = pc[:, 0:tq]
    for h in range(1, H):
        psum = psum + pc[:, h * tq:(h + 1) * tq]
    p_hi = psum.astype(BF16)
    p_lo = (psum - p_hi.astype(F32)).astype(BF16)
    ov = ov_ref[...]
    imp = _dot(ov, p_hi) + _dot(ov, p_lo)
    jblk = lax.broadcasted_iota(I32, (n_slc, tq), 0)
    jblk_f = jblk.astype(F32)
    qpos_l = q0 + lax.broadcasted_iota(I32, (n_slc, tq), 1)
    cur = lax.shift_right_logical(qpos_l, blk_shift)
    forced = (jblk == 0) | (jblk == cur) | (jblk == cur - 1)
    val = jnp.where(forced, FORCE, jnp.where(jblk * SLC_BLOCK <= qpos_l, imp, NEG))
    sel_t = jnp.zeros((n_slc, tq), F32)
    for _ in range(n_sel):
        mxv = jnp.max(val, axis=0, keepdims=True)
        first = jnp.min(jnp.where(val == mxv, jblk_f, float(n_slc)), axis=0, keepdims=True)
        hit = jblk_f == first
        sel_t = jnp.where(hit, 1.0, sel_t)
        val = jnp.where(hit, -jnp.inf, val)
    sel_b = ((sel_t - 1.0) * (-NEG)).astype(BF16)

    bpt = tk // SLC_BLOCK
    n_tiles = (q0 + tq + tk - 1) // tk
    n_full = q0 // tk
    acc_ref[...] = jnp.zeros(acc_ref.shape, F32)
    row_blk = lax.shift_right_logical(lax.broadcasted_iota(I32, (tk, n_slc), 0), blk_shift)
    col_blk = lax.broadcasted_iota(I32, (tk, n_slc), 1)

    n_ch = 2
    cw = R // n_ch
    chains = [slice(c * cw, (c + 1) * cw) for c in range(n_ch)]
    qs_ref[...] = qs

    def slc_tile(j, ms, masked):
        rows = pl.ds(pl.multiple_of(j * tk, tk), tk)
        ks = ks_ref[rows, :]
        ts = [_dot(ks, qs_ref[:, c]) for c in chains]
        expand = jnp.where(row_blk == col_blk - j * bpt, 1.0, 0.0).astype(BF16)
        bias = _dot(expand, sel_b)
        if masked:
            kpos = j * tk + lax.broadcasted_iota(I32, (tk, tq), 0)
            qpos = q0 + lax.broadcasted_iota(I32, (tk, tq), 1)
            bias = jnp.where(kpos <= qpos, bias, NEG)
        bias = jnp.concatenate([bias] * (H // n_ch), axis=1)
        steps = [_softmax_step(ts[c] + bias, ms[c]) for c in range(n_ch)]
        for c, (_, alpha, p) in enumerate(steps):
            acc_ref[:, chains[c]] = alpha * acc_ref[:, chains[c]] + _pv(vst_ref, j * (tk // VT), slice(None), p)
        return tuple(s[0] for s in steps)

    init = tuple(jnp.full((1, cw), M_INIT, F32) for _ in range(n_ch))
    ms = lax.fori_loop(0, n_full, lambda j, c: slc_tile(j, c, False), init)
    lax.fori_loop(n_full, n_tiles, lambda j, c: slc_tile(j, c, True), ms)
    o_slc = acc_ref[:HEAD_DIM, :] * (1.0 / acc_ref[HEAD_DIM:HEAD_DIM + 1, :])

    wlen = WIN + tq
    start = pl.multiple_of(jnp.maximum(q0 - WIN, 0), tq)
    lw = _dot(kw_ref[pl.ds(start, wlen), :], qs)
    dist = qpos_of((wlen, R)) - (start + lax.broadcasted_iota(I32, (wlen, R), 0))
    lw = jnp.where((dist >= 0) & (dist < WIN), lw, NEG)
    mxw = jnp.max(lw, axis=0, keepdims=True)
    pw = jnp.exp2(lw - mxw).astype(BF16)
    o_win = _pv(vwt_ref, start // VT, slice(None), pw)
    o_win = o_win[:HEAD_DIM] * (1.0 / o_win[HEAD_DIM:HEAD_DIM + 1])

    gates = jax.nn.sigmoid(misc_ref[gate_row:gate_row + 3 * H, :])
    for h in range(H):
        c = slice(h * tq, (h + 1) * tq)
        out = (gates[3 * h:3 * h + 1] * o_cmp[:, c] + gates[3 * h + 1:3 * h + 2] * o_slc[:, c]
               + gates[3 * h + 2:3 * h + 3] * o_win[:, c])
        o_ref[h * HEAD_DIM:(h + 1) * HEAD_DIM, :] = out.astype(o_ref.dtype)


def _nsa(q_t, q_blk, kc, vct, k_arr, ks_blk, kw_blk, v_t, vs_blk, vw_blk, misc_t, gate_row, *, S):
    tq, tk = min(TQ_NSA, S), min(TK, S)
    assert S >= WIN + tq and tq & (tq - 1) == 0 and tq == VT and WIN % VT == 0
    n_slc = S // SLC_BLOCK
    n_sel = min(SLC_COUNT, n_slc)
    ncp = S // CMP_STRIDE
    W = D_HEADS * HEAD_DIM
    ci = np.arange(ncp)[None, :]
    sj = np.arange(n_slc)[:, None]
    ov = ((ci * CMP_STRIDE < (sj + 1) * SLC_BLOCK) & (ci * CMP_STRIDE + CMP_LEN > sj * SLC_BLOCK)
          & (ci < ncp - 1))
    ov = jnp.asarray(ov, BF16)
    kern = functools.partial(_nsa_body, tq=tq, tk=tk, n_slc=n_slc, n_sel=n_sel, ncp=ncp, gate_row=gate_row)
    return pl.pallas_call(
        kern,
        grid=(S // tq,),
        in_specs=[pl.BlockSpec((W, tq), lambda i: (q_blk, i)),
                  _resident((ncp, HEAD_DIM), lambda i: (0, 0)),
                  _resident((HEAD_DIM, ncp), lambda i: (0, 0)),
                  _resident((S, HEAD_DIM), lambda i: (0, ks_blk)),
                  _resident((S, HEAD_DIM), lambda i: (0, kw_blk)),
                  _resident((S // VT, HEAD_DIM, VT), lambda i: (0, vs_blk, 0)),
                  _resident((S // VT, HEAD_DIM, VT), lambda i: (0, vw_blk, 0)),
                  pl.BlockSpec((LANE, tq), lambda i: (0, i)),
                  _resident((n_slc, ncp), lambda i: (0, 0))],
        out_specs=pl.BlockSpec((W, tq), lambda i: (0, i)),
        out_shape=jax.ShapeDtypeStruct((W, S), BF16),
        scratch_shapes=[pltpu.VMEM((PV_ROWS, D_HEADS * tq), F32), pltpu.VMEM((HEAD_DIM, D_HEADS * tq), BF16)],
        compiler_params=_params(("parallel",)),
        name="nsa_attn",
    )(q_t, kc, vct, k_arr, k_arr, v_t, v_t, misc_t, ov)


def _rope_tables(S):
    pos = jnp.arange(S, dtype=F32)[:, None]
    half = HEAD_DIM // 2
    ang = pos * (ROPE_THETA ** (-jnp.arange(half, dtype=F32) / half))[None, :]
    cos, sin = jnp.cos(ang), jnp.sin(ang)
    t128 = (jnp.concatenate([cos, cos], 1), jnp.concatenate([-sin, sin], 1))
    h64 = IDX_DIM // 2
    ang = pos * (ROPE_THETA ** (-jnp.arange(h64, dtype=F32) / h64))[None, :]
    cos, sin = jnp.cos(ang), jnp.sin(ang)
    z = jnp.zeros_like(sin)
    t64 = (jnp.concatenate([cos] * 4, 1), jnp.concatenate([-sin, z, -sin, z], 1),
           jnp.concatenate([z, sin, z, sin], 1))
    return t128, t64


def _split_w_in(w_in):
    offs = np.cumsum([0,
                      A_HEADS * HEAD_DIM, A_HEADS * HEAD_DIM, A_HEADS * HEAD_DIM,
                      B_HEADS * HEAD_DIM, B_HEADS * HEAD_DIM, B_HEADS * HEAD_DIM, B_HEADS,
                      C_HEADS * HEAD_DIM, C_HEADS * HEAD_DIM, C_HEADS * HEAD_DIM,
                      IDX_HEADS * IDX_DIM, IDX_DIM, IDX_HEADS,
                      D_HEADS * HEAD_DIM] + [HEAD_DIM] * 6 + [3 * D_HEADS])
    names = ["aq", "ak", "av", "bq", "bk", "bv", "bf", "cq", "ck", "cv", "cqi", "cki", "cwi",
             "dq", "dkc", "dvc", "dks", "dvs", "dkw", "dvw", "dg"]
    col = {n: w_in[:, offs[k]:offs[k + 1]] for k, n in enumerate(names)}
    D = w_in.shape[0]

    def pack(ns, pad=0):
        parts = [col[n] for n in ns]
        if pad:
            parts.append(jnp.zeros((D, pad), w_in.dtype))
        return jnp.concatenate(parts, axis=1).astype(BF16)

    n_misc = B_HEADS + IDX_HEADS + 3 * D_HEADS
    direct = dict(
        a_qk=(int(offs[0]), int(offs[2] - offs[0])),
        a_v=(int(offs[2]), int(offs[3] - offs[2])),
        b_q=(int(offs[3]), int(offs[4] - offs[3])),
    )
    return direct, dict(
        b_k=pack(["bk", "dvc"]),
        cd_k=pack(["ck", "dkc", "dks", "dkw"]),
        cd_q=pack(["cq", "dq"]),
        idx=pack(["cqi", "cki"], pad=LANE - IDX_DIM),
        v_t=pack(["bv", "cv", "dvs", "dvw"]),
        misc=pack(["bf", "cwi", "dg"], pad=LANE - n_misc),
    )


def _pick(n, cands):
    for c in cands:
        if n % c == 0:
            return c
    return n


def _layer(x, p, stacked, layer, tables):
    S, D = x.shape
    (cos128, sin128), (cos64, sin64a, sin64b) = tables
    tm = _pick(S, (1024, 512, 256))
    h = _rmsnorm(x, p["norm_mix"], BF16)
    direct, w = _split_w_in(p["w_in"])

    def proj(name, mode="plain", extras=(), out_dtype=BF16, layout="rows", scale=None):
        n = direct[name][1] if name in direct else w[name].shape[1]
        tn = _pick(n, (1280, 1152, 1024, 896, 768, 640, 512, 384, 256, 128))
        kw = dict(tm=tm, tn=tn, out_dtype=out_dtype, mode=mode, extras=extras, layout=layout, name="in_" + name,
                  scale=scale)
        if name in direct and direct[name][0] % tn == 0:
            return _mm(h, stacked["w_in"], layer=layer, cols=direct[name], **kw)
        if name in direct:
            c0, n = direct[name]
            return _mm(h, p["w_in"][:, c0:c0 + n].astype(BF16), **kw)
        return _mm(h, w[name], **kw)

    a_qk = proj("a_qk", "rope128", (cos128, sin128))
    a_v = proj("a_v")
    b_k = proj("b_k")
    b_q_t = proj("b_q", layout="cols", scale=EXP_C)
    cd_k = proj("cd_k", "rope128", (cos128, sin128))
    cd_q_t = proj("cd_q", "rope128", (cos128, sin128), layout="cols", scale=EXP_C)
    idx_t = proj("idx", "rope64", (cos64, sin64a, sin64b), layout="cols")
    v_t = proj("v_t", layout="coltiles")
    misc_t = proj("misc", out_dtype=F32, layout="cols")

    o_a = _mixer_a(a_qk, a_v, S=S)

    BW = B_HEADS * HEAD_DIM
    gam = _forget_decay(misc_t[:B_HEADS], p["b_f"])
    pad = HEAD_DIM - 3
    bq_t = b_q_t.reshape(B_HEADS, HEAD_DIM, S)
    bk = b_k[:, :BW].reshape(S, B_HEADS, HEAD_DIM)
    q_aug_t = jnp.concatenate([bq_t, jnp.ones((B_HEADS, 3, S), BF16), jnp.zeros((B_HEADS, pad, S), BF16)], axis=1)
    k_aug = jnp.concatenate([bk, gam.transpose(2, 1, 0), jnp.zeros((S, B_HEADS, pad), BF16)], axis=2)
    o_b_t = _fox(q_aug_t.reshape(-1, S), k_aug.reshape(S, -1), v_t, 0, S=S)

    n_qi = IDX_HEADS * IDX_DIM
    qi_t = idx_t[:n_qi].reshape(IDX_HEADS, IDX_DIM, S)
    kidx = idx_t[n_qi:n_qi + IDX_DIM].T
    o_c_t = _dsa(qi_t, kidx, misc_t, B_HEADS, cd_q_t, 0, cd_k, 0, v_t, 1, S=S)

    cw = C_HEADS * HEAD_DIM
    nb = cw // HEAD_DIM
    kv_c = jnp.stack([cd_k[:, cw:cw + HEAD_DIM], b_k[:, BW:BW + HEAD_DIM]])
    kv_cmp, kv_cmp_t = _compress(kv_c, p["cmp_pe"], p["cmp_w1"], p["cmp_w2"])
    o_d_t = _nsa(cd_q_t, 1, kv_cmp[0], kv_cmp_t[1], cd_k, nb + 1, nb + 2, v_t, 2 * nb, 2 * nb + 1,
                 misc_t, B_HEADS + IDX_HEADS, S=S)

    tn = _pick(D, (512, 256, 128))
    nl = stacked["w_gate"].shape[0]
    wg = stacked["w_gate"].reshape(nl, D, N_BRANCH * D)
    bg = stacked["b_gate"].reshape(nl, N_BRANCH, 1, D)
    merged = _gate_merge(h, wg, bg, (o_a, o_b_t, o_c_t, o_d_t), stacked["w_branch"], layer,
                         tm=tm, tn=_pick(D, (256, 128)))
    x = _mm(merged, stacked["w_out"], tm=tm, tn=tn, out_dtype=F32, mode="resid", extras=(x,), name="out_proj",
            layer=layer)

    h2 = _rmsnorm(x, p["norm_ffn"], BF16)
    Hf = stacked["w_ffn_in"].shape[-1] // 2
    act = _mm_swiglu(h2, stacked["w_ffn_in"], layer, tm=tm, tn=_pick(Hf, (512, 256, 128)))
    x = _mm(act, stacked["w_ffn_out"], tm=_pick(S, (512, 256)), tn=tn, out_dtype=F32, mode="resid",
            extras=(x,), name="ffn_out", layer=layer)
    return x


def kernel(x, norm_mix, w_in, w_gate, b_gate, b_f, cmp_pe, cmp_w1, cmp_w2, w_branch, w_out, norm_ffn,
           w_ffn_in, w_ffn_out, norm_final):
    B, S, D = x.shape
    tables = _rope_tables(S)
    stacked = dict(w_in=w_in, w_gate=w_gate, b_gate=b_gate, w_branch=w_branch, w_out=w_out, w_ffn_in=w_ffn_in,
                   w_ffn_out=w_ffn_out)
    outs = []
    for b in range(B):
        xb = x[b]
        for l in range(norm_mix.shape[0]):
            p = dict(norm_mix=norm_mix[l], w_in=w_in[l], b_f=b_f[l], cmp_pe=cmp_pe[l], cmp_w1=cmp_w1[l],
                     cmp_w2=cmp_w2[l], norm_ffn=norm_ffn[l])
            xb = _layer(xb, p, stacked, l, tables)
        outs.append(_rmsnorm(xb, norm_final, x.dtype))
    return jnp.stack(outs)
```

```python
import functools
import math

import jax
import jax.numpy as jnp
import numpy as np
from jax import lax
from jax.experimental import pallas as pl
from jax.experimental.pallas import tpu as pltpu

F32 = jnp.float32
BF16 = jnp.bfloat16
I32 = jnp.int32

HEAD_DIM = 128
ROPE_THETA = 10000.0
ATTN_SCALE = HEAD_DIM ** -0.5
QB = 128
NEG = -1e30
TINY = 1e-30
DIL_PATTERNS = ((128, 1), (512, 4), (2048, 16))
A_HEADS_PER_GROUP = 4
A_HEADS = A_HEADS_PER_GROUP * len(DIL_PATTERNS)
B_HEADS = 4
C_HEADS = 4
IDX_HEADS = 16
IDX_DIM = 64
DSA_TOPK = 256
D_HEADS = 4
CMP_LEN = 32
CMP_STRIDE = 16
CMP_HIDDEN = 256
SLC_BLOCK = 64
SLC_COUNT = 16
WIN = 512
FORCE = 1e9
N_BRANCH = 4
BRANCH_WIDTH = 4 * HEAD_DIM

LANE = 128
SUBLANE = 8
VMEM_LIMIT = 56 * 1024 * 1024
VT = 256
TQ = 512
TQ_NSA = 256
TK = 512

LOG2E = math.log2(math.e)
EXP_C = ATTN_SCALE * LOG2E
M_INIT = 0.5 * NEG
DEN_ROWS = 16
PV_ROWS = HEAD_DIM + DEN_ROWS

INT_MIN = -2 ** 31
HI16_MASK = -(2 ** 16)
F32_MIN_NORMAL = 2.0 ** -126
BF16_MIN_NORMAL_BITS = 0x0080
_NEG_BITS = int(np.array(NEG, np.float32).view(np.int32))
NEG_KEY = _NEG_BITS ^ 0x7FFFFFFF
NEG_HI = float(np.array(_NEG_BITS & 0xFFFF0000, np.uint32).view(np.float32))


def _params(sem):
    return pltpu.CompilerParams(dimension_semantics=sem, vmem_limit_bytes=VMEM_LIMIT)


def _resident(shape, index_map):
    return pl.BlockSpec(shape, index_map, pipeline_mode=pl.Buffered(1))


def _dot(a, b):
    return jnp.dot(a, b, preferred_element_type=F32)


def _dot_nt(a, b):
    return lax.dot_general(a, b, (((1,), (1,)), ((), ())), preferred_element_type=F32)


def _dot_tn(a, b):
    return lax.dot_general(a, b, (((0,), (0,)), ((), ())), preferred_element_type=F32)


def _rmsnorm_body(x_ref, g_ref, o_ref):
    x = x_ref[...]
    ms = jnp.mean(x * x, axis=-1, keepdims=True)
    o_ref[...] = (x * lax.rsqrt(ms + 1e-6) * g_ref[...]).astype(o_ref.dtype)


def _rmsnorm(x, g, out_dtype, tm=512):
    S, D = x.shape
    return pl.pallas_call(
        _rmsnorm_body,
        grid=(S // tm,),
        in_specs=[pl.BlockSpec((tm, D), lambda i: (i, 0)),
                  pl.BlockSpec((1, D), lambda i: (0, 0))],
        out_specs=pl.BlockSpec((tm, D), lambda i: (i, 0)),
        out_shape=jax.ShapeDtypeStruct((S, D), out_dtype),
        compiler_params=_params(("parallel",)),
        name="rmsnorm",
    )(x, g.reshape(1, D))


def _rope128(acc, cos, sin):
    parts = []
    for c in range(acc.shape[1] // LANE):
        blk = acc[:, c * LANE:(c + 1) * LANE]
        parts.append(blk * cos + pltpu.roll(blk, LANE // 2, axis=1) * sin)
    return parts[0] if len(parts) == 1 else jnp.concatenate(parts, axis=1)


def _rope64(acc, cos, sin_a, sin_b):
    parts = []
    for c in range(acc.shape[1] // LANE):
        blk = acc[:, c * LANE:(c + 1) * LANE]
        parts.append(blk * cos + pltpu.roll(blk, LANE - IDX_DIM // 2, axis=1) * sin_a
                     + pltpu.roll(blk, IDX_DIM // 2, axis=1) * sin_b)
    return parts[0] if len(parts) == 1 else jnp.concatenate(parts, axis=1)


def _cast_once(src_ref, dst_ref):
    @pl.when(pl.program_id(1) == 0)
    def _():
        dst_ref[...] = src_ref[...].astype(BF16)


def _mm_body(*refs, mode, layout, cast_w, scale):
    a_ref, b_ref = refs[0], refs[1]
    if cast_w:
        o_ref, wb_ref = refs[-2], refs[-1]
        _cast_once(b_ref, wb_ref)
        b_ref = wb_ref
    else:
        o_ref = refs[-1]
    acc = _dot(a_ref[...], b_ref[...])
    if mode == "rope128":
        acc = _rope128(acc, refs[2][...], refs[3][...])
    elif mode == "rope64":
        acc = _rope64(acc, refs[2][...], refs[3][...], refs[4][...])
    elif mode == "resid":
        acc = refs[2][...] + acc
    if scale is not None:
        acc = acc * scale
    if layout == "rows":
        o_ref[...] = acc.astype(o_ref.dtype)
    elif layout == "cols":
        o_ref[...] = acc.T.astype(o_ref.dtype)
    else:
        acc_t = acc.T
        for c in range(acc_t.shape[1] // VT):
            o_ref[c] = acc_t[:, c * VT:(c + 1) * VT].astype(o_ref.dtype)


def _mm(a, b, *, tm, tn, out_dtype, mode="plain", extras=(), layout="rows", name="mm", layer=None, scale=None):
    M, K = a.shape
    N = b.shape[-1]
    assert M % tm == 0 and N % tn == 0, (M, N, tm, tn)
    cast_w = layer is not None
    if cast_w:
        b_spec = pl.BlockSpec((None, K, tn), lambda j, i: (layer, 0, j))
    else:
        b_spec = pl.BlockSpec((K, tn), lambda j, i: (0, j))
    in_specs = [pl.BlockSpec((tm, K), lambda j, i: (i, 0)), b_spec]
    if mode in ("rope128", "rope64"):
        in_specs += [pl.BlockSpec((tm, LANE), lambda j, i: (i, 0)) for _ in extras]
    elif mode == "resid":
        in_specs += [pl.BlockSpec((tm, tn), lambda j, i: (i, j))]
    if layout == "rows":
        out_spec = pl.BlockSpec((tm, tn), lambda j, i: (i, j))
        out_shape = (M, N)
    elif layout == "cols":
        out_spec = pl.BlockSpec((tn, tm), lambda j, i: (j, i))
        out_shape = (N, M)
    else:
        assert tm % VT == 0
        out_spec = pl.BlockSpec((tm // VT, tn, VT), lambda j, i: (i, j, 0))
        out_shape = (M // VT, N, VT)
    return pl.pallas_call(
        functools.partial(_mm_body, mode=mode, layout=layout, cast_w=cast_w, scale=scale),
        grid=(N // tn, M // tm),
        in_specs=in_specs,
        out_specs=out_spec,
        out_shape=jax.ShapeDtypeStruct(out_shape, out_dtype),
        scratch_shapes=[pltpu.VMEM((K, tn), BF16)] if cast_w else [],
        compiler_params=_params(("parallel", "arbitrary" if cast_w else "parallel")),
        name=name,
    )(a, b, *extras)


def _swiglu_body(a_ref, bg_ref, bu_ref, o_ref, wg_ref, wu_ref):
    _cast_once(bg_ref, wg_ref)
    _cast_once(bu_ref, wu_ref)
    a = a_ref[...]
    g = _dot(a, wg_ref[...])
    u = _dot(a, wu_ref[...])
    o_ref[...] = (g * jax.nn.sigmoid(g) * u).astype(o_ref.dtype)


def _mm_swiglu(a, w, layer, *, tm, tn):
    M, K = a.shape
    H = w.shape[-1] // 2
    nj = H // tn
    assert H % tn == 0 and M % tm == 0
    return pl.pallas_call(
        _swiglu_body,
        grid=(nj, M // tm),
        in_specs=[pl.BlockSpec((tm, K), lambda j, i: (i, 0)),
                  pl.BlockSpec((None, K, tn), lambda j, i: (layer, 0, j)),
                  pl.BlockSpec((None, K, tn), lambda j, i: (layer, 0, j + nj))],
        out_specs=pl.BlockSpec((tm, tn), lambda j, i: (i, j)),
        out_shape=jax.ShapeDtypeStruct((M, H), BF16),
        scratch_shapes=[pltpu.VMEM((K, tn), BF16), pltpu.VMEM((K, tn), BF16)],
        compiler_params=_params(("parallel", "arbitrary")),
        name="ffn_in_swiglu",
    )(a, w, w)


def _gate_merge_body(h_ref, wg0, wg1, wg2, wg3, bg_ref, br0, br1, br2, br3, wb_ref, o_ref, wbs_ref):
    _cast_once(wb_ref, wbs_ref)
    h = h_ref[...]
    acc = None
    for n, (wg, br) in enumerate(((wg0, br0), (wg1, br1), (wg2, br2), (wg3, br3))):
        gl = _dot(h, wg[...]) + bg_ref[n]
        y = _dot(br[...], wbs_ref[n]) if n == 0 else _dot_tn(br[...], wbs_ref[n])
        t = jax.nn.sigmoid(gl) * y
        acc = t if acc is None else acc + t
    o_ref[...] = acc.astype(o_ref.dtype)


def _gate_merge(h, wg, bg, branches, wb, layer, *, tm, tn):
    S, D = h.shape
    nj = D // tn
    wg_specs = [pl.BlockSpec((None, D, tn), (lambda j, i, n=n: (layer, 0, n * nj + j))) for n in range(N_BRANCH)]
    br_specs = [pl.BlockSpec((tm, BRANCH_WIDTH), lambda j, i: (i, 0))]
    br_specs += [pl.BlockSpec((BRANCH_WIDTH, tm), lambda j, i: (0, i)) for _ in range(N_BRANCH - 1)]
    return pl.pallas_call(
        _gate_merge_body,
        grid=(nj, S // tm),
        in_specs=[pl.BlockSpec((tm, D), lambda j, i: (i, 0))] + wg_specs
        + [pl.BlockSpec((None, N_BRANCH, 1, tn), lambda j, i: (layer, 0, 0, j))] + br_specs
        + [pl.BlockSpec((None, N_BRANCH, BRANCH_WIDTH, tn), lambda j, i: (layer, 0, 0, j))],
        out_specs=pl.BlockSpec((tm, tn), lambda j, i: (i, j)),
        out_shape=jax.ShapeDtypeStruct((S, D), BF16),
        scratch_shapes=[pltpu.VMEM((N_BRANCH, BRANCH_WIDTH, tn), BF16)],
        compiler_params=_params(("parallel", "arbitrary")),
        name="gate_merge",
    )(h, wg, wg, wg, wg, bg, *branches, wb)


A_CHUNK = QB * max(d for _, d in DIL_PATTERNS)


def _mixer_a_body(*refs, ch):
    ng = len(DIL_PATTERNS)
    ins = [refs[5 * g:5 * g + 5] for g in range(ng)]
    o_ref = refs[5 * ng]
    qf, kf, vf, of, lf = refs[5 * ng + 1:]
    c = pl.program_id(1)
    qi = lax.broadcasted_iota(I32, (QB, 2 * QB), 0)
    kj = lax.broadcasted_iota(I32, (QB, 2 * QB), 1)
    bias_rest = jnp.where((kj >= qi) & (kj <= qi + QB), 0.0, NEG)
    bias_first = jnp.where((kj >= QB) & (kj <= qi + QB), 0.0, NEG)
    bias_n0 = jnp.where(c == 0, bias_first, bias_rest)

    for g, (_, d) in enumerate(DIL_PATTERNS):
        q_ref, k_ref, kp_ref, v_ref, vp_ref = ins[g]
        nb = ch // (QB * d)
        if d > 1:
            qf[...] = q_ref[...].astype(F32)
            kf[0:ch] = kp_ref[...].astype(F32)
            kf[ch:2 * ch] = k_ref[...].astype(F32)
            vf[0:ch] = vp_ref[...].astype(F32)
            vf[ch:2 * ch] = v_ref[...].astype(F32)

        def rows(r, n):
            return pl.ds(r + n * QB * d, QB, stride=d) if d > 1 else pl.ds(n * QB, QB)

        def window(src, prev, cur, r, n):
            if d > 1:
                return src[pl.ds(ch + r + (n - 1) * QB * d, 2 * QB, stride=d), :].astype(BF16)
            if n == 0:
                return jnp.concatenate([prev[ch - QB:ch, :], cur[0:QB, :]], axis=0)
            return cur[(n - 1) * QB:(n + 1) * QB, :]

        blocks = [(r, n) for r in range(d) for n in range(nb)]
        logits = []
        for r, n in blocks:
            qb = qf[rows(r, n), :].astype(BF16) if d > 1 else q_ref[rows(r, n), :]
            s = _dot_nt(qb, window(kf, kp_ref, k_ref, r, n)) * ATTN_SCALE
            logits.append(s + (bias_n0 if n == 0 else bias_rest))
        stats = []
        for s in logits:
            m = jnp.max(s, axis=-1, keepdims=True)
            p = jnp.exp(s - m)
            stats.append((m, p, jnp.sum(p, axis=-1, keepdims=True)))
        for (r, n), (m, p, den) in zip(blocks, stats):
            pv = _dot(p.astype(BF16), window(vf, vp_ref, v_ref, r, n))
            of[g, rows(r, n), :] = pv / den
            lf[g, rows(r, n), :] = jnp.broadcast_to(m + jnp.log(den), (QB, HEAD_DIM))

    a = [lf[g] for g in range(ng)]
    mx = functools.reduce(jnp.maximum, a)
    e = [jnp.exp(x - mx) for x in a]
    num = sum(e[g] * of[g] for g in range(ng))
    o_ref[...] = (num / sum(e)).astype(o_ref.dtype)


def _mixer_a(qk_arr, v_arr, *, S):
    G = A_HEADS_PER_GROUP
    ch = min(A_CHUNK, S)
    assert S % ch == 0 and all(w // d == QB and ch % (QB * d) == 0 for w, d in DIL_PATTERNS)
    in_specs, args = [], []
    for g in range(len(DIL_PATTERNS)):
        qc, kc, vc = g * G, A_HEADS + g * G, g * G
        in_specs += [pl.BlockSpec((ch, HEAD_DIM), lambda h, c, o=qc: (c, o + h)),
                     pl.BlockSpec((ch, HEAD_DIM), lambda h, c, o=kc: (c, o + h)),
                     pl.BlockSpec((ch, HEAD_DIM), lambda h, c, o=kc: (jnp.maximum(c - 1, 0), o + h)),
                     pl.BlockSpec((ch, HEAD_DIM), lambda h, c, o=vc: (c, o + h)),
                     pl.BlockSpec((ch, HEAD_DIM), lambda h, c, o=vc: (jnp.maximum(c - 1, 0), o + h))]
        args += [qk_arr, qk_arr, qk_arr, v_arr, v_arr]
    ng = len(DIL_PATTERNS)
    return pl.pallas_call(
        functools.partial(_mixer_a_body, ch=ch),
        grid=(G, S // ch),
        in_specs=in_specs,
        out_specs=pl.BlockSpec((ch, HEAD_DIM), lambda h, c: (c, h)),
        out_shape=jax.ShapeDtypeStruct((S, G * HEAD_DIM), BF16),
        scratch_shapes=[pltpu.VMEM((ch, HEAD_DIM), F32), pltpu.VMEM((2 * ch, HEAD_DIM), F32),
                        pltpu.VMEM((2 * ch, HEAD_DIM), F32), pltpu.VMEM((ng, ch, HEAD_DIM), F32),
                        pltpu.VMEM((ng, ch, HEAD_DIM), F32)],
        compiler_params=_params(("parallel", "parallel")),
        name="dilated_attn",
    )(*args)


def _softmax_step(t, m):
    m_new = jnp.maximum(m, jnp.max(t, axis=0, keepdims=True))
    return m_new, jnp.exp2(m - m_new), jnp.exp2(t - m_new).astype(BF16)


def _pv(vt_ref, tile0, rows, p):
    ones = jnp.ones((DEN_ROWS, VT), BF16)
    out = None
    for c in range(p.shape[0] // VT):
        v_aug = jnp.concatenate([vt_ref[tile0 + c, rows, :], ones], axis=0)
        part = _dot(v_aug, p[c * VT:(c + 1) * VT])
        out = part if out is None else out + part
    return out


def _causal_bias(j, q0, tk, tq):
    kpos = j * tk + lax.broadcasted_iota(I32, (tk, tq), 0)
    qpos = q0 + lax.broadcasted_iota(I32, (tk, tq), 1)
    return jnp.where(kpos <= qpos, 0.0, NEG)


def _forget_cumsum_body(f_ref, b_ref, g_ref, *, rows_per_head):
    x = f_ref[...] + b_ref[...]
    x = jnp.minimum(x, 0.0) - jnp.log1p(jnp.exp(-jnp.abs(x)))
    lane = lax.broadcasted_iota(I32, x.shape, 1)
    s = 1
    while s < LANE:
        x = x + jnp.where(lane >= s, pltpu.roll(x, s, axis=1), 0.0)
        s *= 2
    tot = jnp.broadcast_to(x[:, LANE - 1:LANE], x.shape)
    row = lax.broadcasted_iota(I32, x.shape, 0) % rows_per_head
    t = tot
    s = 1
    while s < rows_per_head:
        t = t + jnp.where(row >= s, pltpu.roll(t, s, axis=0), 0.0)
        s *= 2
    c = x + (t - tot)
    gam = c * (-LOG2E)
    hi = gam.astype(BF16)
    r1 = gam - hi.astype(F32)
    mid = r1.astype(BF16)
    lo = (r1 - mid.astype(F32)).astype(BF16)
    g_ref[0] = hi
    g_ref[1] = mid
    g_ref[2] = lo


def _forget_decay(f_t, b_f):
    H, S = f_t.shape
    rph = S // LANE
    rows = H * rph
    b_col = jnp.repeat(b_f.astype(F32), rph).reshape(rows, 1)
    g = pl.pallas_call(
        functools.partial(_forget_cumsum_body, rows_per_head=rph),
        out_shape=jax.ShapeDtypeStruct((3, rows, LANE), BF16),
        name="forget_cumsum",
    )(f_t.reshape(rows, LANE), b_col)
    return g.reshape(3, H, S)


def _fox_body(q_ref, k_ref, vt_ref, o_ref, acc_ref, *, tq, tk, ka):
    i = pl.program_id(0)
    q0 = i * tq
    H = B_HEADS
    n_full = q0 // tk
    n_all = (q0 + tq + tk - 1) // tk
    acc_ref[...] = jnp.zeros(acc_ref.shape, F32)

    def tile(j, ms, masked):
        rows = pl.ds(pl.multiple_of(j * tk, tk), tk)
        ts = [_dot(k_ref[rows, h * ka:(h + 1) * ka], q_ref[h * ka:(h + 1) * ka, :]) for h in range(H)]
        bias = _causal_bias(j, q0, tk, tq) if masked else None
        steps = [_softmax_step(ts[h] + bias if masked else ts[h], ms[h]) for h in range(H)]
        for h, (_, alpha, p) in enumerate(steps):
            acc_ref[h] = alpha * acc_ref[h] + _pv(vt_ref, j * (tk // VT), slice(h * HEAD_DIM, (h + 1) * HEAD_DIM), p)
        return tuple(s[0] for s in steps)

    init = tuple(jnp.full((1, tq), M_INIT, F32) for _ in range(H))
    ms = lax.fori_loop(0, n_full, lambda j, c: tile(j, c, False), init)
    lax.fori_loop(n_full, n_all, lambda j, c: tile(j, c, True), ms)
    for h in range(H):
        o_ref[h * HEAD_DIM:(h + 1) * HEAD_DIM, :] = (
            acc_ref[h, :HEAD_DIM, :] / acc_ref[h, HEAD_DIM:HEAD_DIM + 1, :]).astype(o_ref.dtype)


def _fox(q_aug_t, k_aug, v_t, v_blk, *, S):
    tq, tk = min(TQ, S), min(TK, S)
    H = B_HEADS
    ka = k_aug.shape[1] // H
    W = H * HEAD_DIM
    return pl.pallas_call(
        functools.partial(_fox_body, tq=tq, tk=tk, ka=ka),
        grid=(S // tq,),
        in_specs=[pl.BlockSpec((H * ka, tq), lambda i: (0, i)),
                  _resident((S, H * ka), lambda i: (0, 0)),
                  _resident((S // VT, W, VT), lambda i: (0, v_blk, 0))],
        out_specs=pl.BlockSpec((W, tq), lambda i: (0, i)),
        out_shape=jax.ShapeDtypeStruct((W, S), BF16),
        scratch_shapes=[pltpu.VMEM((H, PV_ROWS, tq), F32)],
        compiler_params=_params(("parallel",)),
        name="fox_attn",
    )(q_aug_t, k_aug, v_t)


def _dsa_body(qi_ref, kidx_ref, misc_ref, q_ref, k_ref, vt_ref, o_ref, keys_ref, hi_ref, acc_ref,
              *, tq, tk, ts, topk, w_row):
    i = pl.program_id(0)
    q0 = i * tq
    H = C_HEADS
    n_tiles = (q0 + tq + tk - 1) // tk
    n_full = q0 // tk
    ns_all = (q0 + tq) // ts
    ns_full = q0 // ts
    w = misc_ref[w_row:w_row + IDX_HEADS, :] * (IDX_DIM ** -0.5 * IDX_HEADS ** -0.5)

    def score_tile(j, masked):
        rows = pl.ds(pl.multiple_of(j * ts, ts), ts)
        kt = kidx_ref[rows, :]
        acc = jnp.zeros((ts, tq), F32)
        for h in range(IDX_HEADS):
            rel = _dot(kt, qi_ref[h])
            acc = acc + jnp.maximum(rel, 0.0) * w[h:h + 1, :]
        s = jnp.where(jnp.abs(acc) < F32_MIN_NORMAL, 0.0, acc)
        if masked:
            kpos = j * ts + lax.broadcasted_iota(I32, (ts, tq), 0)
            qpos = q0 + lax.broadcasted_iota(I32, (ts, tq), 1)
            s = jnp.where(kpos <= qpos, s, NEG)
        bits = pltpu.bitcast(s, I32)
        keys_ref[rows, :] = bits ^ (lax.shift_right_arithmetic(bits, 31) & 0x7FFFFFFF)
        hi_ref[rows, :] = pltpu.bitcast(bits & HI16_MASK, F32).astype(BF16)

    def _s_full(j, c):
        score_tile(j, False)
        return c

    def _s_mask(j, c):
        score_tile(j, True)
        return c

    lax.fori_loop(0, ns_full, _s_full, 0)
    lax.fori_loop(ns_full, ns_all, _s_mask, 0)

    @pl.when(ns_all * ts < n_tiles * tk)
    def _():
        pad_rows = pl.ds(pl.multiple_of(ns_all * ts, ts), ts)
        keys_ref[pad_rows, :] = jnp.full((ts, tq), NEG_KEY, I32)
        hi_ref[pad_rows, :] = jnp.full((ts, tq), NEG_HI, BF16)

    kf = float(topk)
    hi_grp = 2 * 16

    def count_hi(cand16):
        cand16 = jnp.where((cand16 > 0) & (cand16 < BF16_MIN_NORMAL_BITS), BF16_MIN_NORMAL_BITS, cand16)
        b16 = jnp.where(cand16 >= 0, cand16, cand16 ^ 0x7FFF) & 0xFFFF
        cand = pltpu.bitcast(lax.shift_left(b16, 16), F32).astype(BF16)

        def body(c, acc):
            t = hi_ref[pl.ds(pl.multiple_of(c * tk, tk), tk), :]
            ind = jnp.where(t >= cand, jnp.ones((), BF16), jnp.zeros((), BF16))
            parts = [ind[r * hi_grp:(r + 1) * hi_grp] for r in range(tk // hi_grp)]
            while len(parts) > 1:
                parts = [parts[k] + parts[k + 1] for k in range(0, len(parts), 2)]
            return acc + parts[0].astype(F32)
        acc = lax.fori_loop(0, n_tiles, body, jnp.zeros((hi_grp, tq), F32))
        return jnp.sum(acc, axis=0, keepdims=True)

    lo16 = jnp.where(count_hi(jnp.zeros((1, tq), I32)) >= kf, 0, -(2 ** 15)).astype(I32)

    def hi_step(b, lo16):
        cand = lo16 + lax.shift_left(jnp.int32(1), 14 - b)
        return jnp.where(count_hi(cand) >= kf, cand, lo16)

    lo16 = lax.fori_loop(0, 15, hi_step, lo16)
    n_acc = 4

    def count_ge(cand):
        def body(c, acc):
            t = keys_ref[pl.ds(pl.multiple_of(c * tk, tk), tk), :]
            ind = jnp.where(t >= cand, 1.0, 0.0)
            part = jnp.sum(ind.reshape(tk // (n_acc * SUBLANE), n_acc * SUBLANE, tq), axis=0)
            return acc + part
        acc = lax.fori_loop(0, n_tiles, body, jnp.zeros((n_acc * SUBLANE, tq), F32))
        return jnp.sum(acc, axis=0, keepdims=True)

    def bit_step(b, lo):
        cand = lo + lax.shift_left(jnp.int32(1), 15 - b)
        return jnp.where(count_ge(cand) >= kf, cand, lo)

    thr = lax.fori_loop(0, 16, bit_step, lax.shift_left(lo16, 16))

    n_ge = count_ge(thr)
    tied = jnp.where((n_ge > kf) & (thr != NEG_KEY), 1.0, 0.0)
    has_ties = jnp.max(tied) > 0.0

    @pl.when(has_ties)
    def _():
        need = kf - count_ge(thr + 1)
        tri = (lax.broadcasted_iota(I32, (tk, tk), 0) >= lax.broadcasted_iota(I32, (tk, tk), 1)).astype(BF16)

        def demote(j, seen):
            rows = pl.ds(pl.multiple_of(j * tk, tk), tk)
            key = keys_ref[rows, :]
            eq = jnp.where(key == thr, 1.0, 0.0)
            rank = _dot(tri, eq.astype(BF16)) + seen
            keys_ref[rows, :] = jnp.where((eq > 0.5) & (rank > need), INT_MIN, key)
            return seen + jnp.sum(eq, axis=0, keepdims=True)

        lax.fori_loop(0, n_tiles, demote, jnp.zeros((1, tq), F32))

    def to_bias(j, masked):
        rows = pl.ds(pl.multiple_of(j * tk, tk), tk)
        sel = keys_ref[rows, :] >= thr
        if masked:
            kpos = j * tk + lax.broadcasted_iota(I32, (tk, tq), 0)
            qpos = q0 + lax.broadcasted_iota(I32, (tk, tq), 1)
            sel = sel & (kpos <= qpos)
        keys_ref[rows, :] = pltpu.bitcast(jnp.where(sel, 0.0, NEG), I32)

    def _b_full(j, c):
        to_bias(j, False)
        return c

    def _b_mask(j, c):
        to_bias(j, True)
        return c

    lax.fori_loop(0, n_full, _b_full, 0)
    lax.fori_loop(n_full, n_tiles, _b_mask, 0)

    acc_ref[...] = jnp.zeros(acc_ref.shape, F32)
    hd = [slice(h * HEAD_DIM, (h + 1) * HEAD_DIM) for h in range(H)]

    def tile(j, ms):
        rows = pl.ds(pl.multiple_of(j * tk, tk), tk)
        ts = [_dot(k_ref[rows, hd[h]], q_ref[hd[h], :]) for h in range(H)]
        bias = pltpu.bitcast(keys_ref[rows, :], F32)
        steps = [_softmax_step(ts[h] + bias, ms[h]) for h in range(H)]
        for h, (_, alpha, p) in enumerate(steps):
            acc_ref[h] = alpha * acc_ref[h] + _pv(vt_ref, j * (tk // VT), hd[h], p)
        return tuple(s[0] for s in steps)

    lax.fori_loop(0, n_tiles, tile, tuple(jnp.full((1, tq), M_INIT, F32) for _ in range(H)))
    for h in range(H):
        den = jnp.maximum(acc_ref[h, HEAD_DIM:HEAD_DIM + 1, :], TINY)
        o_ref[h * HEAD_DIM:(h + 1) * HEAD_DIM, :] = (acc_ref[h, :HEAD_DIM, :] / den).astype(o_ref.dtype)


def _dsa(qi_t, kidx, misc_t, w_row, q_t, q_blk, k_arr, k_blk, v_t, v_blk, *, S):
    tq, tk = min(TQ, S), min(TK, S)
    ts = min(256, tk)
    topk = min(DSA_TOPK, S // 4)
    W = C_HEADS * HEAD_DIM
    assert S % tk == 0 and tk % ts == 0 and tq % ts == 0
    kern = functools.partial(_dsa_body, tq=tq, tk=tk, ts=ts, topk=topk, w_row=w_row)
    return pl.pallas_call(
        kern,
        grid=(S // tq,),
        in_specs=[pl.BlockSpec((IDX_HEADS, IDX_DIM, tq), lambda i: (0, 0, i)),
                  _resident((S, IDX_DIM), lambda i: (0, 0)),
                  pl.BlockSpec((LANE, tq), lambda i: (0, i)),
                  pl.BlockSpec((W, tq), lambda i: (q_blk, i)),
                  _resident((S, W), lambda i: (0, k_blk)),
                  _resident((S // VT, W, VT), lambda i: (0, v_blk, 0))],
        out_specs=pl.BlockSpec((W, tq), lambda i: (0, i)),
        out_shape=jax.ShapeDtypeStruct((W, S), BF16),
        scratch_shapes=[pltpu.VMEM((S, tq), I32), pltpu.VMEM((S, tq), BF16),
                        pltpu.VMEM((C_HEADS, PV_ROWS, tq), F32)],
        compiler_params=_params(("arbitrary",)),
        name="dsa_attn",
    )(qi_t, kidx, misc_t, q_t, k_arr, v_t)


def _compress_body(x_ref, pea_ref, peb_ref, w1a_ref, w1b_ref, w2_ref, o_ref, ot_ref):
    x = x_ref[...].astype(F32)
    a = _dot((x + pea_ref[...]).astype(BF16), w1a_ref[...].astype(BF16))
    b = _dot((x + peb_ref[...]).astype(BF16), w1b_ref[...].astype(BF16))
    n = a.shape[0]
    hid = a + pltpu.roll(b, n - 1, axis=0)
    out = _dot(jax.nn.gelu(hid).astype(BF16), w2_ref[...].astype(BF16))
    o_ref[...] = out.astype(o_ref.dtype)
    ot_ref[...] = out.T.astype(ot_ref.dtype)


def _compress(x2, pe, w1, w2):
    _, S, Dh = x2.shape
    n = S // CMP_STRIDE
    half = CMP_STRIDE * Dh
    xr = x2.reshape(2, n, half)
    pe_a = pe[:, :CMP_STRIDE].reshape(2, 1, half)
    pe_b = pe[:, CMP_STRIDE:].reshape(2, 1, half)
    return pl.pallas_call(
        _compress_body,
        grid=(2,),
        in_specs=[pl.BlockSpec((None, n, half), lambda g: (g, 0, 0)),
                  pl.BlockSpec((None, 1, half), lambda g: (g, 0, 0)),
                  pl.BlockSpec((None, 1, half), lambda g: (g, 0, 0)),
                  pl.BlockSpec((None, half, CMP_HIDDEN), lambda g: (g, 0, 0)),
                  pl.BlockSpec((None, half, CMP_HIDDEN), lambda g: (g, 1, 0)),
                  pl.BlockSpec((None, CMP_HIDDEN, Dh), lambda g: (g, 0, 0))],
        out_specs=[pl.BlockSpec((None, n, Dh), lambda g: (g, 0, 0)),
                   pl.BlockSpec((None, Dh, n), lambda g: (g, 0, 0))],
        out_shape=[jax.ShapeDtypeStruct((2, n, Dh), BF16), jax.ShapeDtypeStruct((2, Dh, n), BF16)],
        compiler_params=_params(("parallel",)),
        name="nsa_compress",
    )(xr, pe_a, pe_b, w1, w1, w2)


def _nsa_body(q_ref, kc_ref, vct_ref, ks_ref, kw_ref, vst_ref, vwt_ref, misc_ref, ov_ref, o_ref, acc_ref, qs_ref,
              *, tq, tk, n_slc, n_sel, ncp, gate_row):
    i = pl.program_id(0)
    q0 = i * tq
    H = D_HEADS
    R = H * tq
    blk_shift = int(math.log2(SLC_BLOCK))
    qs = jnp.concatenate([q_ref[h * HEAD_DIM:(h + 1) * HEAD_DIM, :] for h in range(H)], axis=1)

    def qpos_of(shape):
        return q0 + (lax.broadcasted_iota(I32, shape, 1) & (tq - 1))

    lc = _dot(kc_ref[...], qs)
    cend = lax.broadcasted_iota(I32, (ncp, R), 0) * CMP_STRIDE + (CMP_LEN - 1)
    mc = cend <= qpos_of((ncp, R))
    lc = jnp.where(mc, lc, NEG)
    mx = jnp.max(lc, axis=0, keepdims=True)
    pc = jnp.where(mc, jnp.exp2(lc - mx), 0.0)
    pc = pc * (1.0 / jnp.maximum(jnp.sum(pc, axis=0, keepdims=True), TINY))
    o_cmp = _dot(vct_ref[...], pc.astype(BF16))

    psum = pc[:, 0:tq]
    for h in range(1, H):
        psum = psum + pc[:, h * tq:(h + 1) * tq]
    p_hi = psum.astype(BF16)
    p_lo = (psum - p_hi.astype(F32)).astype(BF16)
    ov = ov_ref[...]
    imp = _dot(ov, p_hi) + _dot(ov, p_lo)
    jblk = lax.broadcasted_iota(I32, (n_slc, tq), 0)
    jblk_f = jblk.astype(F32)
    qpos_l = q0 + lax.broadcasted_iota(I32, (n_slc, tq), 1)
    cur = lax.shift_right_logical(qpos_l, blk_shift)
    forced = (jblk == 0) | (jblk == cur) | (jblk == cur - 1)
    val = jnp.where(forced, FORCE, jnp.where(jblk * SLC_BLOCK <= qpos_l, imp, NEG))
    sel_t = jnp.zeros((n_slc, tq), F32)
    for _ in range(n_sel):
        mxv = jnp.max(val, axis=0, keepdims=True)
        first = jnp.min(jnp.where(val == mxv, jblk_f, float(n_slc)), axis=0, keepdims=True)
        hit = jblk_f == first
        sel_t = jnp.where(hit, 1.0, sel_t)
        val = jnp.where(hit, -jnp.inf, val)
    sel_b = ((sel_t - 1.0) * (-NEG)).astype(BF16)

    bpt = tk // SLC_BLOCK
    n_tiles = (q0 + tq + tk - 1) // tk
    n_full = q0 // tk
    acc_ref[...] = jnp.zeros(acc_ref.shape, F32)
    row_blk = lax.shift_right_logical(lax.broadcasted_iota(I32, (tk, n_slc), 0), blk_shift)
    col_blk = lax.broadcasted_iota(I32, (tk, n_slc), 1)

    n_ch = 2
    cw = R // n_ch
    chains = [slice(c * cw, (c + 1) * cw) for c in range(n_ch)]
    qs_ref[...] = qs

    def slc_tile(j, ms, masked):
        rows = pl.ds(pl.multiple_of(j * tk, tk), tk)
        ks = ks_ref[rows, :]
        ts = [_dot(ks, qs_ref[:, c]) for c in chains]
        expand = jnp.where(row_blk == col_blk - j * bpt, 1.0, 0.0).astype(BF16)
        bias = _dot(expand, sel_b)
        if masked:
            kpos = j * tk + lax.broadcasted_iota(I32, (tk, tq), 0)
            qpos = q0 + lax.broadcasted_iota(I32, (tk, tq), 1)
            bias = jnp.where(kpos <= qpos, bias, NEG)
        bias = jnp.concatenate([bias] * (H // n_ch), axis=1)
        steps = [_softmax_step(ts[c] + bias, ms[c]) for c in range(n_ch)]
        for c, (_, alpha, p) in enumerate(steps):
            acc_ref[:, chains[c]] = alpha * acc_ref[:, chains[c]] + _pv(vst_ref, j * (tk // VT), slice(None), p)
        return tuple(s[0] for s in steps)

    init = tuple(jnp.full((1, cw), M_INIT, F32) for _ in range(n_ch))
    ms = lax.fori_loop(0, n_full, lambda j, c: slc_tile(j, c, False), init)
    lax.fori_loop(n_full, n_tiles, lambda j, c: slc_tile(j, c, True), ms)
    o_slc = acc_ref[:HEAD_DIM, :] * (1.0 / acc_ref[HEAD_DIM:HEAD_DIM + 1, :])

    wlen = WIN + tq
    start = pl.multiple_of(jnp.maximum(q0 - WIN, 0), tq)
    lw = _dot(kw_ref[pl.ds(start, wlen), :], qs)
    dist = qpos_of((wlen, R)) - (start + lax.broadcasted_iota(I32, (wlen, R), 0))
    lw = jnp.where((dist >= 0) & (dist < WIN), lw, NEG)
    mxw = jnp.max(lw, axis=0, keepdims=True)
    pw = jnp.exp2(lw - mxw).astype(BF16)
    o_win = _pv(vwt_ref, start // VT, slice(None), pw)
    o_win = o_win[:HEAD_DIM] * (1.0 / o_win[HEAD_DIM:HEAD_DIM + 1])

    gates = jax.nn.sigmoid(misc_ref[gate_row:gate_row + 3 * H, :])
    for h in range(H):
        c = slice(h * tq, (h + 1) * tq)
        out = (gates[3 * h:3 * h + 1] * o_cmp[:, c] + gates[3 * h + 1:3 * h + 2] * o_slc[:, c]
               + gates[3 * h + 2:3 * h + 3] * o_win[:, c])
        o_ref[h * HEAD_DIM:(h + 1) * HEAD_DIM, :] = out.astype(o_ref.dtype)


def _nsa(q_t, q_blk, kc, vct, k_arr, ks_blk, kw_blk, v_t, vs_blk, vw_blk, misc_t, gate_row, *, S):
    tq, tk = min(TQ_NSA, S), min(TK, S)
    assert S >= WIN + tq and tq & (tq - 1) == 0 and tq == VT and WIN % VT == 0
    n_slc = S // SLC_BLOCK
    n_sel = min(SLC_COUNT, n_slc)
    ncp = S // CMP_STRIDE
    W = D_HEADS * HEAD_DIM
    ci = np.arange(ncp)[None, :]
    sj = np.arange(n_slc)[:, None]
    ov = ((ci * CMP_STRIDE < (sj + 1) * SLC_BLOCK) & (ci * CMP_STRIDE + CMP_LEN > sj * SLC_BLOCK)
          & (ci < ncp - 1))
    ov = jnp.asarray(ov, BF16)
    kern = functools.partial(_nsa_body, tq=tq, tk=tk, n_slc=n_slc, n_sel=n_sel, ncp=ncp, gate_row=gate_row)
    return pl.pallas_call(
        kern,
        grid=(S // tq,),
        in_specs=[pl.BlockSpec((W, tq), lambda i: (q_blk, i)),
                  _resident((ncp, HEAD_DIM), lambda i: (0, 0)),
                  _resident((HEAD_DIM, ncp), lambda i: (0, 0)),
                  _resident((S, HEAD_DIM), lambda i: (0, ks_blk)),
                  _resident((S, HEAD_DIM), lambda i: (0, kw_blk)),
                  _resident((S // VT, HEAD_DIM, VT), lambda i: (0, vs_blk, 0)),
                  _resident((S // VT, HEAD_DIM, VT), lambda i: (0, vw_blk, 0)),
                  pl.BlockSpec((LANE, tq), lambda i: (0, i)),
                  _resident((n_slc, ncp), lambda i: (0, 0))],
        out_specs=pl.BlockSpec((W, tq), lambda i: (0, i)),
        out_shape=jax.ShapeDtypeStruct((W, S), BF16),
        scratch_shapes=[pltpu.VMEM((PV_ROWS, D_HEADS * tq), F32), pltpu.VMEM((HEAD_DIM, D_HEADS * tq), BF16)],
        compiler_params=_params(("parallel",)),
        name="nsa_attn",
    )(q_t, kc, vct, k_arr, k_arr, v_t, v_t, misc_t, ov)


def _rope_tables(S):
    pos = jnp.arange(S, dtype=F32)[:, None]
    half = HEAD_DIM // 2
    ang = pos * (ROPE_THETA ** (-jnp.arange(half, dtype=F32) / half))[None, :]
    cos, sin = jnp.cos(ang), jnp.sin(ang)
    t128 = (jnp.concatenate([cos, cos], 1), jnp.concatenate([-sin, sin], 1))
    h64 = IDX_DIM // 2
    ang = pos * (ROPE_THETA ** (-jnp.arange(h64, dtype=F32) / h64))[None, :]
    cos, sin = jnp.cos(ang), jnp.sin(ang)
    z = jnp.zeros_like(sin)
    t64 = (jnp.concatenate([cos] * 4, 1), jnp.concatenate([-sin, z, -sin, z], 1),
           jnp.concatenate([z, sin, z, sin], 1))
    return t128, t64


def _split_w_in(w_in):
    offs = np.cumsum([0,
                      A_HEADS * HEAD_DIM, A_HEADS * HEAD_DIM, A_HEADS * HEAD_DIM,
                      B_HEADS * HEAD_DIM, B_HEADS * HEAD_DIM, B_HEADS * HEAD_DIM, B_HEADS,
                      C_HEADS * HEAD_DIM, C_HEADS * HEAD_DIM, C_HEADS * HEAD_DIM,
                      IDX_HEADS * IDX_DIM, IDX_DIM, IDX_HEADS,
                      D_HEADS * HEAD_DIM] + [HEAD_DIM] * 6 + [3 * D_HEADS])
    names = ["aq", "ak", "av", "bq", "bk", "bv", "bf", "cq", "ck", "cv", "cqi", "cki", "cwi",
             "dq", "dkc", "dvc", "dks", "dvs", "dkw", "dvw", "dg"]
    col = {n: w_in[:, offs[k]:offs[k + 1]] for k, n in enumerate(names)}
    D = w_in.shape[0]

    def pack(ns, pad=0):
        parts = [col[n] for n in ns]
        if pad:
            parts.append(jnp.zeros((D, pad), w_in.dtype))
        return jnp.concatenate(parts, axis=1).astype(BF16)

    n_misc = B_HEADS + IDX_HEADS + 3 * D_HEADS
    return dict(
        a_qk=pack(["aq", "ak"]),
        a_v=pack(["av"]),
        b_q=pack(["bq"]),
        b_k=pack(["bk", "dvc"]),
        cd_k=pack(["ck", "dkc", "dks", "dkw"]),
        cd_q=pack(["cq", "dq"]),
        idx=pack(["cqi", "cki"], pad=LANE - IDX_DIM),
        v_t=pack(["bv", "cv", "dvs", "dvw"]),
        misc=pack(["bf", "cwi", "dg"], pad=LANE - n_misc),
    )


def _pick(n, cands):
    for c in cands:
        if n % c == 0:
            return c
    return n


def _layer(x, p, stacked, layer, tables):
    S, D = x.shape
    (cos128, sin128), (cos64, sin64a, sin64b) = tables
    tm = _pick(S, (1024, 512, 256))
    h = _rmsnorm(x, p["norm_mix"], BF16)
    w = _split_w_in(p["w_in"])

    def proj(name, mode="plain", extras=(), out_dtype=BF16, layout="rows", scale=None):
        n = w[name].shape[1]
        tn = _pick(n, (1280, 1152, 1024, 896, 768, 640, 512, 384, 256, 128))
        return _mm(h, w[name], tm=tm, tn=tn, out_dtype=out_dtype, mode=mode, extras=extras, layout=layout,
                   name="in_" + name, scale=scale)

    a_qk = proj("a_qk", "rope128", (cos128, sin128))
    a_v = proj("a_v")
    b_k = proj("b_k")
    b_q_t = proj("b_q", layout="cols", scale=EXP_C)
    cd_k = proj("cd_k", "rope128", (cos128, sin128))
    cd_q_t = proj("cd_q", "rope128", (cos128, sin128), layout="cols", scale=EXP_C)
    idx_t = proj("idx", "rope64", (cos64, sin64a, sin64b), layout="cols")
    v_t = proj("v_t", layout="coltiles")
    misc_t = proj("misc", out_dtype=F32, layout="cols")

    o_a = _mixer_a(a_qk, a_v, S=S)

    BW = B_HEADS * HEAD_DIM
    gam = _forget_decay(misc_t[:B_HEADS], p["b_f"])
    pad = HEAD_DIM - 3
    bq_t = b_q_t.reshape(B_HEADS, HEAD_DIM, S)
    bk = b_k[:, :BW].reshape(S, B_HEADS, HEAD_DIM)
    q_aug_t = jnp.concatenate([bq_t, jnp.ones((B_HEADS, 3, S), BF16), jnp.zeros((B_HEADS, pad, S), BF16)], axis=1)
    k_aug = jnp.concatenate([bk, gam.transpose(2, 1, 0), jnp.zeros((S, B_HEADS, pad), BF16)], axis=2)
    o_b_t = _fox(q_aug_t.reshape(-1, S), k_aug.reshape(S, -1), v_t, 0, S=S)

    n_qi = IDX_HEADS * IDX_DIM
    qi_t = idx_t[:n_qi].reshape(IDX_HEADS, IDX_DIM, S)
    kidx = idx_t[n_qi:n_qi + IDX_DIM].T
    o_c_t = _dsa(qi_t, kidx, misc_t, B_HEADS, cd_q_t, 0, cd_k, 0, v_t, 1, S=S)

    cw = C_HEADS * HEAD_DIM
    nb = cw // HEAD_DIM
    kv_c = jnp.stack([cd_k[:, cw:cw + HEAD_DIM], b_k[:, BW:BW + HEAD_DIM]])
    kv_cmp, kv_cmp_t = _compress(kv_c, p["cmp_pe"], p["cmp_w1"], p["cmp_w2"])
    o_d_t = _nsa(cd_q_t, 1, kv_cmp[0], kv_cmp_t[1], cd_k, nb + 1, nb + 2, v_t, 2 * nb, 2 * nb + 1,
                 misc_t, B_HEADS + IDX_HEADS, S=S)

    tn = _pick(D, (512, 256, 128))
    nl = stacked["b_gate"].shape[0]
    bg = stacked["b_gate"].reshape(nl, N_BRANCH, 1, D)
    merged = _gate_merge(h, stacked["w_gate_bf16"], bg, (o_a, o_b_t, o_c_t, o_d_t), stacked["w_branch"], layer,
                         tm=tm, tn=tn)
    x = _mm(merged, stacked["w_out"], tm=tm, tn=tn, out_dtype=F32, mode="resid", extras=(x,), name="out_proj",
            layer=layer)

    h2 = _rmsnorm(x, p["norm_ffn"], BF16)
    Hf = stacked["w_ffn_in"].shape[-1] // 2
    act = _mm_swiglu(h2, stacked["w_ffn_in"], layer, tm=tm, tn=_pick(Hf, (512, 256, 128)))
    x = _mm(act, stacked["w_ffn_out"], tm=_pick(S, (512, 256)), tn=tn, out_dtype=F32, mode="resid",
            extras=(x,), name="ffn_out", layer=layer)
    return x


def kernel(x, norm_mix, w_in, w_gate, b_gate, b_f, cmp_pe, cmp_w1, cmp_w2, w_branch, w_out, norm_ffn,
           w_ffn_in, w_ffn_out, norm_final):
    B, S, D = x.shape
    tables = _rope_tables(S)
    nl = w_gate.shape[0]
    stacked = dict(w_gate_bf16=w_gate.reshape(nl, D, N_BRANCH * D).astype(BF16), b_gate=b_gate, w_branch=w_branch, w_out=w_out, w_ffn_in=w_ffn_in,
                   w_ffn_out=w_ffn_out)
    outs = []
    for b in range(B):
        xb = x[b]
        for l in range(norm_mix.shape[0]):
            p = dict(norm_mix=norm_mix[l], w_in=w_in[l], b_f=b_f[l], cmp_pe=cmp_pe[l], cmp_w1=cmp_w1[l],
                     cmp_w2=cmp_w2[l], norm_ffn=norm_ffn[l])
            xb = _layer(xb, p, stacked, l, tables)
        outs.append(_rmsnorm(xb, norm_final, x.dtype))
    return jnp.stack(outs)
```

```python
import functools
import math

import jax
import jax.numpy as jnp
import numpy as np
from jax import lax
from jax.experimental import pallas as pl
from jax.experimental.pallas import tpu as pltpu

F32 = jnp.float32
BF16 = jnp.bfloat16
I32 = jnp.int32

HEAD_DIM = 128
ROPE_THETA = 10000.0
ATTN_SCALE = HEAD_DIM ** -0.5
QB = 128
NEG = -1e30
TINY = 1e-30
DIL_PATTERNS = ((128, 1), (512, 4), (2048, 16))
A_HEADS_PER_GROUP = 4
A_HEADS = A_HEADS_PER_GROUP * len(DIL_PATTERNS)
B_HEADS = 4
C_HEADS = 4
IDX_HEADS = 16
IDX_DIM = 64
DSA_TOPK = 256
D_HEADS = 4
CMP_LEN = 32
CMP_STRIDE = 16
CMP_HIDDEN = 256
SLC_BLOCK = 64
SLC_COUNT = 16
WIN = 512
FORCE = 1e9
N_BRANCH = 4
BRANCH_WIDTH = 4 * HEAD_DIM

LANE = 128
SUBLANE = 8
VMEM_LIMIT = 56 * 1024 * 1024
VT = 256
TQ = 512
TQ_NSA = 256
NSA_CHAINS = 2
TK = 512

LOG2E = math.log2(math.e)
EXP_C = ATTN_SCALE * LOG2E
M_INIT = 0.5 * NEG
DEN_ROWS = 16
PV_ROWS = HEAD_DIM + DEN_ROWS

INT_MIN = -2 ** 31
HI16_MASK = -(2 ** 16)
F32_MIN_NORMAL = 2.0 ** -126
BF16_MIN_NORMAL_BITS = 0x0080
_NEG_BITS = int(np.array(NEG, np.float32).view(np.int32))
NEG_KEY = _NEG_BITS ^ 0x7FFFFFFF
NEG_HI = float(np.array(_NEG_BITS & 0xFFFF0000, np.uint32).view(np.float32))


def _params(sem):
    return pltpu.CompilerParams(dimension_semantics=sem, vmem_limit_bytes=VMEM_LIMIT)


def _resident(shape, index_map):
    return pl.BlockSpec(shape, index_map, pipeline_mode=pl.Buffered(1))


def _dot(a, b):
    return jnp.dot(a, b, preferred_element_type=F32)


def _dot_nt(a, b):
    return lax.dot_general(a, b, (((1,), (1,)), ((), ())), preferred_element_type=F32)


def _dot_tn(a, b):
    return lax.dot_general(a, b, (((0,), (0,)), ((), ())), preferred_element_type=F32)


def _rmsnorm_body(x_ref, g_ref, o_ref):
    x = x_ref[...]
    ms = jnp.mean(x * x, axis=-1, keepdims=True)
    o_ref[...] = (x * lax.rsqrt(ms + 1e-6) * g_ref[...]).astype(o_ref.dtype)


def _rmsnorm(x, g, out_dtype, tm=512):
    S, D = x.shape
    return pl.pallas_call(
        _rmsnorm_body,
        grid=(S // tm,),
        in_specs=[pl.BlockSpec((tm, D), lambda i: (i, 0)),
                  pl.BlockSpec((1, D), lambda i: (0, 0))],
        out_specs=pl.BlockSpec((tm, D), lambda i: (i, 0)),
        out_shape=jax.ShapeDtypeStruct((S, D), out_dtype),
        compiler_params=_params(("parallel",)),
        name="rmsnorm",
    )(x, g.reshape(1, D))


def _rope128(acc, cos, sin):
    parts = []
    for c in range(acc.shape[1] // LANE):
        blk = acc[:, c * LANE:(c + 1) * LANE]
        parts.append(blk * cos + pltpu.roll(blk, LANE // 2, axis=1) * sin)
    return parts[0] if len(parts) == 1 else jnp.concatenate(parts, axis=1)


def _rope64(acc, cos, sin_a, sin_b):
    parts = []
    for c in range(acc.shape[1] // LANE):
        blk = acc[:, c * LANE:(c + 1) * LANE]
        parts.append(blk * cos + pltpu.roll(blk, LANE - IDX_DIM // 2, axis=1) * sin_a
                     + pltpu.roll(blk, IDX_DIM // 2, axis=1) * sin_b)
    return parts[0] if len(parts) == 1 else jnp.concatenate(parts, axis=1)


def _cast_once(src_ref, dst_ref):
    @pl.when(pl.program_id(1) == 0)
    def _():
        dst_ref[...] = src_ref[...].astype(BF16)


def _mm_body(*refs, mode, layout, cast_w, scale):
    a_ref, b_ref = refs[0], refs[1]
    if cast_w:
        o_ref, wb_ref = refs[-2], refs[-1]
        _cast_once(b_ref, wb_ref)
        b_ref = wb_ref
    else:
        o_ref = refs[-1]
    acc = _dot(a_ref[...], b_ref[...])
    if mode == "rope128":
        acc = _rope128(acc, refs[2][...], refs[3][...])
    elif mode == "rope64":
        acc = _rope64(acc, refs[2][...], refs[3][...], refs[4][...])
    elif mode == "resid":
        acc = refs[2][...] + acc
    if scale is not None:
        acc = acc * scale
    if layout == "rows":
        o_ref[...] = acc.astype(o_ref.dtype)
    elif layout == "cols":
        o_ref[...] = acc.T.astype(o_ref.dtype)
    else:
        acc_t = acc.T
        for c in range(acc_t.shape[1] // VT):
            o_ref[c] = acc_t[:, c * VT:(c + 1) * VT].astype(o_ref.dtype)


def _mm(a, b, *, tm, tn, out_dtype, mode="plain", extras=(), layout="rows", name="mm", layer=None, scale=None):
    M, K = a.shape
    N = b.shape[-1]
    assert M % tm == 0 and N % tn == 0, (M, N, tm, tn)
    cast_w = layer is not None
    if cast_w:
        b_spec = pl.BlockSpec((None, K, tn), lambda j, i: (layer, 0, j))
    else:
        b_spec = pl.BlockSpec((K, tn), lambda j, i: (0, j))
    in_specs = [pl.BlockSpec((tm, K), lambda j, i: (i, 0)), b_spec]
    if mode in ("rope128", "rope64"):
        in_specs += [pl.BlockSpec((tm, LANE), lambda j, i: (i, 0)) for _ in extras]
    elif mode == "resid":
        in_specs += [pl.BlockSpec((tm, tn), lambda j, i: (i, j))]
    if layout == "rows":
        out_spec = pl.BlockSpec((tm, tn), lambda j, i: (i, j))
        out_shape = (M, N)
    elif layout == "cols":
        out_spec = pl.BlockSpec((tn, tm), lambda j, i: (j, i))
        out_shape = (N, M)
    else:
        assert tm % VT == 0
        out_spec = pl.BlockSpec((tm // VT, tn, VT), lambda j, i: (i, j, 0))
        out_shape = (M // VT, N, VT)
    return pl.pallas_call(
        functools.partial(_mm_body, mode=mode, layout=layout, cast_w=cast_w, scale=scale),
        grid=(N // tn, M // tm),
        in_specs=in_specs,
        out_specs=out_spec,
        out_shape=jax.ShapeDtypeStruct(out_shape, out_dtype),
        scratch_shapes=[pltpu.VMEM((K, tn), BF16)] if cast_w else [],
        compiler_params=_params(("parallel", "arbitrary" if cast_w else "parallel")),
        name=name,
    )(a, b, *extras)


def _swiglu_body(a_ref, bg_ref, bu_ref, o_ref, wg_ref, wu_ref):
    _cast_once(bg_ref, wg_ref)
    _cast_once(bu_ref, wu_ref)
    a = a_ref[...]
    g = _dot(a, wg_ref[...])
    u = _dot(a, wu_ref[...])
    o_ref[...] = (g * jax.nn.sigmoid(g) * u).astype(o_ref.dtype)


def _mm_swiglu(a, w, layer, *, tm, tn):
    M, K = a.shape
    H = w.shape[-1] // 2
    nj = H // tn
    assert H % tn == 0 and M % tm == 0
    return pl.pallas_call(
        _swiglu_body,
        grid=(nj, M // tm),
        in_specs=[pl.BlockSpec((tm, K), lambda j, i: (i, 0)),
                  pl.BlockSpec((None, K, tn), lambda j, i: (layer, 0, j)),
                  pl.BlockSpec((None, K, tn), lambda j, i: (layer, 0, j + nj))],
        out_specs=pl.BlockSpec((tm, tn), lambda j, i: (i, j)),
        out_shape=jax.ShapeDtypeStruct((M, H), BF16),
        scratch_shapes=[pltpu.VMEM((K, tn), BF16), pltpu.VMEM((K, tn), BF16)],
        compiler_params=_params(("parallel", "arbitrary")),
        name="ffn_in_swiglu",
    )(a, w, w)


def _gate_merge_body(h_ref, wg0, wg1, wg2, wg3, bg_ref, br0, br1, br2, br3, wb_ref, o_ref, wbs_ref):
    _cast_once(wb_ref, wbs_ref)
    h = h_ref[...]
    acc = None
    for n, (wg, br) in enumerate(((wg0, br0), (wg1, br1), (wg2, br2), (wg3, br3))):
        gl = _dot(h, wg[...]) + bg_ref[n]
        y = _dot(br[...], wbs_ref[n]) if n == 0 else _dot_tn(br[...], wbs_ref[n])
        t = jax.nn.sigmoid(gl) * y
        acc = t if acc is None else acc + t
    o_ref[...] = acc.astype(o_ref.dtype)


def _gate_merge(h, wg, bg, branches, wb, layer, *, tm, tn):
    S, D = h.shape
    nj = D // tn
    wg_specs = [pl.BlockSpec((None, D, tn), (lambda j, i, n=n: (layer, 0, n * nj + j))) for n in range(N_BRANCH)]
    br_specs = [pl.BlockSpec((tm, BRANCH_WIDTH), lambda j, i: (i, 0))]
    br_specs += [pl.BlockSpec((BRANCH_WIDTH, tm), lambda j, i: (0, i)) for _ in range(N_BRANCH - 1)]
    return pl.pallas_call(
        _gate_merge_body,
        grid=(nj, S // tm),
        in_specs=[pl.BlockSpec((tm, D), lambda j, i: (i, 0))] + wg_specs
        + [pl.BlockSpec((None, N_BRANCH, 1, tn), lambda j, i: (layer, 0, 0, j))] + br_specs
        + [pl.BlockSpec((None, N_BRANCH, BRANCH_WIDTH, tn), lambda j, i: (layer, 0, 0, j))],
        out_specs=pl.BlockSpec((tm, tn), lambda j, i: (i, j)),
        out_shape=jax.ShapeDtypeStruct((S, D), BF16),
        scratch_shapes=[pltpu.VMEM((N_BRANCH, BRANCH_WIDTH, tn), BF16)],
        compiler_params=_params(("parallel", "arbitrary")),
        name="gate_merge",
    )(h, wg, wg, wg, wg, bg, *branches, wb)


A_CHUNK = QB * max(d for _, d in DIL_PATTERNS)


def _mixer_a_body(*refs, ch):
    ng = len(DIL_PATTERNS)
    ins = [refs[5 * g:5 * g + 5] for g in range(ng)]
    o_ref = refs[5 * ng]
    qf, kf, vf, of, lf = refs[5 * ng + 1:]
    c = pl.program_id(1)
    qi = lax.broadcasted_iota(I32, (QB, 2 * QB), 0)
    kj = lax.broadcasted_iota(I32, (QB, 2 * QB), 1)
    bias_rest = jnp.where((kj >= qi) & (kj <= qi + QB), 0.0, NEG)
    bias_first = jnp.where((kj >= QB) & (kj <= qi + QB), 0.0, NEG)
    bias_n0 = jnp.where(c == 0, bias_first, bias_rest)

    for g, (_, d) in enumerate(DIL_PATTERNS):
        q_ref, k_ref, kp_ref, v_ref, vp_ref = ins[g]
        nb = ch // (QB * d)
        if d > 1:
            qf[...] = q_ref[...].astype(F32)
            kf[0:ch] = kp_ref[...].astype(F32)
            kf[ch:2 * ch] = k_ref[...].astype(F32)
            vf[0:ch] = vp_ref[...].astype(F32)
            vf[ch:2 * ch] = v_ref[...].astype(F32)

        def rows(r, n):
            return pl.ds(r + n * QB * d, QB, stride=d) if d > 1 else pl.ds(n * QB, QB)

        def window(src, prev, cur, r, n):
            if d > 1:
                return src[pl.ds(ch + r + (n - 1) * QB * d, 2 * QB, stride=d), :].astype(BF16)
            if n == 0:
                return jnp.concatenate([prev[ch - QB:ch, :], cur[0:QB, :]], axis=0)
            return cur[(n - 1) * QB:(n + 1) * QB, :]

        blocks = [(r, n) for r in range(d) for n in range(nb)]
        logits = []
        for r, n in blocks:
            qb = qf[rows(r, n), :].astype(BF16) if d > 1 else q_ref[rows(r, n), :]
            s = _dot_nt(qb, window(kf, kp_ref, k_ref, r, n)) * ATTN_SCALE
            logits.append(s + (bias_n0 if n == 0 else bias_rest))
        stats = []
        for s in logits:
            m = jnp.max(s, axis=-1, keepdims=True)
            p = jnp.exp(s - m)
            stats.append((m, p, jnp.sum(p, axis=-1, keepdims=True)))
        for (r, n), (m, p, den) in zip(blocks, stats):
            pv = _dot(p.astype(BF16), window(vf, vp_ref, v_ref, r, n))
            of[g, rows(r, n), :] = pv / den
            lf[g, rows(r, n), :] = jnp.broadcast_to(m + jnp.log(den), (QB, HEAD_DIM))

    a = [lf[g] for g in range(ng)]
    mx = functools.reduce(jnp.maximum, a)
    e = [jnp.exp(x - mx) for x in a]
    num = sum(e[g] * of[g] for g in range(ng))
    o_ref[...] = (num / sum(e)).astype(o_ref.dtype)


def _mixer_a(qk_arr, v_arr, *, S):
    G = A_HEADS_PER_GROUP
    ch = min(A_CHUNK, S)
    assert S % ch == 0 and all(w // d == QB and ch % (QB * d) == 0 for w, d in DIL_PATTERNS)
    in_specs, args = [], []
    for g in range(len(DIL_PATTERNS)):
        qc, kc, vc = g * G, A_HEADS + g * G, g * G
        in_specs += [pl.BlockSpec((ch, HEAD_DIM), lambda h, c, o=qc: (c, o + h)),
                     pl.BlockSpec((ch, HEAD_DIM), lambda h, c, o=kc: (c, o + h)),
                     pl.BlockSpec((ch, HEAD_DIM), lambda h, c, o=kc: (jnp.maximum(c - 1, 0), o + h)),
                     pl.BlockSpec((ch, HEAD_DIM), lambda h, c, o=vc: (c, o + h)),
                     pl.BlockSpec((ch, HEAD_DIM), lambda h, c, o=vc: (jnp.maximum(c - 1, 0), o + h))]
        args += [qk_arr, qk_arr, qk_arr, v_arr, v_arr]
    ng = len(DIL_PATTERNS)
    return pl.pallas_call(
        functools.partial(_mixer_a_body, ch=ch),
        grid=(G, S // ch),
        in_specs=in_specs,
        out_specs=pl.BlockSpec((ch, HEAD_DIM), lambda h, c: (c, h)),
        out_shape=jax.ShapeDtypeStruct((S, G * HEAD_DIM), BF16),
        scratch_shapes=[pltpu.VMEM((ch, HEAD_DIM), F32), pltpu.VMEM((2 * ch, HEAD_DIM), F32),
                        pltpu.VMEM((2 * ch, HEAD_DIM), F32), pltpu.VMEM((ng, ch, HEAD_DIM), F32),
                        pltpu.VMEM((ng, ch, HEAD_DIM), F32)],
        compiler_params=_params(("parallel", "parallel")),
        name="dilated_attn",
    )(*args)


def _softmax_step(t, m):
    m_new = jnp.maximum(m, jnp.max(t, axis=0, keepdims=True))
    return m_new, jnp.exp2(m - m_new), jnp.exp2(t - m_new).astype(BF16)


def _pv(vt_ref, tile0, rows, p):
    ones = jnp.ones((DEN_ROWS, VT), BF16)
    out = None
    for c in range(p.shape[0] // VT):
        v_aug = jnp.concatenate([vt_ref[tile0 + c, rows, :], ones], axis=0)
        part = _dot(v_aug, p[c * VT:(c + 1) * VT])
        out = part if out is None else out + part
    return out


def _retire_then_softmax(ts, ms, alphas, j, slots, vt_ref, head_rows, p_ref, acc_ref):
    read_slot, write_slot = slots
    tiles_per_key_tile = p_ref.shape[2] // VT
    j_prev = jnp.maximum(j - 1, 0)
    for h, rows in enumerate(head_rows):
        acc_ref[h] = alphas[h] * acc_ref[h] + _pv(vt_ref, j_prev * tiles_per_key_tile, rows, p_ref[read_slot, h])
    steps = [_softmax_step(ts[h], ms[h]) for h in range(len(head_rows))]
    for h, (_, _, p) in enumerate(steps):
        p_ref[write_slot, h] = p
    return tuple(s[0] for s in steps), tuple(s[1] for s in steps)


def _pipelined_sweep(n_plain, plain_stage, final_stage, vt_ref, head_rows, p_ref, acc_ref, width):
    n_heads = len(head_rows)
    p_ref[...] = jnp.zeros(p_ref.shape, p_ref.dtype)
    acc_ref[...] = jnp.zeros(acc_ref.shape, F32)
    carry = (tuple(jnp.full((1, width), M_INIT, F32) for _ in range(n_heads)),
             tuple(jnp.ones((1, width), F32) for _ in range(n_heads)))
    odd = lax.rem(n_plain, 2)
    carry = lax.cond(odd == 1, lambda c: plain_stage(0, c, (0, 1)), lambda c: c, carry)

    def pair(m, c):
        j = odd + 2 * m
        return plain_stage(j + 1, plain_stage(j, c, (1, 0)), (0, 1))

    carry = lax.fori_loop(0, n_plain // 2, pair, carry)
    _, alphas = final_stage(n_plain, carry, (1, 0))
    tiles_per_key_tile = p_ref.shape[2] // VT
    for h, rows in enumerate(head_rows):
        acc_ref[h] = alphas[h] * acc_ref[h] + _pv(vt_ref, n_plain * tiles_per_key_tile, rows, p_ref[0, h])


def _causal_bias(j, q0, tk, tq):
    kpos = j * tk + lax.broadcasted_iota(I32, (tk, tq), 0)
    qpos = q0 + lax.broadcasted_iota(I32, (tk, tq), 1)
    return jnp.where(kpos <= qpos, 0.0, NEG)


def _forget_cumsum_body(f_ref, b_ref, g_ref, *, rows_per_head):
    x = f_ref[...] + b_ref[...]
    x = jnp.minimum(x, 0.0) - jnp.log1p(jnp.exp(-jnp.abs(x)))
    lane = lax.broadcasted_iota(I32, x.shape, 1)
    s = 1
    while s < LANE:
        x = x + jnp.where(lane >= s, pltpu.roll(x, s, axis=1), 0.0)
        s *= 2
    tot = jnp.broadcast_to(x[:, LANE - 1:LANE], x.shape)
    row = lax.broadcasted_iota(I32, x.shape, 0) % rows_per_head
    t = tot
    s = 1
    while s < rows_per_head:
        t = t + jnp.where(row >= s, pltpu.roll(t, s, axis=0), 0.0)
        s *= 2
    c = x + (t - tot)
    gam = c * (-LOG2E)
    hi = gam.astype(BF16)
    r1 = gam - hi.astype(F32)
    mid = r1.astype(BF16)
    lo = (r1 - mid.astype(F32)).astype(BF16)
    g_ref[0] = hi
    g_ref[1] = mid
    g_ref[2] = lo


def _forget_decay(f_t, b_f):
    H, S = f_t.shape
    rph = S // LANE
    rows = H * rph
    b_col = jnp.repeat(b_f.astype(F32), rph).reshape(rows, 1)
    g = pl.pallas_call(
        functools.partial(_forget_cumsum_body, rows_per_head=rph),
        out_shape=jax.ShapeDtypeStruct((3, rows, LANE), BF16),
        name="forget_cumsum",
    )(f_t.reshape(rows, LANE), b_col)
    return g.reshape(3, H, S)


def _fox_body(q_ref, k_ref, vt_ref, o_ref, acc_ref, p_ref, *, t, ka):
    i = pl.program_id(0)
    H = B_HEADS
    hd = [slice(h * HEAD_DIM, (h + 1) * HEAD_DIM) for h in range(H)]

    def stage(j, carry, slots, masked):
        ms, alphas = carry
        rows = pl.ds(pl.multiple_of(j * t, t), t)
        ts = [_dot(k_ref[rows, h * ka:(h + 1) * ka], q_ref[h * ka:(h + 1) * ka, :]) for h in range(H)]
        if masked:
            bias = _causal_bias(j, i * t, t, t)
            ts = [x + bias for x in ts]
        return _retire_then_softmax(ts, ms, alphas, j, slots, vt_ref, hd, p_ref, acc_ref)

    _pipelined_sweep(i, functools.partial(stage, masked=False), functools.partial(stage, masked=True),
                     vt_ref, hd, p_ref, acc_ref, t)
    for h in range(H):
        o_ref[h * HEAD_DIM:(h + 1) * HEAD_DIM, :] = (
            acc_ref[h, :HEAD_DIM, :] / acc_ref[h, HEAD_DIM:HEAD_DIM + 1, :]).astype(o_ref.dtype)


def _fox(q_aug_t, k_aug, v_t, v_blk, *, S):
    tq, tk = min(TQ, S), min(TK, S)
    assert tq == tk
    H = B_HEADS
    ka = k_aug.shape[1] // H
    W = H * HEAD_DIM
    return pl.pallas_call(
        functools.partial(_fox_body, t=tq, ka=ka),
        grid=(S // tq,),
        in_specs=[pl.BlockSpec((H * ka, tq), lambda i: (0, i)),
                  _resident((S, H * ka), lambda i: (0, 0)),
                  _resident((S // VT, W, VT), lambda i: (0, v_blk, 0))],
        out_specs=pl.BlockSpec((W, tq), lambda i: (0, i)),
        out_shape=jax.ShapeDtypeStruct((W, S), BF16),
        scratch_shapes=[pltpu.VMEM((H, PV_ROWS, tq), F32), pltpu.VMEM((2, H, tk, tq), BF16)],
        compiler_params=_params(("parallel",)),
        name="fox_attn",
    )(q_aug_t, k_aug, v_t)


def _dsa_body(qi_ref, kidx_ref, misc_ref, q_ref, k_ref, vt_ref, o_ref, keys_ref, hi_ref, acc_ref, p_ref,
              *, tq, tk, ts, topk, w_row):
    i = pl.program_id(0)
    q0 = i * tq
    H = C_HEADS
    n_tiles = (q0 + tq + tk - 1) // tk
    n_full = q0 // tk
    ns_all = (q0 + tq) // ts
    ns_full = q0 // ts
    w = misc_ref[w_row:w_row + IDX_HEADS, :] * (IDX_DIM ** -0.5 * IDX_HEADS ** -0.5)

    def score_tile(j, masked):
        rows = pl.ds(pl.multiple_of(j * ts, ts), ts)
        kt = kidx_ref[rows, :]
        acc = jnp.zeros((ts, tq), F32)
        for h in range(IDX_HEADS):
            rel = _dot(kt, qi_ref[h])
            acc = acc + jnp.maximum(rel, 0.0) * w[h:h + 1, :]
        s = jnp.where(jnp.abs(acc) < F32_MIN_NORMAL, 0.0, acc)
        if masked:
            kpos = j * ts + lax.broadcasted_iota(I32, (ts, tq), 0)
            qpos = q0 + lax.broadcasted_iota(I32, (ts, tq), 1)
            s = jnp.where(kpos <= qpos, s, NEG)
        bits = pltpu.bitcast(s, I32)
        keys_ref[rows, :] = bits ^ (lax.shift_right_arithmetic(bits, 31) & 0x7FFFFFFF)
        hi_ref[rows, :] = pltpu.bitcast(bits & HI16_MASK, F32).astype(BF16)

    def _s_full(j, c):
        score_tile(j, False)
        return c

    def _s_mask(j, c):
        score_tile(j, True)
        return c

    lax.fori_loop(0, ns_full, _s_full, 0)
    lax.fori_loop(ns_full, ns_all, _s_mask, 0)

    @pl.when(ns_all * ts < n_tiles * tk)
    def _():
        pad_rows = pl.ds(pl.multiple_of(ns_all * ts, ts), ts)
        keys_ref[pad_rows, :] = jnp.full((ts, tq), NEG_KEY, I32)
        hi_ref[pad_rows, :] = jnp.full((ts, tq), NEG_HI, BF16)

    kf = float(topk)
    hi_grp = 2 * 16

    def count_hi(cand16):
        cand16 = jnp.where((cand16 > 0) & (cand16 < BF16_MIN_NORMAL_BITS), BF16_MIN_NORMAL_BITS, cand16)
        b16 = jnp.where(cand16 >= 0, cand16, cand16 ^ 0x7FFF) & 0xFFFF
        cand = pltpu.bitcast(lax.shift_left(b16, 16), F32).astype(BF16)

        def body(c, acc):
            t = hi_ref[pl.ds(pl.multiple_of(c * tk, tk), tk), :]
            ind = jnp.where(t >= cand, jnp.ones((), BF16), jnp.zeros((), BF16))
            parts = [ind[r * hi_grp:(r + 1) * hi_grp] for r in range(tk // hi_grp)]
            while len(parts) > 1:
                parts = [parts[k] + parts[k + 1] for k in range(0, len(parts), 2)]
            return acc + parts[0].astype(F32)
        acc = lax.fori_loop(0, n_tiles, body, jnp.zeros((hi_grp, tq), F32))
        return jnp.sum(acc, axis=0, keepdims=True)

    lo16 = jnp.where(count_hi(jnp.zeros((1, tq), I32)) >= kf, 0, -(2 ** 15)).astype(I32)

    def hi_step(b, lo16):
        cand = lo16 + lax.shift_left(jnp.int32(1), 14 - b)
        return jnp.where(count_hi(cand) >= kf, cand, lo16)

    lo16 = lax.fori_loop(0, 15, hi_step, lo16)
    n_acc = 4

    def count_ge(cand):
        def body(c, acc):
            t = keys_ref[pl.ds(pl.multiple_of(c * tk, tk), tk), :]
            ind = jnp.where(t >= cand, 1.0, 0.0)
            part = jnp.sum(ind.reshape(tk // (n_acc * SUBLANE), n_acc * SUBLANE, tq), axis=0)
            return acc + part
        acc = lax.fori_loop(0, n_tiles, body, jnp.zeros((n_acc * SUBLANE, tq), F32))
        return jnp.sum(acc, axis=0, keepdims=True)

    def bit_step(b, lo):
        cand = lo + lax.shift_left(jnp.int32(1), 15 - b)
        return jnp.where(count_ge(cand) >= kf, cand, lo)

    thr = lax.fori_loop(0, 16, bit_step, lax.shift_left(lo16, 16))

    n_ge = count_ge(thr)
    tied = jnp.where((n_ge > kf) & (thr != NEG_KEY), 1.0, 0.0)
    has_ties = jnp.max(tied) > 0.0

    @pl.when(has_ties)
    def _():
        need = kf - count_ge(thr + 1)
        tri = (lax.broadcasted_iota(I32, (tk, tk), 0) >= lax.broadcasted_iota(I32, (tk, tk), 1)).astype(BF16)

        def demote(j, seen):
            rows = pl.ds(pl.multiple_of(j * tk, tk), tk)
            key = keys_ref[rows, :]
            eq = jnp.where(key == thr, 1.0, 0.0)
            rank = _dot(tri, eq.astype(BF16)) + seen
            keys_ref[rows, :] = jnp.where((eq > 0.5) & (rank > need), INT_MIN, key)
            return seen + jnp.sum(eq, axis=0, keepdims=True)

        lax.fori_loop(0, n_tiles, demote, jnp.zeros((1, tq), F32))

    def to_bias(j, masked):
        rows = pl.ds(pl.multiple_of(j * tk, tk), tk)
        sel = keys_ref[rows, :] >= thr
        if masked:
            kpos = j * tk + lax.broadcasted_iota(I32, (tk, tq), 0)
            qpos = q0 + lax.broadcasted_iota(I32, (tk, tq), 1)
            sel = sel & (kpos <= qpos)
        keys_ref[rows, :] = pltpu.bitcast(jnp.where(sel, 0.0, NEG), I32)

    def _b_full(j, c):
        to_bias(j, False)
        return c

    def _b_mask(j, c):
        to_bias(j, True)
        return c

    lax.fori_loop(0, n_full, _b_full, 0)
    lax.fori_loop(n_full, n_tiles, _b_mask, 0)

    hd = [slice(h * HEAD_DIM, (h + 1) * HEAD_DIM) for h in range(H)]

    def stage(j, carry, slots):
        ms, alphas = carry
        rows = pl.ds(pl.multiple_of(j * tk, tk), tk)
        ts = [_dot(k_ref[rows, hd[h]], q_ref[hd[h], :]) for h in range(H)]
        bias = pltpu.bitcast(keys_ref[rows, :], F32)
        ts = [x + bias for x in ts]
        return _retire_then_softmax(ts, ms, alphas, j, slots, vt_ref, hd, p_ref, acc_ref)

    _pipelined_sweep(n_tiles - 1, stage, stage, vt_ref, hd, p_ref, acc_ref, tq)
    for h in range(H):
        den = jnp.maximum(acc_ref[h, HEAD_DIM:HEAD_DIM + 1, :], TINY)
        o_ref[h * HEAD_DIM:(h + 1) * HEAD_DIM, :] = (acc_ref[h, :HEAD_DIM, :] / den).astype(o_ref.dtype)


def _dsa(qi_t, kidx, misc_t, w_row, q_t, q_blk, k_arr, k_blk, v_t, v_blk, *, S):
    tq, tk = min(TQ, S), min(TK, S)
    ts = min(256, tk)
    topk = min(DSA_TOPK, S // 4)
    W = C_HEADS * HEAD_DIM
    assert S % tk == 0 and tk % ts == 0 and tq % ts == 0
    kern = functools.partial(_dsa_body, tq=tq, tk=tk, ts=ts, topk=topk, w_row=w_row)
    return pl.pallas_call(
        kern,
        grid=(S // tq,),
        in_specs=[pl.BlockSpec((IDX_HEADS, IDX_DIM, tq), lambda i: (0, 0, i)),
                  _resident((S, IDX_DIM), lambda i: (0, 0)),
                  pl.BlockSpec((LANE, tq), lambda i: (0, i)),
                  pl.BlockSpec((W, tq), lambda i: (q_blk, i)),
                  _resident((S, W), lambda i: (0, k_blk)),
                  _resident((S // VT, W, VT), lambda i: (0, v_blk, 0))],
        out_specs=pl.BlockSpec((W, tq), lambda i: (0, i)),
        out_shape=jax.ShapeDtypeStruct((W, S), BF16),
        scratch_shapes=[pltpu.VMEM((S, tq), I32), pltpu.VMEM((S, tq), BF16),
                        pltpu.VMEM((C_HEADS, PV_ROWS, tq), F32), pltpu.VMEM((2, C_HEADS, tk, tq), BF16)],
        compiler_params=_params(("arbitrary",)),
        name="dsa_attn",
    )(qi_t, kidx, misc_t, q_t, k_arr, v_t)


def _compress_body(x_ref, pea_ref, peb_ref, w1a_ref, w1b_ref, w2_ref, o_ref, ot_ref):
    x = x_ref[...].astype(F32)
    a = _dot((x + pea_ref[...]).astype(BF16), w1a_ref[...].astype(BF16))
    b = _dot((x + peb_ref[...]).astype(BF16), w1b_ref[...].astype(BF16))
    n = a.shape[0]
    hid = a + pltpu.roll(b, n - 1, axis=0)
    out = _dot(jax.nn.gelu(hid).astype(BF16), w2_ref[...].astype(BF16))
    o_ref[...] = out.astype(o_ref.dtype)
    ot_ref[...] = out.T.astype(ot_ref.dtype)


def _compress(x2, pe, w1, w2):
    _, S, Dh = x2.shape
    n = S // CMP_STRIDE
    half = CMP_STRIDE * Dh
    xr = x2.reshape(2, n, half)
    pe_a = pe[:, :CMP_STRIDE].reshape(2, 1, half)
    pe_b = pe[:, CMP_STRIDE:].reshape(2, 1, half)
    return pl.pallas_call(
        _compress_body,
        grid=(2,),
        in_specs=[pl.BlockSpec((None, n, half), lambda g: (g, 0, 0)),
                  pl.BlockSpec((None, 1, half), lambda g: (g, 0, 0)),
                  pl.BlockSpec((None, 1, half), lambda g: (g, 0, 0)),
                  pl.BlockSpec((None, half, CMP_HIDDEN), lambda g: (g, 0, 0)),
                  pl.BlockSpec((None, half, CMP_HIDDEN), lambda g: (g, 1, 0)),
                  pl.BlockSpec((None, CMP_HIDDEN, Dh), lambda g: (g, 0, 0))],
        out_specs=[pl.BlockSpec((None, n, Dh), lambda g: (g, 0, 0)),
                   pl.BlockSpec((None, Dh, n), lambda g: (g, 0, 0))],
        out_shape=[jax.ShapeDtypeStruct((2, n, Dh), BF16), jax.ShapeDtypeStruct((2, Dh, n), BF16)],
        compiler_params=_params(("parallel",)),
        name="nsa_compress",
    )(xr, pe_a, pe_b, w1, w1, w2)


def _nsa_body(q_ref, kc_ref, vct_ref, ks_ref, kw_ref, vst_ref, vwt_ref, misc_ref, ov_ref, o_ref, acc_ref, p_ref,
              qs_ref,
              *, tq, tk, n_slc, n_sel, ncp, gate_row):
    i = pl.program_id(0)
    q0 = i * tq
    H = D_HEADS
    R = H * tq
    blk_shift = int(math.log2(SLC_BLOCK))
    qs = jnp.concatenate([q_ref[h * HEAD_DIM:(h + 1) * HEAD_DIM, :] for h in range(H)], axis=1)

    def qpos_of(shape):
        return q0 + (lax.broadcasted_iota(I32, shape, 1) & (tq - 1))

    lc = _dot(kc_ref[...], qs)
    cend = lax.broadcasted_iota(I32, (ncp, R), 0) * CMP_STRIDE + (CMP_LEN - 1)
    mc = cend <= qpos_of((ncp, R))
    lc = jnp.where(mc, lc, NEG)
    mx = jnp.max(lc, axis=0, keepdims=True)
    pc = jnp.where(mc, jnp.exp2(lc - mx), 0.0)
    pc = pc * (1.0 / jnp.maximum(jnp.sum(pc, axis=0, keepdims=True), TINY))
    o_cmp = _dot(vct_ref[...], pc.astype(BF16))

    psum = pc[:, 0:tq]
    for h in range(1, H):
        psum = psum + pc[:, h * tq:(h + 1) * tq]
    p_hi = psum.astype(BF16)
    p_lo = (psum - p_hi.astype(F32)).astype(BF16)
    ov = ov_ref[...]
    imp = _dot(ov, p_hi) + _dot(ov, p_lo)
    jblk = lax.broadcasted_iota(I32, (n_slc, tq), 0)
    jblk_f = jblk.astype(F32)
    qpos_l = q0 + lax.broadcasted_iota(I32, (n_slc, tq), 1)
    cur = lax.shift_right_logical(qpos_l, blk_shift)
    forced = (jblk == 0) | (jblk == cur) | (jblk == cur - 1)
    val = jnp.where(forced, FORCE, jnp.where(jblk * SLC_BLOCK <= qpos_l, imp, NEG))
    sel_t = jnp.zeros((n_slc, tq), F32)
    for _ in range(n_sel):
        mxv = jnp.max(val, axis=0, keepdims=True)
        first = jnp.min(jnp.where(val == mxv, jblk_f, float(n_slc)), axis=0, keepdims=True)
        hit = jblk_f == first
        sel_t = jnp.where(hit, 1.0, sel_t)
        val = jnp.where(hit, -jnp.inf, val)
    sel_b = ((sel_t - 1.0) * (-NEG)).astype(BF16)

    bpt = tk // SLC_BLOCK
    n_tiles = (q0 + tq + tk - 1) // tk
    n_full = q0 // tk
    row_blk = lax.shift_right_logical(lax.broadcasted_iota(I32, (tk, n_slc), 0), blk_shift)
    col_blk = lax.broadcasted_iota(I32, (tk, n_slc), 1)

    n_ch = NSA_CHAINS
    cw = R // n_ch
    chains = [slice(c * cw, (c + 1) * cw) for c in range(n_ch)]
    qs_ref[...] = qs

    all_rows = [slice(None)] * n_ch

    def slc_stage(j, carry, slots, masked):
        ms, alphas = carry
        rows = pl.ds(pl.multiple_of(j * tk, tk), tk)
        ks = ks_ref[rows, :]
        ts = [_dot(ks, qs_ref[:, c]) for c in chains]
        expand = jnp.where(row_blk == col_blk - j * bpt, 1.0, 0.0).astype(BF16)
        bias = _dot(expand, sel_b)
        if masked:
            kpos = j * tk + lax.broadcasted_iota(I32, (tk, tq), 0)
            qpos = q0 + lax.broadcasted_iota(I32, (tk, tq), 1)
            bias = jnp.where(kpos <= qpos, bias, NEG)
        bias = jnp.concatenate([bias] * (H // n_ch), axis=1)
        ts = [x + bias for x in ts]
        return _retire_then_softmax(ts, ms, alphas, j, slots, vst_ref, all_rows, p_ref, acc_ref)

    _pipelined_sweep(n_full, functools.partial(slc_stage, masked=False), functools.partial(slc_stage, masked=True),
                     vst_ref, all_rows, p_ref, acc_ref, cw)
    o_slc = jnp.concatenate(
        [acc_ref[c, :HEAD_DIM, :] * (1.0 / acc_ref[c, HEAD_DIM:HEAD_DIM + 1, :]) for c in range(n_ch)], axis=1)

    wlen = WIN + tq
    start = pl.multiple_of(jnp.maximum(q0 - WIN, 0), tq)
    lw = _dot(kw_ref[pl.ds(start, wlen), :], qs)
    dist = qpos_of((wlen, R)) - (start + lax.broadcasted_iota(I32, (wlen, R), 0))
    lw = jnp.where((dist >= 0) & (dist < WIN), lw, NEG)
    mxw = jnp.max(lw, axis=0, keepdims=True)
    pw = jnp.exp2(lw - mxw).astype(BF16)
    o_win = _pv(vwt_ref, start // VT, slice(None), pw)
    o_win = o_win[:HEAD_DIM] * (1.0 / o_win[HEAD_DIM:HEAD_DIM + 1])

    gates = jax.nn.sigmoid(misc_ref[gate_row:gate_row + 3 * H, :])
    for h in range(H):
        c = slice(h * tq, (h + 1) * tq)
        out = (gates[3 * h:3 * h + 1] * o_cmp[:, c] + gates[3 * h + 1:3 * h + 2] * o_slc[:, c]
               + gates[3 * h + 2:3 * h + 3] * o_win[:, c])
        o_ref[h * HEAD_DIM:(h + 1) * HEAD_DIM, :] = out.astype(o_ref.dtype)


def _nsa(q_t, q_blk, kc, vct, k_arr, ks_blk, kw_blk, v_t, vs_blk, vw_blk, misc_t, gate_row, *, S):
    tq, tk = min(TQ_NSA, S), min(TK, S)
    assert S >= WIN + tq and tq & (tq - 1) == 0 and tq == VT and WIN % VT == 0
    n_slc = S // SLC_BLOCK
    n_sel = min(SLC_COUNT, n_slc)
    ncp = S // CMP_STRIDE
    W = D_HEADS * HEAD_DIM
    ci = np.arange(ncp)[None, :]
    sj = np.arange(n_slc)[:, None]
    ov = ((ci * CMP_STRIDE < (sj + 1) * SLC_BLOCK) & (ci * CMP_STRIDE + CMP_LEN > sj * SLC_BLOCK)
          & (ci < ncp - 1))
    ov = jnp.asarray(ov, BF16)
    kern = functools.partial(_nsa_body, tq=tq, tk=tk, n_slc=n_slc, n_sel=n_sel, ncp=ncp, gate_row=gate_row)
    return pl.pallas_call(
        kern,
        grid=(S // tq,),
        in_specs=[pl.BlockSpec((W, tq), lambda i: (q_blk, i)),
                  _resident((ncp, HEAD_DIM), lambda i: (0, 0)),
                  _resident((HEAD_DIM, ncp), lambda i: (0, 0)),
                  _resident((S, HEAD_DIM), lambda i: (0, ks_blk)),
                  _resident((S, HEAD_DIM), lambda i: (0, kw_blk)),
                  _resident((S // VT, HEAD_DIM, VT), lambda i: (0, vs_blk, 0)),
                  _resident((S // VT, HEAD_DIM, VT), lambda i: (0, vw_blk, 0)),
                  pl.BlockSpec((LANE, tq), lambda i: (0, i)),
                  _resident((n_slc, ncp), lambda i: (0, 0))],
        out_specs=pl.BlockSpec((W, tq), lambda i: (0, i)),
        out_shape=jax.ShapeDtypeStruct((W, S), BF16),
        scratch_shapes=[pltpu.VMEM((NSA_CHAINS, PV_ROWS, D_HEADS * tq // NSA_CHAINS), F32),
                        pltpu.VMEM((2, NSA_CHAINS, tk, D_HEADS * tq // NSA_CHAINS), BF16),
                        pltpu.VMEM((HEAD_DIM, D_HEADS * tq), BF16)],
        compiler_params=_params(("parallel",)),
        name="nsa_attn",
    )(q_t, kc, vct, k_arr, k_arr, v_t, v_t, misc_t, ov)


def _rope_tables(S):
    pos = jnp.arange(S, dtype=F32)[:, None]
    half = HEAD_DIM // 2
    ang = pos * (ROPE_THETA ** (-jnp.arange(half, dtype=F32) / half))[None, :]
    cos, sin = jnp.cos(ang), jnp.sin(ang)
    t128 = (jnp.concatenate([cos, cos], 1), jnp.concatenate([-sin, sin], 1))
    h64 = IDX_DIM // 2
    ang = pos * (ROPE_THETA ** (-jnp.arange(h64, dtype=F32) / h64))[None, :]
    cos, sin = jnp.cos(ang), jnp.sin(ang)
    z = jnp.zeros_like(sin)
    t64 = (jnp.concatenate([cos] * 4, 1), jnp.concatenate([-sin, z, -sin, z], 1),
           jnp.concatenate([z, sin, z, sin], 1))
    return t128, t64


def _split_w_in(w_in):
    offs = np.cumsum([0,
                      A_HEADS * HEAD_DIM, A_HEADS * HEAD_DIM, A_HEADS * HEAD_DIM,
                      B_HEADS * HEAD_DIM, B_HEADS * HEAD_DIM, B_HEADS * HEAD_DIM, B_HEADS,
                      C_HEADS * HEAD_DIM, C_HEADS * HEAD_DIM, C_HEADS * HEAD_DIM,
                      IDX_HEADS * IDX_DIM, IDX_DIM, IDX_HEADS,
                      D_HEADS * HEAD_DIM] + [HEAD_DIM] * 6 + [3 * D_HEADS])
    names = ["aq", "ak", "av", "bq", "bk", "bv", "bf", "cq", "ck", "cv", "cqi", "cki", "cwi",
             "dq", "dkc", "dvc", "dks", "dvs", "dkw", "dvw", "dg"]
    col = {n: w_in[:, offs[k]:offs[k + 1]] for k, n in enumerate(names)}
    D = w_in.shape[0]

    def pack(ns, pad=0):
        parts = [col[n] for n in ns]
        if pad:
            parts.append(jnp.zeros((D, pad), w_in.dtype))
        return jnp.concatenate(parts, axis=1).astype(BF16)

    n_misc = B_HEADS + IDX_HEADS + 3 * D_HEADS
    return dict(
        a_qk=pack(["aq", "ak"]),
        a_v=pack(["av"]),
        b_q=pack(["bq"]),
        b_k=pack(["bk", "dvc"]),
        cd_k=pack(["ck", "dkc", "dks", "dkw"]),
        cd_q=pack(["cq", "dq"]),
        idx=pack(["cqi", "cki"], pad=LANE - IDX_DIM),
        v_t=pack(["bv", "cv", "dvs", "dvw"]),
        misc=pack(["bf", "cwi", "dg"], pad=LANE - n_misc),
    )


def _pick(n, cands):
    for c in cands:
        if n % c == 0:
            return c
    return n


def _layer(x, p, stacked, layer, tables):
    S, D = x.shape
    (cos128, sin128), (cos64, sin64a, sin64b) = tables
    tm = _pick(S, (1024, 512, 256))
    h = _rmsnorm(x, p["norm_mix"], BF16)
    w = _split_w_in(p["w_in"])

    def proj(name, mode="plain", extras=(), out_dtype=BF16, layout="rows", scale=None):
        n = w[name].shape[1]
        tn = _pick(n, (1280, 1152, 1024, 896, 768, 640, 512, 384, 256, 128))
        return _mm(h, w[name], tm=tm, tn=tn, out_dtype=out_dtype, mode=mode, extras=extras, layout=layout,
                   name="in_" + name, scale=scale)

    a_qk = proj("a_qk", "rope128", (cos128, sin128))
    a_v = proj("a_v")
    b_k = proj("b_k")
    b_q_t = proj("b_q", layout="cols", scale=EXP_C)
    cd_k = proj("cd_k", "rope128", (cos128, sin128))
    cd_q_t = proj("cd_q", "rope128", (cos128, sin128), layout="cols", scale=EXP_C)
    idx_t = proj("idx", "rope64", (cos64, sin64a, sin64b), layout="cols")
    v_t = proj("v_t", layout="coltiles")
    misc_t = proj("misc", out_dtype=F32, layout="cols")

    o_a = _mixer_a(a_qk, a_v, S=S)

    BW = B_HEADS * HEAD_DIM
    gam = _forget_decay(misc_t[:B_HEADS], p["b_f"])
    pad = HEAD_DIM - 3
    bq_t = b_q_t.reshape(B_HEADS, HEAD_DIM, S)
    bk = b_k[:, :BW].reshape(S, B_HEADS, HEAD_DIM)
    q_aug_t = jnp.concatenate([bq_t, jnp.ones((B_HEADS, 3, S), BF16), jnp.zeros((B_HEADS, pad, S), BF16)], axis=1)
    k_aug = jnp.concatenate([bk, gam.transpose(2, 1, 0), jnp.zeros((S, B_HEADS, pad), BF16)], axis=2)
    o_b_t = _fox(q_aug_t.reshape(-1, S), k_aug.reshape(S, -1), v_t, 0, S=S)

    n_qi = IDX_HEADS * IDX_DIM
    qi_t = idx_t[:n_qi].reshape(IDX_HEADS, IDX_DIM, S)
    kidx = idx_t[n_qi:n_qi + IDX_DIM].T
    o_c_t = _dsa(qi_t, kidx, misc_t, B_HEADS, cd_q_t, 0, cd_k, 0, v_t, 1, S=S)

    cw = C_HEADS * HEAD_DIM
    nb = cw // HEAD_DIM
    kv_c = jnp.stack([cd_k[:, cw:cw + HEAD_DIM], b_k[:, BW:BW + HEAD_DIM]])
    kv_cmp, kv_cmp_t = _compress(kv_c, p["cmp_pe"], p["cmp_w1"], p["cmp_w2"])
    o_d_t = _nsa(cd_q_t, 1, kv_cmp[0], kv_cmp_t[1], cd_k, nb + 1, nb + 2, v_t, 2 * nb, 2 * nb + 1,
                 misc_t, B_HEADS + IDX_HEADS, S=S)

    tn = _pick(D, (512, 256, 128))
    nl = stacked["b_gate"].shape[0]
    bg = stacked["b_gate"].reshape(nl, N_BRANCH, 1, D)
    merged = _gate_merge(h, stacked["w_gate_bf16"], bg, (o_a, o_b_t, o_c_t, o_d_t), stacked["w_branch"], layer,
                         tm=tm, tn=tn)
    x = _mm(merged, stacked["w_out"], tm=tm, tn=tn, out_dtype=F32, mode="resid", extras=(x,), name="out_proj",
            layer=layer)

    h2 = _rmsnorm(x, p["norm_ffn"], BF16)
    Hf = stacked["w_ffn_in"].shape[-1] // 2
    act = _mm_swiglu(h2, stacked["w_ffn_in"], layer, tm=tm, tn=_pick(Hf, (512, 256, 128)))
    x = _mm(act, stacked["w_ffn_out"], tm=_pick(S, (512, 256)), tn=tn, out_dtype=F32, mode="resid",
            extras=(x,), name="ffn_out", layer=layer)
    return x


def kernel(x, norm_mix, w_in, w_gate, b_gate, b_f, cmp_pe, cmp_w1, cmp_w2, w_branch, w_out, norm_ffn,
           w_ffn_in, w_ffn_out, norm_final):
    B, S, D = x.shape
    tables = _rope_tables(S)
    nl = w_gate.shape[0]
    stacked = dict(w_gate_bf16=w_gate.astype(BF16).reshape(nl, D, N_BRANCH * D), b_gate=b_gate, w_branch=w_branch, w_out=w_out, w_ffn_in=w_ffn_in,
                   w_ffn_out=w_ffn_out)
    outs = []
    for b in range(B):
        xb = x[b]
        for l in range(norm_mix.shape[0]):
            p = dict(norm_mix=norm_mix[l], w_in=w_in[l], b_f=b_f[l], cmp_pe=cmp_pe[l], cmp_w1=cmp_w1[l],
                     cmp_w2=cmp_w2[l], norm_ffn=norm_ffn[l])
            xb = _layer(xb, p, stacked, l, tables)
        outs.append(_rmsnorm(xb, norm_final, x.dtype))
    return jnp.stack(outs)
```

```python
import functools
import math

import jax
import jax.numpy as jnp
import numpy as np
from jax import lax
from jax.experimental import pallas as pl
from jax.experimental.pallas import tpu as pltpu

F32 = jnp.float32
BF16 = jnp.bfloat16
I32 = jnp.int32

HEAD_DIM = 128
ROPE_THETA = 10000.0
ATTN_SCALE = HEAD_DIM ** -0.5
QB = 128
NEG = -1e30
TINY = 1e-30
DIL_PATTERNS = ((128, 1), (512, 4), (2048, 16))
A_HEADS_PER_GROUP = 4
A_HEADS = A_HEADS_PER_GROUP * len(DIL_PATTERNS)
B_HEADS = 4
C_HEADS = 4
IDX_HEADS = 16
IDX_DIM = 64
DSA_TOPK = 256
D_HEADS = 4
CMP_LEN = 32
CMP_STRIDE = 16
CMP_HIDDEN = 256
SLC_BLOCK = 64
SLC_COUNT = 16
WIN = 512
FORCE = 1e9
N_BRANCH = 4
BRANCH_WIDTH = 4 * HEAD_DIM

LANE = 128
SUBLANE = 8
VMEM_LIMIT = 56 * 1024 * 1024
VT = 256
TQ = 512
TQ_NSA = 256
NSA_CHAINS = 2
TK = 512

LOG2E = math.log2(math.e)
EXP_C = ATTN_SCALE * LOG2E
M_INIT = 0.5 * NEG
DEN_ROWS = 16
PV_ROWS = HEAD_DIM + DEN_ROWS

INT_MIN = -2 ** 31
HI16_MASK = -(2 ** 16)
F32_MIN_NORMAL = 2.0 ** -126
BF16_MIN_NORMAL_BITS = 0x0080
_NEG_BITS = int(np.array(NEG, np.float32).view(np.int32))
NEG_KEY = _NEG_BITS ^ 0x7FFFFFFF
NEG_HI = float(np.array(_NEG_BITS & 0xFFFF0000, np.uint32).view(np.float32))


def _params(sem):
    return pltpu.CompilerParams(dimension_semantics=sem, vmem_limit_bytes=VMEM_LIMIT)


def _resident(shape, index_map):
    return pl.BlockSpec(shape, index_map, pipeline_mode=pl.Buffered(1))


def _dot(a, b):
    return jnp.dot(a, b, preferred_element_type=F32)


def _dot_nt(a, b):
    return lax.dot_general(a, b, (((1,), (1,)), ((), ())), preferred_element_type=F32)


def _dot_tn(a, b):
    return lax.dot_general(a, b, (((0,), (0,)), ((), ())), preferred_element_type=F32)


def _rmsnorm_body(x_ref, g_ref, o_ref):
    x = x_ref[...]
    ms = jnp.mean(x * x, axis=-1, keepdims=True)
    o_ref[...] = (x * lax.rsqrt(ms + 1e-6) * g_ref[...]).astype(o_ref.dtype)


def _rmsnorm(x, g, out_dtype, tm=1024):
    S, D = x.shape
    tm = min(tm, S)
    return pl.pallas_call(
        _rmsnorm_body,
        grid=(S // tm,),
        in_specs=[pl.BlockSpec((tm, D), lambda i: (i, 0)),
                  pl.BlockSpec((1, D), lambda i: (0, 0))],
        out_specs=pl.BlockSpec((tm, D), lambda i: (i, 0)),
        out_shape=jax.ShapeDtypeStruct((S, D), out_dtype),
        compiler_params=_params(("parallel",)),
        name="rmsnorm",
    )(x, g.reshape(1, D))


def _rope128(acc, cos, sin):
    parts = []
    for c in range(acc.shape[1] // LANE):
        blk = acc[:, c * LANE:(c + 1) * LANE]
        parts.append(blk * cos + pltpu.roll(blk, LANE // 2, axis=1) * sin)
    return parts[0] if len(parts) == 1 else jnp.concatenate(parts, axis=1)


def _rope64(acc, cos, sin_a, sin_b):
    parts = []
    for c in range(acc.shape[1] // LANE):
        blk = acc[:, c * LANE:(c + 1) * LANE]
        parts.append(blk * cos + pltpu.roll(blk, LANE - IDX_DIM // 2, axis=1) * sin_a
                     + pltpu.roll(blk, IDX_DIM // 2, axis=1) * sin_b)
    return parts[0] if len(parts) == 1 else jnp.concatenate(parts, axis=1)


def _cast_once(src_ref, dst_ref):
    @pl.when(pl.program_id(1) == 0)
    def _():
        dst_ref[...] = src_ref[...].astype(BF16)


def _mm_body(*refs, mode, layout, cast_w, scale):
    a_ref, b_ref = refs[0], refs[1]
    if cast_w:
        o_ref, wb_ref = refs[-2], refs[-1]
        _cast_once(b_ref, wb_ref)
        b_ref = wb_ref
    else:
        o_ref = refs[-1]
    acc = _dot(a_ref[...], b_ref[...])
    if mode == "rope128":
        acc = _rope128(acc, refs[2][...], refs[3][...])
    elif mode == "rope64":
        acc = _rope64(acc, refs[2][...], refs[3][...], refs[4][...])
    elif mode == "resid":
        acc = refs[2][...] + acc
    if scale is not None:
        acc = acc * scale
    if layout == "rows":
        o_ref[...] = acc.astype(o_ref.dtype)
    elif layout == "cols":
        o_ref[...] = acc.T.astype(o_ref.dtype)
    else:
        acc_t = acc.T
        for c in range(acc_t.shape[1] // VT):
            o_ref[c] = acc_t[:, c * VT:(c + 1) * VT].astype(o_ref.dtype)


def _mm(a, b, *, tm, tn, out_dtype, mode="plain", extras=(), layout="rows", name="mm", layer=None, scale=None):
    M, K = a.shape
    N = b.shape[-1]
    assert M % tm == 0 and N % tn == 0, (M, N, tm, tn)
    cast_w = layer is not None
    if cast_w:
        b_spec = pl.BlockSpec((None, K, tn), lambda j, i: (layer, 0, j))
    else:
        b_spec = pl.BlockSpec((K, tn), lambda j, i: (0, j))
    in_specs = [pl.BlockSpec((tm, K), lambda j, i: (i, 0)), b_spec]
    if mode in ("rope128", "rope64"):
        in_specs += [pl.BlockSpec((tm, LANE), lambda j, i: (i, 0)) for _ in extras]
    elif mode == "resid":
        in_specs += [pl.BlockSpec((tm, tn), lambda j, i: (i, j))]
    if layout == "rows":
        out_spec = pl.BlockSpec((tm, tn), lambda j, i: (i, j))
        out_shape = (M, N)
    elif layout == "cols":
        out_spec = pl.BlockSpec((tn, tm), lambda j, i: (j, i))
        out_shape = (N, M)
    else:
        assert tm % VT == 0
        out_spec = pl.BlockSpec((tm // VT, tn, VT), lambda j, i: (i, j, 0))
        out_shape = (M // VT, N, VT)
    return pl.pallas_call(
        functools.partial(_mm_body, mode=mode, layout=layout, cast_w=cast_w, scale=scale),
        grid=(N // tn, M // tm),
        in_specs=in_specs,
        out_specs=out_spec,
        out_shape=jax.ShapeDtypeStruct(out_shape, out_dtype),
        scratch_shapes=[pltpu.VMEM((K, tn), BF16)] if cast_w else [],
        compiler_params=_params(("parallel", "arbitrary" if cast_w else "parallel")),
        name=name,
    )(a, b, *extras)


def _swiglu_body(a_ref, bg_ref, bu_ref, o_ref, wg_ref, wu_ref):
    _cast_once(bg_ref, wg_ref)
    _cast_once(bu_ref, wu_ref)
    a = a_ref[...]
    g = _dot(a, wg_ref[...])
    u = _dot(a, wu_ref[...])
    o_ref[...] = (g * jax.nn.sigmoid(g) * u).astype(o_ref.dtype)


def _mm_swiglu(a, w, layer, *, tm, tn):
    M, K = a.shape
    H = w.shape[-1] // 2
    nj = H // tn
    assert H % tn == 0 and M % tm == 0
    return pl.pallas_call(
        _swiglu_body,
        grid=(nj, M // tm),
        in_specs=[pl.BlockSpec((tm, K), lambda j, i: (i, 0)),
                  pl.BlockSpec((None, K, tn), lambda j, i: (layer, 0, j)),
                  pl.BlockSpec((None, K, tn), lambda j, i: (layer, 0, j + nj))],
        out_specs=pl.BlockSpec((tm, tn), lambda j, i: (i, j)),
        out_shape=jax.ShapeDtypeStruct((M, H), BF16),
        scratch_shapes=[pltpu.VMEM((K, tn), BF16), pltpu.VMEM((K, tn), BF16)],
        compiler_params=_params(("parallel", "arbitrary")),
        name="ffn_in_swiglu",
    )(a, w, w)


def _gate_merge_body(h_ref, wg0, wg1, wg2, wg3, bg_ref, br0, br1, br2, br3, wb_ref, o_ref, wbs_ref):
    _cast_once(wb_ref, wbs_ref)
    h = h_ref[...]
    acc = None
    for n, (wg, br) in enumerate(((wg0, br0), (wg1, br1), (wg2, br2), (wg3, br3))):
        gl = _dot(h, wg[...]) + bg_ref[n]
        y = _dot(br[...], wbs_ref[n]) if n == 0 else _dot_tn(br[...], wbs_ref[n])
        t = jax.nn.sigmoid(gl) * y
        acc = t if acc is None else acc + t
    o_ref[...] = acc.astype(o_ref.dtype)


def _gate_merge(h, wg, bg, branches, wb, layer, *, tm, tn):
    S, D = h.shape
    nj = D // tn
    wg_specs = [pl.BlockSpec((None, D, tn), (lambda j, i, n=n: (layer, 0, n * nj + j))) for n in range(N_BRANCH)]
    br_specs = [pl.BlockSpec((tm, BRANCH_WIDTH), lambda j, i: (i, 0))]
    br_specs += [pl.BlockSpec((BRANCH_WIDTH, tm), lambda j, i: (0, i)) for _ in range(N_BRANCH - 1)]
    return pl.pallas_call(
        _gate_merge_body,
        grid=(nj, S // tm),
        in_specs=[pl.BlockSpec((tm, D), lambda j, i: (i, 0))] + wg_specs
        + [pl.BlockSpec((None, N_BRANCH, 1, tn), lambda j, i: (layer, 0, 0, j))] + br_specs
        + [pl.BlockSpec((None, N_BRANCH, BRANCH_WIDTH, tn), lambda j, i: (layer, 0, 0, j))],
        out_specs=pl.BlockSpec((tm, tn), lambda j, i: (i, j)),
        out_shape=jax.ShapeDtypeStruct((S, D), BF16),
        scratch_shapes=[pltpu.VMEM((N_BRANCH, BRANCH_WIDTH, tn), BF16)],
        compiler_params=_params(("parallel", "arbitrary")),
        name="gate_merge",
    )(h, wg, wg, wg, wg, bg, *branches, wb)


A_CHUNK = QB * max(d for _, d in DIL_PATTERNS)


def _mixer_a_body(*refs, ch):
    ng = len(DIL_PATTERNS)
    ins = [refs[5 * g:5 * g + 5] for g in range(ng)]
    o_ref = refs[5 * ng]
    qf, kf, vf, of, lf = refs[5 * ng + 1:]
    c = pl.program_id(1)
    qi = lax.broadcasted_iota(I32, (QB, 2 * QB), 0)
    kj = lax.broadcasted_iota(I32, (QB, 2 * QB), 1)
    bias_rest = jnp.where((kj >= qi) & (kj <= qi + QB), 0.0, NEG)
    bias_first = jnp.where((kj >= QB) & (kj <= qi + QB), 0.0, NEG)
    bias_n0 = jnp.where(c == 0, bias_first, bias_rest)

    for g, (_, d) in enumerate(DIL_PATTERNS):
        q_ref, k_ref, kp_ref, v_ref, vp_ref = ins[g]
        nb = ch // (QB * d)
        if d > 1:
            qf[...] = q_ref[...].astype(F32)
            kf[0:ch] = kp_ref[...].astype(F32)
            kf[ch:2 * ch] = k_ref[...].astype(F32)
            vf[0:ch] = vp_ref[...].astype(F32)
            vf[ch:2 * ch] = v_ref[...].astype(F32)

        def rows(r, n):
            return pl.ds(r + n * QB * d, QB, stride=d) if d > 1 else pl.ds(n * QB, QB)

        def window(src, prev, cur, r, n):
            if d > 1:
                return src[pl.ds(ch + r + (n - 1) * QB * d, 2 * QB, stride=d), :].astype(BF16)
            if n == 0:
                return jnp.concatenate([prev[ch - QB:ch, :], cur[0:QB, :]], axis=0)
            return cur[(n - 1) * QB:(n + 1) * QB, :]

        blocks = [(r, n) for r in range(d) for n in range(nb)]
        logits = []
        for r, n in blocks:
            qb = qf[rows(r, n), :].astype(BF16) if d > 1 else q_ref[rows(r, n), :]
            s = _dot_nt(qb, window(kf, kp_ref, k_ref, r, n)) * ATTN_SCALE
            logits.append(s + (bias_n0 if n == 0 else bias_rest))
        stats = []
        for s in logits:
            m = jnp.max(s, axis=-1, keepdims=True)
            p = jnp.exp(s - m)
            stats.append((m, p, jnp.sum(p, axis=-1, keepdims=True)))
        for (r, n), (m, p, den) in zip(blocks, stats):
            pv = _dot(p.astype(BF16), window(vf, vp_ref, v_ref, r, n))
            of[g, rows(r, n), :] = pv / den
            lf[g, rows(r, n), :] = jnp.broadcast_to(m + jnp.log(den), (QB, HEAD_DIM))

    a = [lf[g] for g in range(ng)]
    mx = functools.reduce(jnp.maximum, a)
    e = [jnp.exp(x - mx) for x in a]
    num = sum(e[g] * of[g] for g in range(ng))
    o_ref[...] = (num / sum(e)).astype(o_ref.dtype)


def _mixer_a(qk_arr, v_arr, *, S):
    G = A_HEADS_PER_GROUP
    ch = min(A_CHUNK, S)
    assert S % ch == 0 and all(w // d == QB and ch % (QB * d) == 0 for w, d in DIL_PATTERNS)
    in_specs, args = [], []
    for g in range(len(DIL_PATTERNS)):
        qc, kc, vc = g * G, A_HEADS + g * G, g * G
        in_specs += [pl.BlockSpec((ch, HEAD_DIM), lambda h, c, o=qc: (c, o + h)),
                     pl.BlockSpec((ch, HEAD_DIM), lambda h, c, o=kc: (c, o + h)),
                     pl.BlockSpec((ch, HEAD_DIM), lambda h, c, o=kc: (jnp.maximum(c - 1, 0), o + h)),
                     pl.BlockSpec((ch, HEAD_DIM), lambda h, c, o=vc: (c, o + h)),
                     pl.BlockSpec((ch, HEAD_DIM), lambda h, c, o=vc: (jnp.maximum(c - 1, 0), o + h))]
        args += [qk_arr, qk_arr, qk_arr, v_arr, v_arr]
    ng = len(DIL_PATTERNS)
    return pl.pallas_call(
        functools.partial(_mixer_a_body, ch=ch),
        grid=(G, S // ch),
        in_specs=in_specs,
        out_specs=pl.BlockSpec((ch, HEAD_DIM), lambda h, c: (c, h)),
        out_shape=jax.ShapeDtypeStruct((S, G * HEAD_DIM), BF16),
        scratch_shapes=[pltpu.VMEM((ch, HEAD_DIM), F32), pltpu.VMEM((2 * ch, HEAD_DIM), F32),
                        pltpu.VMEM((2 * ch, HEAD_DIM), F32), pltpu.VMEM((ng, ch, HEAD_DIM), F32),
                        pltpu.VMEM((ng, ch, HEAD_DIM), F32)],
        compiler_params=_params(("parallel", "parallel")),
        name="dilated_attn",
    )(*args)


def _softmax_step(t, m):
    m_new = jnp.maximum(m, jnp.max(t, axis=0, keepdims=True))
    return m_new, jnp.exp2(m - m_new), jnp.exp2(t - m_new).astype(BF16)


def _pv(vt_ref, tile0, rows, p):
    ones = jnp.ones((DEN_ROWS, VT), BF16)
    out = None
    for c in range(p.shape[0] // VT):
        v_aug = jnp.concatenate([vt_ref[tile0 + c, rows, :], ones], axis=0)
        part = _dot(v_aug, p[c * VT:(c + 1) * VT])
        out = part if out is None else out + part
    return out


def _retire_then_softmax(ts, ms, alphas, j, slots, vt_ref, head_rows, p_ref, acc_ref):
    read_slot, write_slot = slots
    tiles_per_key_tile = p_ref.shape[2] // VT
    j_prev = jnp.maximum(j - 1, 0)
    for h, rows in enumerate(head_rows):
        acc_ref[h] = alphas[h] * acc_ref[h] + _pv(vt_ref, j_prev * tiles_per_key_tile, rows, p_ref[read_slot, h])
    steps = [_softmax_step(ts[h], ms[h]) for h in range(len(head_rows))]
    for h, (_, _, p) in enumerate(steps):
        p_ref[write_slot, h] = p
    return tuple(s[0] for s in steps), tuple(s[1] for s in steps)


def _pipelined_sweep(n_plain, plain_stage, final_stage, vt_ref, head_rows, p_ref, acc_ref, width):
    n_heads = len(head_rows)
    p_ref[...] = jnp.zeros(p_ref.shape, p_ref.dtype)
    acc_ref[...] = jnp.zeros(acc_ref.shape, F32)
    carry = (tuple(jnp.full((1, width), M_INIT, F32) for _ in range(n_heads)),
             tuple(jnp.ones((1, width), F32) for _ in range(n_heads)))
    odd = lax.rem(n_plain, 2)
    carry = lax.cond(odd == 1, lambda c: plain_stage(0, c, (0, 1)), lambda c: c, carry)

    def pair(m, c):
        j = odd + 2 * m
        return plain_stage(j + 1, plain_stage(j, c, (1, 0)), (0, 1))

    carry = lax.fori_loop(0, n_plain // 2, pair, carry)
    _, alphas = final_stage(n_plain, carry, (1, 0))
    tiles_per_key_tile = p_ref.shape[2] // VT
    for h, rows in enumerate(head_rows):
        acc_ref[h] = alphas[h] * acc_ref[h] + _pv(vt_ref, n_plain * tiles_per_key_tile, rows, p_ref[0, h])


def _causal_bias(j, q0, tk, tq):
    kpos = j * tk + lax.broadcasted_iota(I32, (tk, tq), 0)
    qpos = q0 + lax.broadcasted_iota(I32, (tk, tq), 1)
    return jnp.where(kpos <= qpos, 0.0, NEG)


def _forget_cumsum_body(f_ref, b_ref, g_ref, *, rows_per_head):
    x = f_ref[...] + b_ref[...]
    x = jnp.minimum(x, 0.0) - jnp.log1p(jnp.exp(-jnp.abs(x)))
    lane = lax.broadcasted_iota(I32, x.shape, 1)
    s = 1
    while s < LANE:
        x = x + jnp.where(lane >= s, pltpu.roll(x, s, axis=1), 0.0)
        s *= 2
    tot = jnp.broadcast_to(x[:, LANE - 1:LANE], x.shape)
    row = lax.broadcasted_iota(I32, x.shape, 0) % rows_per_head
    t = tot
    s = 1
    while s < rows_per_head:
        t = t + jnp.where(row >= s, pltpu.roll(t, s, axis=0), 0.0)
        s *= 2
    c = x + (t - tot)
    gam = c * (-LOG2E)
    hi = gam.astype(BF16)
    r1 = gam - hi.astype(F32)
    mid = r1.astype(BF16)
    lo = (r1 - mid.astype(F32)).astype(BF16)
    g_ref[0] = hi
    g_ref[1] = mid
    g_ref[2] = lo


def _forget_decay(f_t, b_f):
    H, S = f_t.shape
    rph = S // LANE
    rows = H * rph
    b_col = jnp.repeat(b_f.astype(F32), rph).reshape(rows, 1)
    g = pl.pallas_call(
        functools.partial(_forget_cumsum_body, rows_per_head=rph),
        out_shape=jax.ShapeDtypeStruct((3, rows, LANE), BF16),
        name="forget_cumsum",
    )(f_t.reshape(rows, LANE), b_col)
    return g.reshape(3, H, S)


def _fox_body(q_ref, k_ref, vt_ref, o_ref, acc_ref, *, t, ka):
    i = pl.program_id(0)
    H = B_HEADS
    hd = [slice(h * HEAD_DIM, (h + 1) * HEAD_DIM) for h in range(H)]

    acc_ref[...] = jnp.zeros(acc_ref.shape, F32)

    def tile(j, ms, masked):
        rows = pl.ds(pl.multiple_of(j * t, t), t)
        ts = [_dot(k_ref[rows, h * ka:(h + 1) * ka], q_ref[h * ka:(h + 1) * ka, :]) for h in range(H)]
        bias = _causal_bias(j, i * t, t, t) if masked else None
        steps = [_softmax_step(ts[h] + bias if masked else ts[h], ms[h]) for h in range(H)]
        for h, (_, alpha, p) in enumerate(steps):
            acc_ref[h] = alpha * acc_ref[h] + _pv(vt_ref, j * (t // VT), hd[h], p)
        return tuple(s[0] for s in steps)

    ms = lax.fori_loop(0, i, lambda j, c: tile(j, c, False), tuple(jnp.full((1, t), M_INIT, F32) for _ in range(H)))
    tile(i, ms, True)
    for h in range(H):
        o_ref[h * HEAD_DIM:(h + 1) * HEAD_DIM, :] = (
            acc_ref[h, :HEAD_DIM, :] / acc_ref[h, HEAD_DIM:HEAD_DIM + 1, :]).astype(o_ref.dtype)


def _fox(q_aug_t, k_aug, v_t, v_blk, *, S):
    tq, tk = min(TQ, S), min(TK, S)
    assert tq == tk
    H = B_HEADS
    ka = k_aug.shape[1] // H
    W = H * HEAD_DIM
    return pl.pallas_call(
        functools.partial(_fox_body, t=tq, ka=ka),
        grid=(S // tq,),
        in_specs=[pl.BlockSpec((H * ka, tq), lambda i: (0, i)),
                  _resident((S, H * ka), lambda i: (0, 0)),
                  _resident((S // VT, W, VT), lambda i: (0, v_blk, 0))],
        out_specs=pl.BlockSpec((W, tq), lambda i: (0, i)),
        out_shape=jax.ShapeDtypeStruct((W, S), BF16),
        scratch_shapes=[pltpu.VMEM((H, PV_ROWS, tq), F32)],
        compiler_params=_params(("parallel",)),
        name="fox_attn",
    )(q_aug_t, k_aug, v_t)


def _dsa_body(qi_ref, kidx_ref, misc_ref, q_ref, k_ref, vt_ref, o_ref, keys_ref, hi_ref, acc_ref,
              *, tq, tk, ts, topk, w_row):
    i = pl.program_id(0)
    q0 = i * tq
    H = C_HEADS
    n_tiles = (q0 + tq + tk - 1) // tk
    n_full = q0 // tk
    ns_all = (q0 + tq) // ts
    ns_full = q0 // ts
    w = misc_ref[w_row:w_row + IDX_HEADS, :] * (IDX_DIM ** -0.5 * IDX_HEADS ** -0.5)

    def score_tile(j, masked):
        rows = pl.ds(pl.multiple_of(j * ts, ts), ts)
        kt = kidx_ref[rows, :]
        acc = jnp.zeros((ts, tq), F32)
        for h in range(IDX_HEADS):
            rel = _dot(kt, qi_ref[h * IDX_DIM:(h + 1) * IDX_DIM, :])
            acc = acc + jnp.maximum(rel, 0.0) * w[h:h + 1, :]
        s = jnp.where(jnp.abs(acc) < F32_MIN_NORMAL, 0.0, acc)
        if masked:
            kpos = j * ts + lax.broadcasted_iota(I32, (ts, tq), 0)
            qpos = q0 + lax.broadcasted_iota(I32, (ts, tq), 1)
            s = jnp.where(kpos <= qpos, s, NEG)
        bits = pltpu.bitcast(s, I32)
        keys_ref[rows, :] = bits ^ (lax.shift_right_arithmetic(bits, 31) & 0x7FFFFFFF)
        hi_ref[rows, :] = pltpu.bitcast(bits & HI16_MASK, F32).astype(BF16)

    def _s_full(j, c):
        score_tile(j, False)
        return c

    def _s_mask(j, c):
        score_tile(j, True)
        return c

    lax.fori_loop(0, ns_full, _s_full, 0)
    lax.fori_loop(ns_full, ns_all, _s_mask, 0)

    @pl.when(ns_all * ts < n_tiles * tk)
    def _():
        pad_rows = pl.ds(pl.multiple_of(ns_all * ts, ts), ts)
        keys_ref[pad_rows, :] = jnp.full((ts, tq), NEG_KEY, I32)
        hi_ref[pad_rows, :] = jnp.full((ts, tq), NEG_HI, BF16)

    kf = float(topk)
    hi_grp = 2 * 16

    def count_hi(cand16):
        cand16 = jnp.where((cand16 > 0) & (cand16 < BF16_MIN_NORMAL_BITS), BF16_MIN_NORMAL_BITS, cand16)
        b16 = jnp.where(cand16 >= 0, cand16, cand16 ^ 0x7FFF) & 0xFFFF
        cand = pltpu.bitcast(lax.shift_left(b16, 16), F32).astype(BF16)

        def body(c, acc):
            t = hi_ref[pl.ds(pl.multiple_of(c * tk, tk), tk), :]
            ind = jnp.where(t >= cand, jnp.ones((), BF16), jnp.zeros((), BF16))
            parts = [ind[r * hi_grp:(r + 1) * hi_grp] for r in range(tk // hi_grp)]
            while len(parts) > 1:
                parts = [parts[k] + parts[k + 1] for k in range(0, len(parts), 2)]
            return acc + parts[0].astype(F32)
        acc = lax.fori_loop(0, n_tiles, body, jnp.zeros((hi_grp, tq), F32))
        return jnp.sum(acc, axis=0, keepdims=True)

    lo16 = jnp.where(count_hi(jnp.zeros((1, tq), I32)) >= kf, 0, -(2 ** 15)).astype(I32)

    def hi_step(b, lo16):
        cand = lo16 + lax.shift_left(jnp.int32(1), 14 - b)
        return jnp.where(count_hi(cand) >= kf, cand, lo16)

    lo16 = lax.fori_loop(0, 15, hi_step, lo16)
    n_acc = 4

    def count_ge(cand):
        def body(c, acc):
            t = keys_ref[pl.ds(pl.multiple_of(c * tk, tk), tk), :]
            ind = jnp.where(t >= cand, 1.0, 0.0)
            part = jnp.sum(ind.reshape(tk // (n_acc * SUBLANE), n_acc * SUBLANE, tq), axis=0)
            return acc + part
        acc = lax.fori_loop(0, n_tiles, body, jnp.zeros((n_acc * SUBLANE, tq), F32))
        return jnp.sum(acc, axis=0, keepdims=True)

    def bit_step(b, lo):
        cand = lo + lax.shift_left(jnp.int32(1), 15 - b)
        return jnp.where(count_ge(cand) >= kf, cand, lo)

    thr = lax.fori_loop(0, 16, bit_step, lax.shift_left(lo16, 16))

    n_ge = count_ge(thr)
    tied = jnp.where((n_ge > kf) & (thr != NEG_KEY), 1.0, 0.0)
    has_ties = jnp.max(tied) > 0.0

    @pl.when(has_ties)
    def _():
        need = kf - count_ge(thr + 1)
        tri = (lax.broadcasted_iota(I32, (tk, tk), 0) >= lax.broadcasted_iota(I32, (tk, tk), 1)).astype(BF16)

        def demote(j, seen):
            rows = pl.ds(pl.multiple_of(j * tk, tk), tk)
            key = keys_ref[rows, :]
            eq = jnp.where(key == thr, 1.0, 0.0)
            rank = _dot(tri, eq.astype(BF16)) + seen
            keys_ref[rows, :] = jnp.where((eq > 0.5) & (rank > need), INT_MIN, key)
            return seen + jnp.sum(eq, axis=0, keepdims=True)

        lax.fori_loop(0, n_tiles, demote, jnp.zeros((1, tq), F32))

    def to_bias(j, masked):
        rows = pl.ds(pl.multiple_of(j * tk, tk), tk)
        sel = keys_ref[rows, :] >= thr
        if masked:
            kpos = j * tk + lax.broadcasted_iota(I32, (tk, tq), 0)
            qpos = q0 + lax.broadcasted_iota(I32, (tk, tq), 1)
            sel = sel & (kpos <= qpos)
        keys_ref[rows, :] = pltpu.bitcast(jnp.where(sel, 0.0, NEG), I32)

    def _b_full(j, c):
        to_bias(j, False)
        return c

    def _b_mask(j, c):
        to_bias(j, True)
        return c

    lax.fori_loop(0, n_full, _b_full, 0)
    lax.fori_loop(n_full, n_tiles, _b_mask, 0)

    hd = [slice(h * HEAD_DIM, (h + 1) * HEAD_DIM) for h in range(H)]

    acc_ref[...] = jnp.zeros(acc_ref.shape, F32)

    def tile(j, ms):
        rows = pl.ds(pl.multiple_of(j * tk, tk), tk)
        ts = [_dot(k_ref[rows, hd[h]], q_ref[hd[h], :]) for h in range(H)]
        bias = pltpu.bitcast(keys_ref[rows, :], F32)
        steps = [_softmax_step(ts[h] + bias, ms[h]) for h in range(H)]
        for h, (_, alpha, p) in enumerate(steps):
            acc_ref[h] = alpha * acc_ref[h] + _pv(vt_ref, j * (tk // VT), hd[h], p)
        return tuple(s[0] for s in steps)

    lax.fori_loop(0, n_tiles, tile, tuple(jnp.full((1, tq), M_INIT, F32) for _ in range(H)))
    for h in range(H):
        den = jnp.maximum(acc_ref[h, HEAD_DIM:HEAD_DIM + 1, :], TINY)
        o_ref[h * HEAD_DIM:(h + 1) * HEAD_DIM, :] = (acc_ref[h, :HEAD_DIM, :] / den).astype(o_ref.dtype)


def _dsa(qi_t, kidx, misc_t, w_row, q_t, q_blk, k_arr, k_blk, v_t, v_blk, *, S):
    tq, tk = min(TQ, S), min(TK, S)
    ts = tk
    topk = min(DSA_TOPK, S // 4)
    W = C_HEADS * HEAD_DIM
    assert S % tk == 0 and tk % ts == 0 and tq % ts == 0
    kern = functools.partial(_dsa_body, tq=tq, tk=tk, ts=ts, topk=topk, w_row=w_row)
    return pl.pallas_call(
        kern,
        grid=(S // tq,),
        in_specs=[pl.BlockSpec((IDX_HEADS * IDX_DIM, tq), lambda i: (0, i)),
                  _resident((S, IDX_DIM), lambda i: (0, 0)),
                  pl.BlockSpec((LANE, tq), lambda i: (0, i)),
                  pl.BlockSpec((W, tq), lambda i: (q_blk, i)),
                  _resident((S, W), lambda i: (0, k_blk)),
                  _resident((S // VT, W, VT), lambda i: (0, v_blk, 0))],
        out_specs=pl.BlockSpec((W, tq), lambda i: (0, i)),
        out_shape=jax.ShapeDtypeStruct((W, S), BF16),
        scratch_shapes=[pltpu.VMEM((S, tq), I32), pltpu.VMEM((S, tq), BF16),
                        pltpu.VMEM((C_HEADS, PV_ROWS, tq), F32)],
        compiler_params=_params(("arbitrary",)),
        name="dsa_attn",
    )(qi_t, kidx, misc_t, q_t, k_arr, v_t)


def _compress_body(x_ref, pea_ref, peb_ref, w1a_ref, w1b_ref, w2_ref, o_ref, ot_ref):
    x = x_ref[...].astype(F32)
    a = _dot((x + pea_ref[...]).astype(BF16), w1a_ref[...].astype(BF16))
    b = _dot((x + peb_ref[...]).astype(BF16), w1b_ref[...].astype(BF16))
    n = a.shape[0]
    hid = a + pltpu.roll(b, n - 1, axis=0)
    out = _dot(jax.nn.gelu(hid).astype(BF16), w2_ref[...].astype(BF16))
    o_ref[...] = out.astype(o_ref.dtype)
    ot_ref[...] = out.T.astype(ot_ref.dtype)


def _compress(x2, pe, w1, w2):
    _, S, Dh = x2.shape
    n = S // CMP_STRIDE
    half = CMP_STRIDE * Dh
    xr = x2.reshape(2, n, half)
    pe_a = pe[:, :CMP_STRIDE].reshape(2, 1, half)
    pe_b = pe[:, CMP_STRIDE:].reshape(2, 1, half)
    return pl.pallas_call(
        _compress_body,
        grid=(2,),
        in_specs=[pl.BlockSpec((None, n, half), lambda g: (g, 0, 0)),
                  pl.BlockSpec((None, 1, half), lambda g: (g, 0, 0)),
                  pl.BlockSpec((None, 1, half), lambda g: (g, 0, 0)),
                  pl.BlockSpec((None, half, CMP_HIDDEN), lambda g: (g, 0, 0)),
                  pl.BlockSpec((None, half, CMP_HIDDEN), lambda g: (g, 1, 0)),
                  pl.BlockSpec((None, CMP_HIDDEN, Dh), lambda g: (g, 0, 0))],
        out_specs=[pl.BlockSpec((None, n, Dh), lambda g: (g, 0, 0)),
                   pl.BlockSpec((None, Dh, n), lambda g: (g, 0, 0))],
        out_shape=[jax.ShapeDtypeStruct((2, n, Dh), BF16), jax.ShapeDtypeStruct((2, Dh, n), BF16)],
        compiler_params=_params(("parallel",)),
        name="nsa_compress",
    )(xr, pe_a, pe_b, w1, w1, w2)


def _nsa_body(q_ref, kc_ref, vct_ref, ks_ref, kw_ref, vst_ref, vwt_ref, misc_ref, ov_ref, o_ref, acc_ref, p_ref,
              qs_ref,
              *, tq, tk, n_slc, n_sel, ncp, gate_row):
    i = pl.program_id(0)
    q0 = i * tq
    H = D_HEADS
    R = H * tq
    blk_shift = int(math.log2(SLC_BLOCK))
    qs = jnp.concatenate([q_ref[h * HEAD_DIM:(h + 1) * HEAD_DIM, :] for h in range(H)], axis=1)

    def qpos_of(shape):
        return q0 + (lax.broadcasted_iota(I32, shape, 1) & (tq - 1))

    lc = _dot(kc_ref[...], qs)
    cend = lax.broadcasted_iota(I32, (ncp, R), 0) * CMP_STRIDE + (CMP_LEN - 1)
    mc = cend <= qpos_of((ncp, R))
    lc = jnp.where(mc, lc, NEG)
    mx = jnp.max(lc, axis=0, keepdims=True)
    pc = jnp.where(mc, jnp.exp2(lc - mx), 0.0)
    pc = pc * (1.0 / jnp.maximum(jnp.sum(pc, axis=0, keepdims=True), TINY))
    o_cmp = _dot(vct_ref[...], pc.astype(BF16))

    psum = pc[:, 0:tq]
    for h in range(1, H):
        psum = psum + pc[:, h * tq:(h + 1) * tq]
    p_hi = psum.astype(BF16)
    p_lo = (psum - p_hi.astype(F32)).astype(BF16)
    ov = ov_ref[...]
    imp = _dot(ov, p_hi) + _dot(ov, p_lo)
    jblk = lax.broadcasted_iota(I32, (n_slc, tq), 0)
    jblk_f = jblk.astype(F32)
    qpos_l = q0 + lax.broadcasted_iota(I32, (n_slc, tq), 1)
    cur = lax.shift_right_logical(qpos_l, blk_shift)
    forced = (jblk == 0) | (jblk == cur) | (jblk == cur - 1)
    val = jnp.where(forced, FORCE, jnp.where(jblk * SLC_BLOCK <= qpos_l, imp, NEG))
    sel_t = jnp.zeros((n_slc, tq), F32)
    for _ in range(n_sel):
        mxv = jnp.max(val, axis=0, keepdims=True)
        first = jnp.min(jnp.where(val == mxv, jblk_f, float(n_slc)), axis=0, keepdims=True)
        hit = jblk_f == first
        sel_t = jnp.where(hit, 1.0, sel_t)
        val = jnp.where(hit, -jnp.inf, val)
    sel_b = ((sel_t - 1.0) * (-NEG)).astype(BF16)

    bpt = tk // SLC_BLOCK
    n_tiles = (q0 + tq + tk - 1) // tk
    n_full = q0 // tk
    row_blk = lax.shift_right_logical(lax.broadcasted_iota(I32, (tk, n_slc), 0), blk_shift)
    col_blk = lax.broadcasted_iota(I32, (tk, n_slc), 1)

    n_ch = NSA_CHAINS
    cw = R // n_ch
    chains = [slice(c * cw, (c + 1) * cw) for c in range(n_ch)]
    qs_ref[...] = qs

    all_rows = [slice(None)] * n_ch

    def slc_stage(j, carry, slots, masked):
        ms, alphas = carry
        rows = pl.ds(pl.multiple_of(j * tk, tk), tk)
        ks = ks_ref[rows, :]
        ts = [_dot(ks, qs_ref[:, c]) for c in chains]
        expand = jnp.where(row_blk == col_blk - j * bpt, 1.0, 0.0).astype(BF16)
        bias = _dot(expand, sel_b)
        if masked:
            kpos = j * tk + lax.broadcasted_iota(I32, (tk, tq), 0)
            qpos = q0 + lax.broadcasted_iota(I32, (tk, tq), 1)
            bias = jnp.where(kpos <= qpos, bias, NEG)
        bias = jnp.concatenate([bias] * (H // n_ch), axis=1)
        ts = [x + bias for x in ts]
        return _retire_then_softmax(ts, ms, alphas, j, slots, vst_ref, all_rows, p_ref, acc_ref)

    _pipelined_sweep(n_full, functools.partial(slc_stage, masked=False), functools.partial(slc_stage, masked=True),
                     vst_ref, all_rows, p_ref, acc_ref, cw)
    o_slc = jnp.concatenate(
        [acc_ref[c, :HEAD_DIM, :] * (1.0 / acc_ref[c, HEAD_DIM:HEAD_DIM + 1, :]) for c in range(n_ch)], axis=1)

    wlen = WIN + tq
    start = pl.multiple_of(jnp.maximum(q0 - WIN, 0), tq)
    lw = _dot(kw_ref[pl.ds(start, wlen), :], qs)
    dist = qpos_of((wlen, R)) - (start + lax.broadcasted_iota(I32, (wlen, R), 0))
    lw = jnp.where((dist >= 0) & (dist < WIN), lw, NEG)
    mxw = jnp.max(lw, axis=0, keepdims=True)
    pw = jnp.exp2(lw - mxw).astype(BF16)
    o_win = _pv(vwt_ref, start // VT, slice(None), pw)
    o_win = o_win[:HEAD_DIM] * (1.0 / o_win[HEAD_DIM:HEAD_DIM + 1])

    gates = jax.nn.sigmoid(misc_ref[gate_row:gate_row + 3 * H, :])
    for h in range(H):
        c = slice(h * tq, (h + 1) * tq)
        out = (gates[3 * h:3 * h + 1] * o_cmp[:, c] + gates[3 * h + 1:3 * h + 2] * o_slc[:, c]
               + gates[3 * h + 2:3 * h + 3] * o_win[:, c])
        o_ref[h * HEAD_DIM:(h + 1) * HEAD_DIM, :] = out.astype(o_ref.dtype)


def _nsa(q_t, q_blk, kc, vct, k_arr, ks_blk, kw_blk, v_t, vs_blk, vw_blk, misc_t, gate_row, *, S):
    tq, tk = min(TQ_NSA, S), min(TK, S)
    assert S >= WIN + tq and tq & (tq - 1) == 0 and tq == VT and WIN % VT == 0
    n_slc = S // SLC_BLOCK
    n_sel = min(SLC_COUNT, n_slc)
    ncp = S // CMP_STRIDE
    W = D_HEADS * HEAD_DIM
    ci = np.arange(ncp)[None, :]
    sj = np.arange(n_slc)[:, None]
    ov = ((ci * CMP_STRIDE < (sj + 1) * SLC_BLOCK) & (ci * CMP_STRIDE + CMP_LEN > sj * SLC_BLOCK)
          & (ci < ncp - 1))
    ov = jnp.asarray(ov, BF16)
    kern = functools.partial(_nsa_body, tq=tq, tk=tk, n_slc=n_slc, n_sel=n_sel, ncp=ncp, gate_row=gate_row)
    return pl.pallas_call(
        kern,
        grid=(S // tq,),
        in_specs=[pl.BlockSpec((W, tq), lambda i: (q_blk, i)),
                  _resident((ncp, HEAD_DIM), lambda i: (0, 0)),
                  _resident((HEAD_DIM, ncp), lambda i: (0, 0)),
                  _resident((S, HEAD_DIM), lambda i: (0, ks_blk)),
                  _resident((S, HEAD_DIM), lambda i: (0, kw_blk)),
                  _resident((S // VT, HEAD_DIM, VT), lambda i: (0, vs_blk, 0)),
                  _resident((S // VT, HEAD_DIM, VT), lambda i: (0, vw_blk, 0)),
                  pl.BlockSpec((LANE, tq), lambda i: (0, i)),
                  _resident((n_slc, ncp), lambda i: (0, 0))],
        out_specs=pl.BlockSpec((W, tq), lambda i: (0, i)),
        out_shape=jax.ShapeDtypeStruct((W, S), BF16),
        scratch_shapes=[pltpu.VMEM((NSA_CHAINS, PV_ROWS, D_HEADS * tq // NSA_CHAINS), F32),
                        pltpu.VMEM((2, NSA_CHAINS, tk, D_HEADS * tq // NSA_CHAINS), BF16),
                        pltpu.VMEM((HEAD_DIM, D_HEADS * tq), BF16)],
        compiler_params=_params(("parallel",)),
        name="nsa_attn",
    )(q_t, kc, vct, k_arr, k_arr, v_t, v_t, misc_t, ov)


def _rope_tables(S):
    f32 = np.float32
    pos = np.arange(S, dtype=f32)[:, None]

    def cos_sin(half):
        inv_freq = np.power(f32(ROPE_THETA), -np.arange(half, dtype=f32) / f32(half), dtype=f32)
        ang = pos * inv_freq[None, :]
        return np.cos(ang, dtype=f32), np.sin(ang, dtype=f32)

    cos, sin = cos_sin(HEAD_DIM // 2)
    t128 = (np.concatenate([cos, cos], 1), np.concatenate([-sin, sin], 1))
    cos, sin = cos_sin(IDX_DIM // 2)
    z = np.zeros_like(sin)
    t64 = (np.concatenate([cos] * 4, 1), np.concatenate([-sin, z, -sin, z], 1),
           np.concatenate([z, sin, z, sin], 1))
    return tuple(jnp.asarray(t) for t in t128), tuple(jnp.asarray(t) for t in t64)


def _split_w_in(w_in):
    offs = np.cumsum([0,
                      A_HEADS * HEAD_DIM, A_HEADS * HEAD_DIM, A_HEADS * HEAD_DIM,
                      B_HEADS * HEAD_DIM, B_HEADS * HEAD_DIM, B_HEADS * HEAD_DIM, B_HEADS,
                      C_HEADS * HEAD_DIM, C_HEADS * HEAD_DIM, C_HEADS * HEAD_DIM,
                      IDX_HEADS * IDX_DIM, IDX_DIM, IDX_HEADS,
                      D_HEADS * HEAD_DIM] + [HEAD_DIM] * 6 + [3 * D_HEADS])
    names = ["aq", "ak", "av", "bq", "bk", "bv", "bf", "cq", "ck", "cv", "cqi", "cki", "cwi",
             "dq", "dkc", "dvc", "dks", "dvs", "dkw", "dvw", "dg"]
    col = {n: w_in[:, offs[k]:offs[k + 1]] for k, n in enumerate(names)}
    D = w_in.shape[0]

    def pack(ns, pad=0):
        parts = [col[n] for n in ns]
        if pad:
            parts.append(jnp.zeros((D, pad), w_in.dtype))
        return jnp.concatenate(parts, axis=1).astype(BF16)

    n_misc = B_HEADS + IDX_HEADS + 3 * D_HEADS
    return dict(
        a_qk=pack(["aq", "ak"]),
        a_v=pack(["av"]),
        b_q=pack(["bq"]),
        b_k=pack(["bk", "dvc"]),
        cd_k=pack(["ck", "dkc", "dks", "dkw"]),
        cd_q=pack(["cq", "dq"]),
        idx=pack(["cqi", "cki"], pad=LANE - IDX_DIM),
        v_t=pack(["bv", "cv", "dvs", "dvw"]),
        misc=pack(["bf", "cwi", "dg"], pad=LANE - n_misc),
    )


def _pick(n, cands):
    for c in cands:
        if n % c == 0:
            return c
    return n


def _layer(x, p, stacked, layer, tables):
    S, D = x.shape
    (cos128, sin128), (cos64, sin64a, sin64b) = tables
    tm = _pick(S, (1024, 512, 256))
    h = _rmsnorm(x, p["norm_mix"], BF16)
    w = _split_w_in(p["w_in"])

    def proj(name, mode="plain", extras=(), out_dtype=BF16, layout="rows", scale=None):
        n = w[name].shape[1]
        tn = _pick(n, (1280, 1152, 1024, 896, 768, 640, 512, 384, 256, 128))
        return _mm(h, w[name], tm=tm, tn=tn, out_dtype=out_dtype, mode=mode, extras=extras, layout=layout,
                   name="in_" + name, scale=scale)

    a_qk = proj("a_qk", "rope128", (cos128, sin128))
    a_v = proj("a_v")
    b_k = proj("b_k")
    b_q_t = proj("b_q", layout="cols", scale=EXP_C)
    cd_k = proj("cd_k", "rope128", (cos128, sin128))
    cd_q_t = proj("cd_q", "rope128", (cos128, sin128), layout="cols", scale=EXP_C)
    idx_t = proj("idx", "rope64", (cos64, sin64a, sin64b), layout="cols")
    v_t = proj("v_t", layout="coltiles")
    misc_t = proj("misc", out_dtype=F32, layout="cols")

    o_a = _mixer_a(a_qk, a_v, S=S)

    BW = B_HEADS * HEAD_DIM
    gam = _forget_decay(misc_t[:B_HEADS], p["b_f"])
    pad = HEAD_DIM - 3
    bq_t = b_q_t.reshape(B_HEADS, HEAD_DIM, S)
    bk = b_k[:, :BW].reshape(S, B_HEADS, HEAD_DIM)
    q_aug_t = jnp.concatenate([bq_t, jnp.ones((B_HEADS, 3, S), BF16), jnp.zeros((B_HEADS, pad, S), BF16)], axis=1)
    k_aug = jnp.concatenate([bk, gam.transpose(2, 1, 0), jnp.zeros((S, B_HEADS, pad), BF16)], axis=2)
    o_b_t = _fox(q_aug_t.reshape(-1, S), k_aug.reshape(S, -1), v_t, 0, S=S)

    n_qi = IDX_HEADS * IDX_DIM
    kidx = idx_t[n_qi:n_qi + IDX_DIM].T
    o_c_t = _dsa(idx_t, kidx, misc_t, B_HEADS, cd_q_t, 0, cd_k, 0, v_t, 1, S=S)

    cw = C_HEADS * HEAD_DIM
    nb = cw // HEAD_DIM
    kv_c = jnp.stack([cd_k[:, cw:cw + HEAD_DIM], b_k[:, BW:BW + HEAD_DIM]])
    kv_cmp, kv_cmp_t = _compress(kv_c, p["cmp_pe"], p["cmp_w1"], p["cmp_w2"])
    o_d_t = _nsa(cd_q_t, 1, kv_cmp[0], kv_cmp_t[1], cd_k, nb + 1, nb + 2, v_t, 2 * nb, 2 * nb + 1,
                 misc_t, B_HEADS + IDX_HEADS, S=S)

    tn = _pick(D, (512, 256, 128))
    nl = stacked["b_gate"].shape[0]
    bg = stacked["b_gate"].reshape(nl, N_BRANCH, 1, D)
    merged = _gate_merge(h, stacked["w_gate_bf16"], bg, (o_a, o_b_t, o_c_t, o_d_t), stacked["w_branch"], layer,
                         tm=tm, tn=tn)
    x = _mm(merged, stacked["w_out"], tm=tm, tn=tn, out_dtype=F32, mode="resid", extras=(x,), name="out_proj",
            layer=layer)

    h2 = _rmsnorm(x, p["norm_ffn"], BF16)
    Hf = stacked["w_ffn_in"].shape[-1] // 2
    act = _mm_swiglu(h2, stacked["w_ffn_in"], layer, tm=tm, tn=_pick(Hf, (512, 256, 128)))
    x = _mm(act, stacked["w_ffn_out"], tm=_pick(S, (512, 256)), tn=tn, out_dtype=F32, mode="resid",
            extras=(x,), name="ffn_out", layer=layer)
    return x


def kernel(x, norm_mix, w_in, w_gate, b_gate, b_f, cmp_pe, cmp_w1, cmp_w2, w_branch, w_out, norm_ffn,
           w_ffn_in, w_ffn_out, norm_final):
    B, S, D = x.shape
    tables = _rope_tables(S)
    nl = w_gate.shape[0]
    w_gate_bf16 = lax.optimization_barrier(w_gate.astype(BF16)).reshape(nl, D, N_BRANCH * D)
    stacked = dict(w_gate_bf16=w_gate_bf16, b_gate=b_gate, w_branch=w_branch, w_out=w_out, w_ffn_in=w_ffn_in,
                   w_ffn_out=w_ffn_out)
    outs = []
    for b in range(B):
        xb = x[b]
        for l in range(norm_mix.shape[0]):
            p = dict(norm_mix=norm_mix[l], w_in=w_in[l], b_f=b_f[l], cmp_pe=cmp_pe[l], cmp_w1=cmp_w1[l],
                     cmp_w2=cmp_w2[l], norm_ffn=norm_ffn[l])
            xb = _layer(xb, p, stacked, l, tables)
        outs.append(_rmsnorm(xb, norm_final, x.dtype))
    return jnp.stack(outs)
```

```python
import functools
import math

import jax
import jax.numpy as jnp
import numpy as np
from jax import lax
from jax.experimental import pallas as pl
from jax.experimental.pallas import tpu as pltpu

F32 = jnp.float32
BF16 = jnp.bfloat16
I32 = jnp.int32
I16 = jnp.int16

HEAD_DIM = 128
ROPE_THETA = 10000.0
ATTN_SCALE = HEAD_DIM ** -0.5
QB = 128
NEG = -1e30
TINY = 1e-30
DIL_PATTERNS = ((128, 1), (512, 4), (2048, 16))
A_HEADS_PER_GROUP = 4
A_HEADS = A_HEADS_PER_GROUP * len(DIL_PATTERNS)
B_HEADS = 4
C_HEADS = 4
IDX_HEADS = 16
IDX_DIM = 64
DSA_TOPK = 256
D_HEADS = 4
CMP_LEN = 32
CMP_STRIDE = 16
CMP_HIDDEN = 256
SLC_BLOCK = 64
SLC_COUNT = 16
WIN = 512
FORCE = 1e9
N_BRANCH = 4
BRANCH_WIDTH = 4 * HEAD_DIM

LANE = 128
SUBLANE = 8
VMEM_LIMIT = 56 * 1024 * 1024
VT = 256
TQ = 512
TQ_NSA = 256
NSA_CHAINS = 2
TK = 512

LOG2E = math.log2(math.e)
EXP_C = ATTN_SCALE * LOG2E
M_INIT = 0.5 * NEG
DEN_ROWS = 16
PV_ROWS = HEAD_DIM + DEN_ROWS

INT_MIN = -2 ** 31
_NEG_BITS = int(np.array(NEG, np.float32).view(np.int32))
NEG_KEY = _NEG_BITS ^ 0x7FFFFFFF


def _params(sem):
    return pltpu.CompilerParams(dimension_semantics=sem, vmem_limit_bytes=VMEM_LIMIT)


def _resident(shape, index_map):
    return pl.BlockSpec(shape, index_map, pipeline_mode=pl.Buffered(1))


def _dot(a, b):
    return jnp.dot(a, b, preferred_element_type=F32)


def _dot_nt(a, b):
    return lax.dot_general(a, b, (((1,), (1,)), ((), ())), preferred_element_type=F32)


def _dot_tn(a, b):
    return lax.dot_general(a, b, (((0,), (0,)), ((), ())), preferred_element_type=F32)


def _rmsnorm_body(x_ref, g_ref, o_ref):
    x = x_ref[...]
    ms = jnp.mean(x * x, axis=-1, keepdims=True)
    o_ref[...] = (x * lax.rsqrt(ms + 1e-6) * g_ref[...]).astype(o_ref.dtype)


def _rmsnorm(x, g, out_dtype, tm=1024):
    S, D = x.shape
    tm = min(tm, S)
    return pl.pallas_call(
        _rmsnorm_body,
        grid=(S // tm,),
        in_specs=[pl.BlockSpec((tm, D), lambda i: (i, 0)),
                  pl.BlockSpec((1, D), lambda i: (0, 0))],
        out_specs=pl.BlockSpec((tm, D), lambda i: (i, 0)),
        out_shape=jax.ShapeDtypeStruct((S, D), out_dtype),
        compiler_params=_params(("parallel",)),
        name="rmsnorm",
    )(x, g.reshape(1, D))


def _rope128(acc, cos, sin):
    parts = []
    for c in range(acc.shape[1] // LANE):
        blk = acc[:, c * LANE:(c + 1) * LANE]
        parts.append(blk * cos + pltpu.roll(blk, LANE // 2, axis=1) * sin)
    return parts[0] if len(parts) == 1 else jnp.concatenate(parts, axis=1)


def _rope64(acc, cos, sin_a, sin_b):
    parts = []
    for c in range(acc.shape[1] // LANE):
        blk = acc[:, c * LANE:(c + 1) * LANE]
        parts.append(blk * cos + pltpu.roll(blk, LANE - IDX_DIM // 2, axis=1) * sin_a
                     + pltpu.roll(blk, IDX_DIM // 2, axis=1) * sin_b)
    return parts[0] if len(parts) == 1 else jnp.concatenate(parts, axis=1)


def _cast_once(src_ref, dst_ref):
    @pl.when(pl.program_id(1) == 0)
    def _():
        dst_ref[...] = src_ref[...].astype(BF16)


def _mm_body(*refs, mode, layout, cast_w, scale):
    a_ref, b_ref = refs[0], refs[1]
    if cast_w:
        o_ref, wb_ref = refs[-2], refs[-1]
        _cast_once(b_ref, wb_ref)
        b_ref = wb_ref
    else:
        o_ref = refs[-1]
    acc = _dot(a_ref[...], b_ref[...])
    if mode == "rope128":
        acc = _rope128(acc, refs[2][...], refs[3][...])
    elif mode == "rope64":
        acc = _rope64(acc, refs[2][...], refs[3][...], refs[4][...])
    elif mode == "resid":
        acc = refs[2][...] + acc
    if scale is not None:
        acc = acc * scale
    if layout == "rows":
        o_ref[...] = acc.astype(o_ref.dtype)
    elif layout == "cols":
        o_ref[...] = acc.T.astype(o_ref.dtype)
    else:
        acc_t = acc.T
        for c in range(acc_t.shape[1] // VT):
            o_ref[c] = acc_t[:, c * VT:(c + 1) * VT].astype(o_ref.dtype)


def _mm(a, b, *, tm, tn, out_dtype, mode="plain", extras=(), layout="rows", name="mm", layer=None, scale=None):
    M, K = a.shape
    N = b.shape[-1]
    assert M % tm == 0 and N % tn == 0, (M, N, tm, tn)
    cast_w = layer is not None
    if cast_w:
        b_spec = pl.BlockSpec((None, K, tn), lambda j, i: (layer, 0, j))
    else:
        b_spec = pl.BlockSpec((K, tn), lambda j, i: (0, j))
    in_specs = [pl.BlockSpec((tm, K), lambda j, i: (i, 0)), b_spec]
    if mode in ("rope128", "rope64"):
        in_specs += [pl.BlockSpec((tm, LANE), lambda j, i: (i, 0)) for _ in extras]
    elif mode == "resid":
        in_specs += [pl.BlockSpec((tm, tn), lambda j, i: (i, j))]
    if layout == "rows":
        out_spec = pl.BlockSpec((tm, tn), lambda j, i: (i, j))
        out_shape = (M, N)
    elif layout == "cols":
        out_spec = pl.BlockSpec((tn, tm), lambda j, i: (j, i))
        out_shape = (N, M)
    else:
        assert tm % VT == 0
        out_spec = pl.BlockSpec((tm // VT, tn, VT), lambda j, i: (i, j, 0))
        out_shape = (M // VT, N, VT)
    return pl.pallas_call(
        functools.partial(_mm_body, mode=mode, layout=layout, cast_w=cast_w, scale=scale),
        grid=(N // tn, M // tm),
        in_specs=in_specs,
        out_specs=out_spec,
        out_shape=jax.ShapeDtypeStruct(out_shape, out_dtype),
        scratch_shapes=[pltpu.VMEM((K, tn), BF16)] if cast_w else [],
        compiler_params=_params(("parallel", "arbitrary" if cast_w else "parallel")),
        name=name,
    )(a, b, *extras)


def _swiglu_body(a_ref, bg_ref, bu_ref, o_ref, wg_ref, wu_ref):
    _cast_once(bg_ref, wg_ref)
    _cast_once(bu_ref, wu_ref)
    a = a_ref[...]
    g = _dot(a, wg_ref[...])
    u = _dot(a, wu_ref[...])
    o_ref[...] = (g * jax.nn.sigmoid(g) * u).astype(o_ref.dtype)


def _mm_swiglu(a, w, layer, *, tm, tn):
    M, K = a.shape
    H = w.shape[-1] // 2
    nj = H // tn
    assert H % tn == 0 and M % tm == 0
    return pl.pallas_call(
        _swiglu_body,
        grid=(nj, M // tm),
        in_specs=[pl.BlockSpec((tm, K), lambda j, i: (i, 0)),
                  pl.BlockSpec((None, K, tn), lambda j, i: (layer, 0, j)),
                  pl.BlockSpec((None, K, tn), lambda j, i: (layer, 0, j + nj))],
        out_specs=pl.BlockSpec((tm, tn), lambda j, i: (i, j)),
        out_shape=jax.ShapeDtypeStruct((M, H), BF16),
        scratch_shapes=[pltpu.VMEM((K, tn), BF16), pltpu.VMEM((K, tn), BF16)],
        compiler_params=_params(("parallel", "arbitrary")),
        name="ffn_in_swiglu",
    )(a, w, w)


def _gate_merge_body(h_ref, wg0, wg1, wg2, wg3, bg_ref, br0, br1, br2, br3, wb_ref, o_ref, wbs_ref):
    _cast_once(wb_ref, wbs_ref)
    h = h_ref[...]
    acc = None
    for n, (wg, br) in enumerate(((wg0, br0), (wg1, br1), (wg2, br2), (wg3, br3))):
        gl = _dot(h, wg[...]) + bg_ref[n]
        y = _dot(br[...], wbs_ref[n]) if n == 0 else _dot_tn(br[...], wbs_ref[n])
        t = jax.nn.sigmoid(gl) * y
        acc = t if acc is None else acc + t
    o_ref[...] = acc.astype(o_ref.dtype)


def _gate_merge(h, wg, bg, branches, wb, layer, *, tm, tn):
    S, D = h.shape
    nj = D // tn
    wg_specs = [pl.BlockSpec((None, D, tn), (lambda j, i, n=n: (layer, 0, n * nj + j))) for n in range(N_BRANCH)]
    br_specs = [pl.BlockSpec((tm, BRANCH_WIDTH), lambda j, i: (i, 0))]
    br_specs += [pl.BlockSpec((BRANCH_WIDTH, tm), lambda j, i: (0, i)) for _ in range(N_BRANCH - 1)]
    return pl.pallas_call(
        _gate_merge_body,
        grid=(nj, S // tm),
        in_specs=[pl.BlockSpec((tm, D), lambda j, i: (i, 0))] + wg_specs
        + [pl.BlockSpec((None, N_BRANCH, 1, tn), lambda j, i: (layer, 0, 0, j))] + br_specs
        + [pl.BlockSpec((None, N_BRANCH, BRANCH_WIDTH, tn), lambda j, i: (layer, 0, 0, j))],
        out_specs=pl.BlockSpec((tm, tn), lambda j, i: (i, j)),
        out_shape=jax.ShapeDtypeStruct((S, D), BF16),
        scratch_shapes=[pltpu.VMEM((N_BRANCH, BRANCH_WIDTH, tn), BF16)],
        compiler_params=_params(("parallel", "arbitrary")),
        name="gate_merge",
    )(h, wg, wg, wg, wg, bg, *branches, wb)


A_CHUNK = QB * max(d for _, d in DIL_PATTERNS)


def _mixer_a_body(*refs, ch):
    ng = len(DIL_PATTERNS)
    ins = [refs[5 * g:5 * g + 5] for g in range(ng)]
    o_ref = refs[5 * ng]
    qf, kf, vf, of, lf = refs[5 * ng + 1:]
    c = pl.program_id(1)
    qi = lax.broadcasted_iota(I32, (QB, 2 * QB), 0)
    kj = lax.broadcasted_iota(I32, (QB, 2 * QB), 1)
    bias_rest = jnp.where((kj >= qi) & (kj <= qi + QB), 0.0, NEG)
    bias_first = jnp.where((kj >= QB) & (kj <= qi + QB), 0.0, NEG)
    bias_n0 = jnp.where(c == 0, bias_first, bias_rest)

    for g, (_, d) in enumerate(DIL_PATTERNS):
        q_ref, k_ref, kp_ref, v_ref, vp_ref = ins[g]
        nb = ch // (QB * d)
        if d > 1:
            qf[...] = q_ref[...].astype(F32)
            kf[0:ch] = kp_ref[...].astype(F32)
            kf[ch:2 * ch] = k_ref[...].astype(F32)
            vf[0:ch] = vp_ref[...].astype(F32)
            vf[ch:2 * ch] = v_ref[...].astype(F32)

        def rows(r, n):
            return pl.ds(r + n * QB * d, QB, stride=d) if d > 1 else pl.ds(n * QB, QB)

        def window(src, prev, cur, r, n):
            if d > 1:
                return src[pl.ds(ch + r + (n - 1) * QB * d, 2 * QB, stride=d), :].astype(BF16)
            if n == 0:
                return jnp.concatenate([prev[ch - QB:ch, :], cur[0:QB, :]], axis=0)
            return cur[(n - 1) * QB:(n + 1) * QB, :]

        blocks = [(r, n) for r in range(d) for n in range(nb)]
        logits = []
        for r, n in blocks:
            qb = qf[rows(r, n), :].astype(BF16) if d > 1 else q_ref[rows(r, n), :]
            s = _dot_nt(qb, window(kf, kp_ref, k_ref, r, n)) * ATTN_SCALE
            logits.append(s + (bias_n0 if n == 0 else bias_rest))
        stats = []
        for s in logits:
            m = jnp.max(s, axis=-1, keepdims=True)
            p = jnp.exp(s - m)
            stats.append((m, p, jnp.sum(p, axis=-1, keepdims=True)))
        for (r, n), (m, p, den) in zip(blocks, stats):
            pv = _dot(p.astype(BF16), window(vf, vp_ref, v_ref, r, n))
            of[g, rows(r, n), :] = pv / den
            lf[g, rows(r, n), :] = jnp.broadcast_to(m + jnp.log(den), (QB, HEAD_DIM))

    a = [lf[g] for g in range(ng)]
    mx = functools.reduce(jnp.maximum, a)
    e = [jnp.exp(x - mx) for x in a]
    num = sum(e[g] * of[g] for g in range(ng))
    o_ref[...] = (num / sum(e)).astype(o_ref.dtype)


def _mixer_a(qk_arr, v_arr, *, S):
    G = A_HEADS_PER_GROUP
    ch = min(A_CHUNK, S)
    assert S % ch == 0 and all(w // d == QB and ch % (QB * d) == 0 for w, d in DIL_PATTERNS)
    in_specs, args = [], []
    for g in range(len(DIL_PATTERNS)):
        qc, kc, vc = g * G, A_HEADS + g * G, g * G
        in_specs += [pl.BlockSpec((ch, HEAD_DIM), lambda h, c, o=qc: (c, o + h)),
                     pl.BlockSpec((ch, HEAD_DIM), lambda h, c, o=kc: (c, o + h)),
                     pl.BlockSpec((ch, HEAD_DIM), lambda h, c, o=kc: (jnp.maximum(c - 1, 0), o + h)),
                     pl.BlockSpec((ch, HEAD_DIM), lambda h, c, o=vc: (c, o + h)),
                     pl.BlockSpec((ch, HEAD_DIM), lambda h, c, o=vc: (jnp.maximum(c - 1, 0), o + h))]
        args += [qk_arr, qk_arr, qk_arr, v_arr, v_arr]
    ng = len(DIL_PATTERNS)
    return pl.pallas_call(
        functools.partial(_mixer_a_body, ch=ch),
        grid=(G, S // ch),
        in_specs=in_specs,
        out_specs=pl.BlockSpec((ch, HEAD_DIM), lambda h, c: (c, h)),
        out_shape=jax.ShapeDtypeStruct((S, G * HEAD_DIM), BF16),
        scratch_shapes=[pltpu.VMEM((ch, HEAD_DIM), F32), pltpu.VMEM((2 * ch, HEAD_DIM), F32),
                        pltpu.VMEM((2 * ch, HEAD_DIM), F32), pltpu.VMEM((ng, ch, HEAD_DIM), F32),
                        pltpu.VMEM((ng, ch, HEAD_DIM), F32)],
        compiler_params=_params(("parallel", "parallel")),
        name="dilated_attn",
    )(*args)


def _softmax_step(t, m):
    m_new = jnp.maximum(m, jnp.max(t, axis=0, keepdims=True))
    return m_new, jnp.exp2(m - m_new), jnp.exp2(t - m_new).astype(BF16)


def _pv(vt_ref, tile0, rows, p):
    ones = jnp.ones((DEN_ROWS, VT), BF16)
    out = None
    for c in range(p.shape[0] // VT):
        v_aug = jnp.concatenate([vt_ref[tile0 + c, rows, :], ones], axis=0)
        part = _dot(v_aug, p[c * VT:(c + 1) * VT])
        out = part if out is None else out + part
    return out


def _retire_then_softmax(ts, ms, alphas, j, slots, vt_ref, head_rows, p_ref, acc_ref):
    read_slot, write_slot = slots
    tiles_per_key_tile = p_ref.shape[2] // VT
    j_prev = jnp.maximum(j - 1, 0)
    for h, rows in enumerate(head_rows):
        acc_ref[h] = alphas[h] * acc_ref[h] + _pv(vt_ref, j_prev * tiles_per_key_tile, rows, p_ref[read_slot, h])
    steps = [_softmax_step(ts[h], ms[h]) for h in range(len(head_rows))]
    for h, (_, _, p) in enumerate(steps):
        p_ref[write_slot, h] = p
    return tuple(s[0] for s in steps), tuple(s[1] for s in steps)


def _pipelined_sweep(n_plain, plain_stage, final_stage, vt_ref, head_rows, p_ref, acc_ref, width):
    n_heads = len(head_rows)
    p_ref[...] = jnp.zeros(p_ref.shape, p_ref.dtype)
    acc_ref[...] = jnp.zeros(acc_ref.shape, F32)
    carry = (tuple(jnp.full((1, width), M_INIT, F32) for _ in range(n_heads)),
             tuple(jnp.ones((1, width), F32) for _ in range(n_heads)))
    odd = lax.rem(n_plain, 2)
    carry = lax.cond(odd == 1, lambda c: plain_stage(0, c, (0, 1)), lambda c: c, carry)

    def pair(m, c):
        j = odd + 2 * m
        return plain_stage(j + 1, plain_stage(j, c, (1, 0)), (0, 1))

    carry = lax.fori_loop(0, n_plain // 2, pair, carry)
    _, alphas = final_stage(n_plain, carry, (1, 0))
    tiles_per_key_tile = p_ref.shape[2] // VT
    for h, rows in enumerate(head_rows):
        acc_ref[h] = alphas[h] * acc_ref[h] + _pv(vt_ref, n_plain * tiles_per_key_tile, rows, p_ref[0, h])


def _causal_bias(j, q0, tk, tq):
    kpos = j * tk + lax.broadcasted_iota(I32, (tk, tq), 0)
    qpos = q0 + lax.broadcasted_iota(I32, (tk, tq), 1)
    return jnp.where(kpos <= qpos, 0.0, NEG)


def _forget_cumsum_body(f_ref, b_ref, g_ref, *, rows_per_head):
    x = f_ref[...] + b_ref[...]
    x = jnp.minimum(x, 0.0) - jnp.log1p(jnp.exp(-jnp.abs(x)))
    lane = lax.broadcasted_iota(I32, x.shape, 1)
    s = 1
    while s < LANE:
        x = x + jnp.where(lane >= s, pltpu.roll(x, s, axis=1), 0.0)
        s *= 2
    tot = jnp.broadcast_to(x[:, LANE - 1:LANE], x.shape)
    row = lax.broadcasted_iota(I32, x.shape, 0) % rows_per_head
    t = tot
    s = 1
    while s < rows_per_head:
        t = t + jnp.where(row >= s, pltpu.roll(t, s, axis=0), 0.0)
        s *= 2
    c = x + (t - tot)
    gam = c * (-LOG2E)
    hi = gam.astype(BF16)
    r1 = gam - hi.astype(F32)
    mid = r1.astype(BF16)
    lo = (r1 - mid.astype(F32)).astype(BF16)
    g_ref[0] = hi
    g_ref[1] = mid
    g_ref[2] = lo


def _forget_decay(f_t, b_f):
    H, S = f_t.shape
    rph = S // LANE
    rows = H * rph
    b_col = jnp.repeat(b_f.astype(F32), rph).reshape(rows, 1)
    g = pl.pallas_call(
        functools.partial(_forget_cumsum_body, rows_per_head=rph),
        out_shape=jax.ShapeDtypeStruct((3, rows, LANE), BF16),
        name="forget_cumsum",
    )(f_t.reshape(rows, LANE), b_col)
    return g.reshape(3, H, S)


def _fox_body(q_ref, k_ref, vt_ref, o_ref, acc_ref, *, t, ka):
    i = pl.program_id(0)
    H = B_HEADS
    hd = [slice(h * HEAD_DIM, (h + 1) * HEAD_DIM) for h in range(H)]

    acc_ref[...] = jnp.zeros(acc_ref.shape, F32)

    def tile(j, ms, masked):
        rows = pl.ds(pl.multiple_of(j * t, t), t)
        ts = [_dot(k_ref[rows, h * ka:(h + 1) * ka], q_ref[h * ka:(h + 1) * ka, :]) for h in range(H)]
        bias = _causal_bias(j, i * t, t, t) if masked else None
        steps = [_softmax_step(ts[h] + bias if masked else ts[h], ms[h]) for h in range(H)]
        for h, (_, alpha, p) in enumerate(steps):
            acc_ref[h] = alpha * acc_ref[h] + _pv(vt_ref, j * (t // VT), hd[h], p)
        return tuple(s[0] for s in steps)

    ms = lax.fori_loop(0, i, lambda j, c: tile(j, c, False), tuple(jnp.full((1, t), M_INIT, F32) for _ in range(H)))
    tile(i, ms, True)
    for h in range(H):
        o_ref[h * HEAD_DIM:(h + 1) * HEAD_DIM, :] = (
            acc_ref[h, :HEAD_DIM, :] / acc_ref[h, HEAD_DIM:HEAD_DIM + 1, :]).astype(o_ref.dtype)


def _fox(q_aug_t, k_aug, v_t, v_blk, *, S):
    tq, tk = min(TQ, S), min(TK, S)
    assert tq == tk
    H = B_HEADS
    ka = k_aug.shape[1] // H
    W = H * HEAD_DIM
    return pl.pallas_call(
        functools.partial(_fox_body, t=tq, ka=ka),
        grid=(S // tq,),
        in_specs=[pl.BlockSpec((H * ka, tq), lambda i: (0, i)),
                  _resident((S, H * ka), lambda i: (0, 0)),
                  _resident((S // VT, W, VT), lambda i: (0, v_blk, 0))],
        out_specs=pl.BlockSpec((W, tq), lambda i: (0, i)),
        out_shape=jax.ShapeDtypeStruct((W, S), BF16),
        scratch_shapes=[pltpu.VMEM((H, PV_ROWS, tq), F32)],
        compiler_params=_params(("parallel",)),
        name="fox_attn",
    )(q_aug_t, k_aug, v_t)


def _dsa_body(qi_ref, kidx_ref, misc_ref, q_ref, k_ref, vt_ref, o_ref, keys_ref, half_ref, acc_ref,
              *, tq, tk, ts, topk, w_row):
    i = pl.program_id(0)
    q0 = i * tq
    H = C_HEADS
    n_tiles = (q0 + tq + tk - 1) // tk
    n_full = q0 // tk
    ns_all = (q0 + tq) // ts
    ns_full = q0 // ts
    w = misc_ref[w_row:w_row + IDX_HEADS, :] * (IDX_DIM ** -0.5 * IDX_HEADS ** -0.5)

    def score_tile(j, masked):
        rows = pl.ds(pl.multiple_of(j * ts, ts), ts)
        kt = kidx_ref[rows, :]
        acc = jnp.zeros((ts, tq), F32)
        for h in range(IDX_HEADS):
            rel = _dot(kt, qi_ref[h * IDX_DIM:(h + 1) * IDX_DIM, :])
            acc = acc + jnp.maximum(rel, 0.0) * w[h:h + 1, :]
        s = acc + 0.0
        if masked:
            kpos = j * ts + lax.broadcasted_iota(I32, (ts, tq), 0)
            qpos = q0 + lax.broadcasted_iota(I32, (ts, tq), 1)
            s = jnp.where(kpos <= qpos, s, NEG)
        bits = pltpu.bitcast(s, I32)
        key = bits ^ (lax.shift_right_arithmetic(bits, 31) & 0x7FFFFFFF)
        keys_ref[rows, :] = key
        half_ref[rows, :] = lax.shift_right_arithmetic(key, 16).astype(I16)

    def _s_full(j, c):
        score_tile(j, False)
        return c

    def _s_mask(j, c):
        score_tile(j, True)
        return c

    lax.fori_loop(0, ns_full, _s_full, 0)
    lax.fori_loop(ns_full, ns_all, _s_mask, 0)

    @pl.when(ns_all * ts < n_tiles * tk)
    def _():
        pad_rows = pl.ds(pl.multiple_of(ns_all * ts, ts), ts)
        keys_ref[pad_rows, :] = jnp.full((ts, tq), NEG_KEY, I32)
        half_ref[pad_rows, :] = jnp.full((ts, tq), NEG_KEY >> 16, I16)

    kf = float(topk)
    grp = 2 * 16

    def count16(cand):
        cand = cand.astype(I16)

        def body(c, acc):
            t = half_ref[pl.ds(pl.multiple_of(c * tk, tk), tk), :]
            ind = jnp.where(t >= cand, jnp.ones((), I16), jnp.zeros((), I16))
            parts = [ind[r * grp:(r + 1) * grp] for r in range(tk // grp)]
            while len(parts) > 1:
                parts = [parts[k] + parts[k + 1] for k in range(0, len(parts), 2)]
            return acc + parts[0].astype(F32)
        acc = lax.fori_loop(0, n_tiles, body, jnp.zeros((grp, tq), F32))
        return jnp.sum(acc, axis=0, keepdims=True)

    def bisect16(lo):
        def step(b, lo):
            cand = lo + lax.shift_left(jnp.int32(1), 14 - b)
            return jnp.where(count16(cand) >= kf, cand, lo)
        return lax.fori_loop(0, 15, step, lo)

    lo16 = bisect16(jnp.where(count16(jnp.zeros((1, tq), I32)) >= kf, 0, -(2 ** 15)).astype(I32))
    n_acc = 4

    def count_ge(cand):
        def body(c, acc):
            t = keys_ref[pl.ds(pl.multiple_of(c * tk, tk), tk), :]
            ind = jnp.where(t >= cand, 1.0, 0.0)
            part = jnp.sum(ind.reshape(tk // (n_acc * SUBLANE), n_acc * SUBLANE, tq), axis=0)
            return acc + part
        acc = lax.fori_loop(0, n_tiles, body, jnp.zeros((n_acc * SUBLANE, tq), F32))
        return jnp.sum(acc, axis=0, keepdims=True)

    def fill_low(c, carry):
        rows = pl.ds(pl.multiple_of(c * tk, tk), tk)
        key = keys_ref[rows, :]
        k16 = lax.shift_right_arithmetic(key, 16)
        low = (key & 0xFFFF) - 2 ** 15
        val = jnp.where(k16 > lo16, 2 ** 15 - 1, jnp.where(k16 == lo16, low, -(2 ** 15)))
        half_ref[rows, :] = val.astype(I16)
        return carry

    lax.fori_loop(0, n_tiles, fill_low, 0)
    low16 = bisect16(jnp.where(count16(jnp.zeros((1, tq), I32)) >= kf, 0, -(2 ** 15)).astype(I32))
    thr = lax.shift_left(lo16, 16) | (low16 + 2 ** 15)

    n_ge = count_ge(thr)
    tied = jnp.where((n_ge > kf) & (thr != NEG_KEY), 1.0, 0.0)
    has_ties = jnp.max(tied) > 0.0

    @pl.when(has_ties)
    def _():
        need = kf - count_ge(thr + 1)
        tri = (lax.broadcasted_iota(I32, (tk, tk), 0) >= lax.broadcasted_iota(I32, (tk, tk), 1)).astype(BF16)

        def demote(j, seen):
            rows = pl.ds(pl.multiple_of(j * tk, tk), tk)
            key = keys_ref[rows, :]
            eq = jnp.where(key == thr, 1.0, 0.0)
            rank = _dot(tri, eq.astype(BF16)) + seen
            keys_ref[rows, :] = jnp.where((eq > 0.5) & (rank > need), INT_MIN, key)
            return seen + jnp.sum(eq, axis=0, keepdims=True)

        lax.fori_loop(0, n_tiles, demote, jnp.zeros((1, tq), F32))

    def to_bias(j, masked):
        rows = pl.ds(pl.multiple_of(j * tk, tk), tk)
        sel = keys_ref[rows, :] >= thr
        if masked:
            kpos = j * tk + lax.broadcasted_iota(I32, (tk, tq), 0)
            qpos = q0 + lax.broadcasted_iota(I32, (tk, tq), 1)
            sel = sel & (kpos <= qpos)
        keys_ref[rows, :] = pltpu.bitcast(jnp.where(sel, 0.0, NEG), I32)

    def _b_full(j, c):
        to_bias(j, False)
        return c

    def _b_mask(j, c):
        to_bias(j, True)
        return c

    lax.fori_loop(0, n_full, _b_full, 0)
    lax.fori_loop(n_full, n_tiles, _b_mask, 0)

    hd = [slice(h * HEAD_DIM, (h + 1) * HEAD_DIM) for h in range(H)]

    acc_ref[...] = jnp.zeros(acc_ref.shape, F32)

    def tile(j, ms):
        rows = pl.ds(pl.multiple_of(j * tk, tk), tk)
        ts = [_dot(k_ref[rows, hd[h]], q_ref[hd[h], :]) for h in range(H)]
        bias = pltpu.bitcast(keys_ref[rows, :], F32)
        steps = [_softmax_step(ts[h] + bias, ms[h]) for h in range(H)]
        for h, (_, alpha, p) in enumerate(steps):
            acc_ref[h] = alpha * acc_ref[h] + _pv(vt_ref, j * (tk // VT), hd[h], p)
        return tuple(s[0] for s in steps)

    lax.fori_loop(0, n_tiles, tile, tuple(jnp.full((1, tq), M_INIT, F32) for _ in range(H)))
    for h in range(H):
        den = jnp.maximum(acc_ref[h, HEAD_DIM:HEAD_DIM + 1, :], TINY)
        o_ref[h * HEAD_DIM:(h + 1) * HEAD_DIM, :] = (acc_ref[h, :HEAD_DIM, :] / den).astype(o_ref.dtype)


def _dsa(qi_t, kidx, misc_t, w_row, q_t, q_blk, k_arr, k_blk, v_t, v_blk, *, S):
    tq, tk = min(TQ, S), min(TK, S)
    ts = tk
    topk = min(DSA_TOPK, S // 4)
    W = C_HEADS * HEAD_DIM
    assert S % tk == 0 and tk % ts == 0 and tq % ts == 0
    kern = functools.partial(_dsa_body, tq=tq, tk=tk, ts=ts, topk=topk, w_row=w_row)
    return pl.pallas_call(
        kern,
        grid=(S // tq,),
        in_specs=[pl.BlockSpec((IDX_HEADS * IDX_DIM, tq), lambda i: (0, i)),
                  _resident((S, IDX_DIM), lambda i: (0, 0)),
                  pl.BlockSpec((LANE, tq), lambda i: (0, i)),
                  pl.BlockSpec((W, tq), lambda i: (q_blk, i)),
                  _resident((S, W), lambda i: (0, k_blk)),
                  _resident((S // VT, W, VT), lambda i: (0, v_blk, 0))],
        out_specs=pl.BlockSpec((W, tq), lambda i: (0, i)),
        out_shape=jax.ShapeDtypeStruct((W, S), BF16),
        scratch_shapes=[pltpu.VMEM((S, tq), I32), pltpu.VMEM((S, tq), I16),
                        pltpu.VMEM((C_HEADS, PV_ROWS, tq), F32)],
        compiler_params=_params(("arbitrary",)),
        name="dsa_attn",
    )(qi_t, kidx, misc_t, q_t, k_arr, v_t)


def _compress_body(x_ref, pea_ref, peb_ref, w1a_ref, w1b_ref, w2_ref, o_ref, ot_ref):
    x = x_ref[...].astype(F32)
    a = _dot((x + pea_ref[...]).astype(BF16), w1a_ref[...].astype(BF16))
    b = _dot((x + peb_ref[...]).astype(BF16), w1b_ref[...].astype(BF16))
    n = a.shape[0]
    hid = a + pltpu.roll(b, n - 1, axis=0)
    out = _dot(jax.nn.gelu(hid).astype(BF16), w2_ref[...].astype(BF16))
    o_ref[...] = out.astype(o_ref.dtype)
    ot_ref[...] = out.T.astype(ot_ref.dtype)


def _compress(x2, pe, w1, w2):
    _, S, Dh = x2.shape
    n = S // CMP_STRIDE
    half = CMP_STRIDE * Dh
    xr = x2.reshape(2, n, half)
    pe_a = pe[:, :CMP_STRIDE].reshape(2, 1, half)
    pe_b = pe[:, CMP_STRIDE:].reshape(2, 1, half)
    return pl.pallas_call(
        _compress_body,
        grid=(2,),
        in_specs=[pl.BlockSpec((None, n, half), lambda g: (g, 0, 0)),
                  pl.BlockSpec((None, 1, half), lambda g: (g, 0, 0)),
                  pl.BlockSpec((None, 1, half), lambda g: (g, 0, 0)),
                  pl.BlockSpec((None, half, CMP_HIDDEN), lambda g: (g, 0, 0)),
                  pl.BlockSpec((None, half, CMP_HIDDEN), lambda g: (g, 1, 0)),
                  pl.BlockSpec((None, CMP_HIDDEN, Dh), lambda g: (g, 0, 0))],
        out_specs=[pl.BlockSpec((None, n, Dh), lambda g: (g, 0, 0)),
                   pl.BlockSpec((None, Dh, n), lambda g: (g, 0, 0))],
        out_shape=[jax.ShapeDtypeStruct((2, n, Dh), BF16), jax.ShapeDtypeStruct((2, Dh, n), BF16)],
        compiler_params=_params(("parallel",)),
        name="nsa_compress",
    )(xr, pe_a, pe_b, w1, w1, w2)


def _nsa_body(q_ref, kc_ref, vct_ref, ks_ref, kw_ref, vst_ref, vwt_ref, misc_ref, ov_ref, o_ref, acc_ref, p_ref,
              qs_ref,
              *, tq, tk, n_slc, n_sel, ncp, gate_row):
    i = pl.program_id(0)
    q0 = i * tq
    H = D_HEADS
    R = H * tq
    blk_shift = int(math.log2(SLC_BLOCK))
    qs = jnp.concatenate([q_ref[h * HEAD_DIM:(h + 1) * HEAD_DIM, :] for h in range(H)], axis=1)

    def qpos_of(shape):
        return q0 + (lax.broadcasted_iota(I32, shape, 1) & (tq - 1))

    lc = _dot(kc_ref[...], qs)
    cend = lax.broadcasted_iota(I32, (ncp, R), 0) * CMP_STRIDE + (CMP_LEN - 1)
    mc = cend <= qpos_of((ncp, R))
    lc = jnp.where(mc, lc, NEG)
    mx = jnp.max(lc, axis=0, keepdims=True)
    pc = jnp.where(mc, jnp.exp2(lc - mx), 0.0)
    pc = pc * (1.0 / jnp.maximum(jnp.sum(pc, axis=0, keepdims=True), TINY))
    o_cmp = _dot(vct_ref[...], pc.astype(BF16))

    psum = pc[:, 0:tq]
    for h in range(1, H):
        psum = psum + pc[:, h * tq:(h + 1) * tq]
    p_hi = psum.astype(BF16)
    p_lo = (psum - p_hi.astype(F32)).astype(BF16)
    ov = ov_ref[...]
    imp = _dot(ov, p_hi) + _dot(ov, p_lo)
    jblk = lax.broadcasted_iota(I32, (n_slc, tq), 0)
    jblk_f = jblk.astype(F32)
    qpos_l = q0 + lax.broadcasted_iota(I32, (n_slc, tq), 1)
    cur = lax.shift_right_logical(qpos_l, blk_shift)
    forced = (jblk == 0) | (jblk == cur) | (jblk == cur - 1)
    val = jnp.where(forced, FORCE, jnp.where(jblk * SLC_BLOCK <= qpos_l, imp, NEG))
    sel_t = jnp.zeros((n_slc, tq), F32)
    for _ in range(n_sel):
        mxv = jnp.max(val, axis=0, keepdims=True)
        first = jnp.min(jnp.where(val == mxv, jblk_f, float(n_slc)), axis=0, keepdims=True)
        hit = jblk_f == first
        sel_t = jnp.where(hit, 1.0, sel_t)
        val = jnp.where(hit, -jnp.inf, val)
    sel_b = ((sel_t - 1.0) * (-NEG)).astype(BF16)

    bpt = tk // SLC_BLOCK
    n_tiles = (q0 + tq + tk - 1) // tk
    n_full = q0 // tk
    row_blk = lax.shift_right_logical(lax.broadcasted_iota(I32, (tk, n_slc), 0), blk_shift)
    col_blk = lax.broadcasted_iota(I32, (tk, n_slc), 1)

    n_ch = NSA_CHAINS
    cw = R // n_ch
    chains = [slice(c * cw, (c + 1) * cw) for c in range(n_ch)]
    qs_ref[...] = qs

    all_rows = [slice(None)] * n_ch

    def slc_stage(j, carry, slots, masked):
        ms, alphas = carry
        rows = pl.ds(pl.multiple_of(j * tk, tk), tk)
        ks = ks_ref[rows, :]
        ts = [_dot(ks, qs_ref[:, c]) for c in chains]
        expand = jnp.where(row_blk == col_blk - j * bpt, 1.0, 0.0).astype(BF16)
        bias = _dot(expand, sel_b)
        if masked:
            kpos = j * tk + lax.broadcasted_iota(I32, (tk, tq), 0)
            qpos = q0 + lax.broadcasted_iota(I32, (tk, tq), 1)
            bias = jnp.where(kpos <= qpos, bias, NEG)
        bias = jnp.concatenate([bias] * (H // n_ch), axis=1)
        ts = [x + bias for x in ts]
        return _retire_then_softmax(ts, ms, alphas, j, slots, vst_ref, all_rows, p_ref, acc_ref)

    _pipelined_sweep(n_full, functools.partial(slc_stage, masked=False), functools.partial(slc_stage, masked=True),
                     vst_ref, all_rows, p_ref, acc_ref, cw)
    o_slc = jnp.concatenate(
        [acc_ref[c, :HEAD_DIM, :] * (1.0 / acc_ref[c, HEAD_DIM:HEAD_DIM + 1, :]) for c in range(n_ch)], axis=1)

    wlen = WIN + tq
    start = pl.multiple_of(jnp.maximum(q0 - WIN, 0), tq)
    lw = _dot(kw_ref[pl.ds(start, wlen), :], qs)
    dist = qpos_of((wlen, R)) - (start + lax.broadcasted_iota(I32, (wlen, R), 0))
    lw = jnp.where((dist >= 0) & (dist < WIN), lw, NEG)
    mxw = jnp.max(lw, axis=0, keepdims=True)
    pw = jnp.exp2(lw - mxw).astype(BF16)
    o_win = _pv(vwt_ref, start // VT, slice(None), pw)
    o_win = o_win[:HEAD_DIM] * (1.0 / o_win[HEAD_DIM:HEAD_DIM + 1])

    gates = jax.nn.sigmoid(misc_ref[gate_row:gate_row + 3 * H, :])
    for h in range(H):
        c = slice(h * tq, (h + 1) * tq)
        out = (gates[3 * h:3 * h + 1] * o_cmp[:, c] + gates[3 * h + 1:3 * h + 2] * o_slc[:, c]
               + gates[3 * h + 2:3 * h + 3] * o_win[:, c])
        o_ref[h * HEAD_DIM:(h + 1) * HEAD_DIM, :] = out.astype(o_ref.dtype)


def _nsa(q_t, q_blk, kc, vct, k_arr, ks_blk, kw_blk, v_t, vs_blk, vw_blk, misc_t, gate_row, *, S):
    tq, tk = min(TQ_NSA, S), min(TK, S)
    assert S >= WIN + tq and tq & (tq - 1) == 0 and tq == VT and WIN % VT == 0
    n_slc = S // SLC_BLOCK
    n_sel = min(SLC_COUNT, n_slc)
    ncp = S // CMP_STRIDE
    W = D_HEADS * HEAD_DIM
    ci = np.arange(ncp)[None, :]
    sj = np.arange(n_slc)[:, None]
    ov = ((ci * CMP_STRIDE < (sj + 1) * SLC_BLOCK) & (ci * CMP_STRIDE + CMP_LEN > sj * SLC_BLOCK)
          & (ci < ncp - 1))
    ov = jnp.asarray(ov, BF16)
    kern = functools.partial(_nsa_body, tq=tq, tk=tk, n_slc=n_slc, n_sel=n_sel, ncp=ncp, gate_row=gate_row)
    return pl.pallas_call(
        kern,
        grid=(S // tq,),
        in_specs=[pl.BlockSpec((W, tq), lambda i: (q_blk, i)),
                  _resident((ncp, HEAD_DIM), lambda i: (0, 0)),
                  _resident((HEAD_DIM, ncp), lambda i: (0, 0)),
                  _resident((S, HEAD_DIM), lambda i: (0, ks_blk)),
                  _resident((S, HEAD_DIM), lambda i: (0, kw_blk)),
                  _resident((S // VT, HEAD_DIM, VT), lambda i: (0, vs_blk, 0)),
                  _resident((S // VT, HEAD_DIM, VT), lambda i: (0, vw_blk, 0)),
                  pl.BlockSpec((LANE, tq), lambda i: (0, i)),
                  _resident((n_slc, ncp), lambda i: (0, 0))],
        out_specs=pl.BlockSpec((W, tq), lambda i: (0, i)),
        out_shape=jax.ShapeDtypeStruct((W, S), BF16),
        scratch_shapes=[pltpu.VMEM((NSA_CHAINS, PV_ROWS, D_HEADS * tq // NSA_CHAINS), F32),
                        pltpu.VMEM((2, NSA_CHAINS, tk, D_HEADS * tq // NSA_CHAINS), BF16),
                        pltpu.VMEM((HEAD_DIM, D_HEADS * tq), BF16)],
        compiler_params=_params(("parallel",)),
        name="nsa_attn",
    )(q_t, kc, vct, k_arr, k_arr, v_t, v_t, misc_t, ov)


def _rope_tables(S):
    f32 = np.float32
    pos = np.arange(S, dtype=f32)[:, None]

    def cos_sin(half):
        inv_freq = np.power(f32(ROPE_THETA), -np.arange(half, dtype=f32) / f32(half), dtype=f32)
        ang = pos * inv_freq[None, :]
        return np.cos(ang, dtype=f32), np.sin(ang, dtype=f32)

    cos, sin = cos_sin(HEAD_DIM // 2)
    t128 = (np.concatenate([cos, cos], 1), np.concatenate([-sin, sin], 1))
    cos, sin = cos_sin(IDX_DIM // 2)
    z = np.zeros_like(sin)
    t64 = (np.concatenate([cos] * 4, 1), np.concatenate([-sin, z, -sin, z], 1),
           np.concatenate([z, sin, z, sin], 1))
    return tuple(jnp.asarray(t) for t in t128), tuple(jnp.asarray(t) for t in t64)


def _split_w_in(w_in):
    offs = np.cumsum([0,
                      A_HEADS * HEAD_DIM, A_HEADS * HEAD_DIM, A_HEADS * HEAD_DIM,
                      B_HEADS * HEAD_DIM, B_HEADS * HEAD_DIM, B_HEADS * HEAD_DIM, B_HEADS,
                      C_HEADS * HEAD_DIM, C_HEADS * HEAD_DIM, C_HEADS * HEAD_DIM,
                      IDX_HEADS * IDX_DIM, IDX_DIM, IDX_HEADS,
                      D_HEADS * HEAD_DIM] + [HEAD_DIM] * 6 + [3 * D_HEADS])
    names = ["aq", "ak", "av", "bq", "bk", "bv", "bf", "cq", "ck", "cv", "cqi", "cki", "cwi",
             "dq", "dkc", "dvc", "dks", "dvs", "dkw", "dvw", "dg"]
    col = {n: w_in[:, offs[k]:offs[k + 1]] for k, n in enumerate(names)}
    D = w_in.shape[0]

    def pack(ns, pad=0):
        parts = [col[n] for n in ns]
        if pad:
            parts.append(jnp.zeros((D, pad), w_in.dtype))
        return jnp.concatenate(parts, axis=1).astype(BF16)

    n_misc = B_HEADS + IDX_HEADS + 3 * D_HEADS
    return dict(
        a_qk=pack(["aq", "ak"]),
        a_v=pack(["av"]),
        b_q=pack(["bq"]),
        b_k=pack(["bk", "dvc"]),
        cd_k=pack(["ck", "dkc", "dks", "dkw"]),
        cd_q=pack(["cq", "dq"]),
        idx=pack(["cqi", "cki"], pad=LANE - IDX_DIM),
        v_t=pack(["bv", "cv", "dvs", "dvw"]),
        misc=pack(["bf", "cwi", "dg"], pad=LANE - n_misc),
    )


def _pick(n, cands):
    for c in cands:
        if n % c == 0:
            return c
    return n


def _layer(x, p, stacked, layer, tables):
    S, D = x.shape
    (cos128, sin128), (cos64, sin64a, sin64b) = tables
    tm = _pick(S, (1024, 512, 256))
    h = _rmsnorm(x, p["norm_mix"], BF16)
    w = _split_w_in(p["w_in"])

    def proj(name, mode="plain", extras=(), out_dtype=BF16, layout="rows", scale=None):
        n = w[name].shape[1]
        tn = _pick(n, (1280, 1152, 1024, 896, 768, 640, 512, 384, 256, 128))
        return _mm(h, w[name], tm=tm, tn=tn, out_dtype=out_dtype, mode=mode, extras=extras, layout=layout,
                   name="in_" + name, scale=scale)

    a_qk = proj("a_qk", "rope128", (cos128, sin128))
    a_v = proj("a_v")
    b_k = proj("b_k")
    b_q_t = proj("b_q", layout="cols", scale=EXP_C)
    cd_k = proj("cd_k", "rope128", (cos128, sin128))
    cd_q_t = proj("cd_q", "rope128", (cos128, sin128), layout="cols", scale=EXP_C)
    idx_t = proj("idx", "rope64", (cos64, sin64a, sin64b), layout="cols")
    v_t = proj("v_t", layout="coltiles")
    misc_t = proj("misc", out_dtype=F32, layout="cols")

    o_a = _mixer_a(a_qk, a_v, S=S)

    BW = B_HEADS * HEAD_DIM
    gam = _forget_decay(misc_t[:B_HEADS], p["b_f"])
    pad = HEAD_DIM - 3
    bq_t = b_q_t.reshape(B_HEADS, HEAD_DIM, S)
    bk = b_k[:, :BW].reshape(S, B_HEADS, HEAD_DIM)
    q_aug_t = jnp.concatenate([bq_t, jnp.ones((B_HEADS, 3, S), BF16), jnp.zeros((B_HEADS, pad, S), BF16)], axis=1)
    k_aug = jnp.concatenate([bk, gam.transpose(2, 1, 0), jnp.zeros((S, B_HEADS, pad), BF16)], axis=2)
    o_b_t = _fox(q_aug_t.reshape(-1, S), k_aug.reshape(S, -1), v_t, 0, S=S)

    n_qi = IDX_HEADS * IDX_DIM
    kidx = idx_t[n_qi:n_qi + IDX_DIM].T
    o_c_t = _dsa(idx_t, kidx, misc_t, B_HEADS, cd_q_t, 0, cd_k, 0, v_t, 1, S=S)

    cw = C_HEADS * HEAD_DIM
    nb = cw // HEAD_DIM
    kv_c = jnp.stack([cd_k[:, cw:cw + HEAD_DIM], b_k[:, BW:BW + HEAD_DIM]])
    kv_cmp, kv_cmp_t = _compress(kv_c, p["cmp_pe"], p["cmp_w1"], p["cmp_w2"])
    o_d_t = _nsa(cd_q_t, 1, kv_cmp[0], kv_cmp_t[1], cd_k, nb + 1, nb + 2, v_t, 2 * nb, 2 * nb + 1,
                 misc_t, B_HEADS + IDX_HEADS, S=S)

    tn = _pick(D, (512, 256, 128))
    nl = stacked["b_gate"].shape[0]
    bg = stacked["b_gate"].reshape(nl, N_BRANCH, 1, D)
    merged = _gate_merge(h, stacked["w_gate_bf16"], bg, (o_a, o_b_t, o_c_t, o_d_t), stacked["w_branch"], layer,
                         tm=tm, tn=tn)
    x = _mm(merged, stacked["w_out"], tm=tm, tn=tn, out_dtype=F32, mode="resid", extras=(x,), name="out_proj",
            layer=layer)

    h2 = _rmsnorm(x, p["norm_ffn"], BF16)
    Hf = stacked["w_ffn_in"].shape[-1] // 2
    act = _mm_swiglu(h2, stacked["w_ffn_in"], layer, tm=tm, tn=_pick(Hf, (512, 256, 128)))
    x = _mm(act, stacked["w_ffn_out"], tm=_pick(S, (512, 256)), tn=tn, out_dtype=F32, mode="resid",
            extras=(x,), name="ffn_out", layer=layer)
    return x


def kernel(x, norm_mix, w_in, w_gate, b_gate, b_f, cmp_pe, cmp_w1, cmp_w2, w_branch, w_out, norm_ffn,
           w_ffn_in, w_ffn_out, norm_final):
    B, S, D = x.shape
    tables = _rope_tables(S)
    nl = w_gate.shape[0]
    w_gate_bf16 = lax.optimization_barrier(w_gate.astype(BF16)).reshape(nl, D, N_BRANCH * D)
    stacked = dict(w_gate_bf16=w_gate_bf16, b_gate=b_gate, w_branch=w_branch, w_out=w_out, w_ffn_in=w_ffn_in,
                   w_ffn_out=w_ffn_out)
    outs = []
    for b in range(B):
        xb = x[b]
        for l in range(norm_mix.shape[0]):
            p = dict(norm_mix=norm_mix[l], w_in=w_in[l], b_f=b_f[l], cmp_pe=cmp_pe[l], cmp_w1=cmp_w1[l],
                     cmp_w2=cmp_w2[l], norm_ffn=norm_ffn[l])
            xb = _layer(xb, p, stacked, l, tables)
        outs.append(_rmsnorm(xb, norm_final, x.dtype))
    return jnp.stack(outs)
```
